```python
import math
import jax
import jax.numpy as jnp
from jax import lax
import numpy as np

D_MODEL = 2048
BATCH = 1
SEQ = 8192
DEPTH = 4

N_A_LAYERS = DEPTH // 2
N_B_LAYERS = DEPTH - N_A_LAYERS
PLE_DIM = 256
FFN_DIM = -(-8 * D_MODEL // (3 * 256)) * 256

GMLP_CHUNK = 128
GMLP_WIDTH = D_MODEL
GMLP_GROUP_DIM = 128
GMLP_GROUPS = GMLP_WIDTH // GMLP_GROUP_DIM

HEAD_DIM = 128
N_HEADS = D_MODEL // HEAD_DIM
N_KV_GROUPS = 2
HEADS_PER_GROUP = N_HEADS // N_KV_GROUPS
N_BRANCH = 3
CMP_BLOCK = 32
CMP_STRIDE = 16
CMP_HIDDEN = 256
SLC_BLOCK = 64
N_SELECT = 16
WINDOW = 512
Q_BLOCK = 128
N_BUCKETS = 32
MAX_DISTANCE = 128
EPS = 1e-6
NEG_INF = -1e30

kernel_name = 'hybrid_gmlp_nsa_yoco_trunk'


def rmsnorm(x, g):
    xf = x.astype(jnp.float32)
    y = xf * lax.rsqrt(jnp.mean(xf * xf, axis=-1, keepdims=True) + EPS)
    return (y * g.astype(jnp.float32)).astype(x.dtype)


def t5_bucket(dist):
    n = jnp.maximum(dist, 0)
    max_exact = N_BUCKETS // 2
    nf = jnp.maximum(n, 1).astype(jnp.float32)
    large = max_exact + (jnp.log(nf / max_exact) / math.log(MAX_DISTANCE / max_exact)
                         * (N_BUCKETS - max_exact)).astype(jnp.int32)
    large = jnp.minimum(large, N_BUCKETS - 1)
    return jnp.where(n < max_exact, n, large)


def rel_bias_heads(table, dist):
    b = table[t5_bucket(dist)]
    b = jnp.transpose(b, (2, 0, 1)).reshape(N_KV_GROUPS, HEADS_PER_GROUP, *dist.shape)
    return b.astype(jnp.float32)


def swiglu(h, w_in, w_out):
    g, u = jnp.split(h @ w_in, 2, axis=-1)
    return (jax.nn.silu(g) * u) @ w_out


def ple_add(x, p_i, norm_g, w_proj, w_gate):
    gate = jax.nn.sigmoid(rmsnorm(x, norm_g) @ w_gate)
    return x + (p_i @ w_proj) * gate


def gmlp_mixer(h, w_in, norm_v, w_s, b_s, w_out):
    B, S, _ = h.shape
    z = jax.nn.gelu(h @ w_in)
    u, v = jnp.split(z, 2, axis=-1)
    v = rmsnorm(v, norm_v)
    v = v.reshape(B, S // GMLP_CHUNK, GMLP_CHUNK, GMLP_GROUPS, GMLP_GROUP_DIM)
    causal = jnp.tril(jnp.ones((GMLP_CHUNK, GMLP_CHUNK), dtype=bool))
    ws = jnp.where(causal, w_s, 0)
    sv = jnp.einsum('gts,bcsgd->bctgd', ws, v) + jnp.transpose(b_s)[None, None, :, :, None]
    return (u * sv.reshape(B, S, GMLP_WIDTH)) @ w_out


def compress_blocks(t, pe, w1, w2):
    B, G, S, Dk = t.shape
    halves = t.reshape(B, G, S // CMP_STRIDE, CMP_STRIDE, Dk)
    blocks = jnp.concatenate([halves[:, :, :-1], halves[:, :, 1:]], axis=3) + pe
    flat = blocks.reshape(B, G, blocks.shape[2], CMP_BLOCK * Dk)
    return jax.nn.gelu(flat @ w1) @ w2


def shared_kv(x, kv_norm, kv_w, k_norm, cmp_pe_k, cmp_pe_v, cmp_wk1, cmp_wk2, cmp_wv1, cmp_wv2):
    B, S, _ = x.shape
    h = rmsnorm(x, kv_norm)
    kv = (h @ kv_w).reshape(B, S, 2 * N_BRANCH, N_KV_GROUPS, HEAD_DIM)
    kv = jnp.transpose(kv, (2, 0, 3, 1, 4))
    k_c, v_c, k_s, v_s, k_w, v_w = kv[0], kv[1], kv[2], kv[3], kv[4], kv[5]
    k_cmp = rmsnorm(compress_blocks(k_c, cmp_pe_k, cmp_wk1, cmp_wk2), k_norm[0])
    v_cmp = compress_blocks(v_c, cmp_pe_v, cmp_wv1, cmp_wv2)
    k_s = rmsnorm(k_s, k_norm[1])
    k_w = rmsnorm(k_w, k_norm[2])
    k_slc_b = k_s.reshape(B, N_KV_GROUPS, S // SLC_BLOCK, SLC_BLOCK, HEAD_DIM)
    v_slc_b = v_s.reshape(B, N_KV_GROUPS, S // SLC_BLOCK, SLC_BLOCK, HEAD_DIM)
    pad = ((0, 0), (0, 0), (WINDOW, 0), (0, 0))
    return (k_cmp, v_cmp, k_slc_b, v_slc_b, jnp.pad(k_w, pad), jnp.pad(v_w, pad))


def nsa_mixer(h, k_cmp, v_cmp, k_slc_b, v_slc_b, k_win_pad, v_win_pad, w_in, q_norm, rel_table, w_out):
    B, S, _ = h.shape
    n_qb = S // Q_BLOCK
    n_cmp = k_cmp.shape[2]
    n_slc = k_slc_b.shape[2]
    n_sel = min(N_SELECT, n_slc)
    scale = HEAD_DIM ** -0.5

    proj = h @ w_in
    q = rmsnorm(proj[..., :N_HEADS * HEAD_DIM].reshape(B, S, N_HEADS, HEAD_DIM), q_norm)
    gates = jax.nn.sigmoid(proj[..., N_HEADS * HEAD_DIM:].astype(jnp.float32))
    q_blocks = q.reshape(B, n_qb, Q_BLOCK, N_KV_GROUPS, HEADS_PER_GROUP, HEAD_DIM).transpose(1, 0, 3, 4, 2, 5)
    g_blocks = gates.reshape(B, n_qb, Q_BLOCK, N_KV_GROUPS, HEADS_PER_GROUP, N_BRANCH).transpose(1, 0, 3, 4, 2, 5)
    starts = jnp.arange(n_qb, dtype=jnp.int32) * Q_BLOCK

    cmp_end = jnp.arange(n_cmp, dtype=jnp.int32) * CMP_STRIDE + CMP_BLOCK - 1
    c0 = jnp.arange(n_cmp, dtype=jnp.int32) * CMP_STRIDE
    s0 = jnp.arange(n_slc, dtype=jnp.int32) * SLC_BLOCK
    lo = jnp.maximum(c0[:, None], s0[None, :])
    hi = jnp.minimum(c0[:, None] + CMP_BLOCK, s0[None, :] + SLC_BLOCK)
    overlap = jnp.maximum(hi - lo, 0).astype(jnp.float32) / CMP_BLOCK
    blk = jnp.arange(n_slc, dtype=jnp.int32)
    bi = jnp.arange(B)[:, None, None, None]
    gi = jnp.arange(N_KV_GROUPS)[None, :, None, None]
    gi5 = jnp.arange(N_KV_GROUPS)[None, :, None, None, None]
    hi5 = jnp.arange(HEADS_PER_GROUP)[None, None, :, None, None]
    tab = jnp.transpose(rel_table.reshape(N_BUCKETS, N_KV_GROUPS, HEADS_PER_GROUP), (1, 2, 0))

    def block_fn(args):
        qb, gb, s = args
        t = s + jnp.arange(Q_BLOCK, dtype=jnp.int32)
        lc = jnp.einsum('bghqd,bgkd->bghqk', qb, k_cmp, preferred_element_type=jnp.float32) * scale \
            + rel_bias_heads(rel_table, t[:, None] - cmp_end[None, :])
        mc = cmp_end[None, :] <= t[:, None]
        pc = jax.nn.softmax(jnp.where(mc, lc, NEG_INF), axis=-1) * jnp.any(mc, axis=-1)[:, None]
        o_cmp = jnp.einsum('bghqk,bgkd->bghqd', pc.astype(v_cmp.dtype), v_cmp)
        imp = jnp.einsum('bghqk,kj->bgqj', pc, overlap)
        cur = t // SLC_BLOCK
        forced = (blk[None, :] == 0) | (blk[None, :] == cur[:, None]) | (blk[None, :] == cur[:, None] - 1)
        valid = blk[None, :] <= cur[:, None]
        score = jnp.where(forced, jnp.inf, jnp.where(valid, imp, -jnp.inf))
        _, sel = lax.top_k(score, n_sel)
        k_sel = k_slc_b[bi, gi, sel].reshape(B, N_KV_GROUPS, Q_BLOCK, n_sel * SLC_BLOCK, HEAD_DIM)
        v_sel = v_slc_b[bi, gi, sel].reshape(B, N_KV_GROUPS, Q_BLOCK, n_sel * SLC_BLOCK, HEAD_DIM)
        pos = (sel[..., None] * SLC_BLOCK + jnp.arange(SLC_BLOCK, dtype=jnp.int32)).reshape(B, N_KV_GROUPS, Q_BLOCK, -1)
        dist = t[None, None, :, None] - pos
        bias_s = tab[gi5, hi5, t5_bucket(dist)[:, :, None]].astype(jnp.float32)
        ls = jnp.einsum('bghqd,bgqkd->bghqk', qb, k_sel, preferred_element_type=jnp.float32) * scale + bias_s
        ps = jax.nn.softmax(jnp.where(dist[:, :, None] >= 0, ls, NEG_INF), axis=-1)
        o_slc = jnp.einsum('bghqk,bgqkd->bghqd', ps.astype(v_sel.dtype), v_sel)
        kw = lax.dynamic_slice_in_dim(k_win_pad, s, WINDOW + Q_BLOCK, axis=2)
        vw = lax.dynamic_slice_in_dim(v_win_pad, s, WINDOW + Q_BLOCK, axis=2)
        posw = s - WINDOW + jnp.arange(WINDOW + Q_BLOCK, dtype=jnp.int32)
        dw = t[:, None] - posw[None, :]
        mw = (dw >= 0) & (dw < WINDOW) & (posw[None, :] >= 0)
        lw = jnp.einsum('bghqd,bgkd->bghqk', qb, kw, preferred_element_type=jnp.float32) * scale \
            + rel_bias_heads(rel_table, dw)
        pw = jax.nn.softmax(jnp.where(mw, lw, NEG_INF), axis=-1)
        o_win = jnp.einsum('bghqk,bgkd->bghqd', pw.astype(vw.dtype), vw)
        out = gb[..., 0:1] * o_cmp + gb[..., 1:2] * o_slc + gb[..., 2:3] * o_win
        return out.astype(qb.dtype)

    o = lax.map(block_fn, (q_blocks, g_blocks, starts))
    o = o.transpose(1, 0, 4, 2, 3, 5).reshape(B, S, N_HEADS * HEAD_DIM)
    return o @ w_out


def _normal(key, shape, scale):
    return scale * jax.random.normal(key, shape, jnp.float32)


def _gain(key, shape):
    return 1.0 + 0.02 * jax.random.normal(key, shape, jnp.float32)


def setup_inputs(seed: int = 0) -> dict:
    key = jax.random.key(seed)
    ks = jax.random.split(key, 27)
    q_cols = N_HEADS * HEAD_DIM + N_HEADS * N_BRANCH
    kv_cols = 2 * N_BRANCH * N_KV_GROUPS * HEAD_DIM
    return {
        'x': jax.random.normal(ks[0], (BATCH, SEQ, D_MODEL), jnp.float32),
        'p': jax.random.normal(ks[1], (DEPTH, BATCH, SEQ, PLE_DIM), jnp.float32),
        'norm_mix': _gain(ks[2], (DEPTH, D_MODEL)),
        'norm_ffn': _gain(ks[3], (DEPTH, D_MODEL)),
        'norm_ple': _gain(ks[4], (DEPTH, D_MODEL)),
        'a_w_in': _normal(ks[5], (N_A_LAYERS, D_MODEL, 2 * GMLP_WIDTH), D_MODEL ** -0.5),
        'a_norm_v': _gain(ks[6], (N_A_LAYERS, GMLP_WIDTH)),
        'a_w_s': _normal(ks[7], (N_A_LAYERS, GMLP_GROUPS, GMLP_CHUNK, GMLP_CHUNK), GMLP_CHUNK ** -0.5),
        'a_b_s': 1.0 + _normal(ks[8], (N_A_LAYERS, GMLP_GROUPS, GMLP_CHUNK), 0.1),
        'a_w_out': _normal(ks[9], (N_A_LAYERS, GMLP_WIDTH, D_MODEL), GMLP_WIDTH ** -0.5),
        'kv_norm': _gain(ks[10], (D_MODEL,)),
        'kv_w': _normal(ks[11], (D_MODEL, kv_cols), D_MODEL ** -0.5),
        'k_norm': _gain(ks[12], (N_BRANCH, HEAD_DIM)),
        'cmp_pe_k': _normal(ks[13], (CMP_BLOCK, HEAD_DIM), 0.1),
        'cmp_pe_v': _normal(ks[14], (CMP_BLOCK, HEAD_DIM), 0.1),
        'cmp_wk1': _normal(ks[15], (CMP_BLOCK * HEAD_DIM, CMP_HIDDEN), (CMP_BLOCK * HEAD_DIM) ** -0.5),
        'cmp_wk2': _normal(ks[16], (CMP_HIDDEN, HEAD_DIM), CMP_HIDDEN ** -0.5),
        'cmp_wv1': _normal(ks[17], (CMP_BLOCK * HEAD_DIM, CMP_HIDDEN), (CMP_BLOCK * HEAD_DIM) ** -0.5),
        'cmp_wv2': _normal(ks[18], (CMP_HIDDEN, HEAD_DIM), CMP_HIDDEN ** -0.5),
        'b_w_in': _normal(ks[19], (N_B_LAYERS, D_MODEL, q_cols), D_MODEL ** -0.5),
        'b_q_norm': _gain(ks[20], (N_B_LAYERS, HEAD_DIM)),
        'b_w_out': _normal(ks[21], (N_B_LAYERS, N_HEADS * HEAD_DIM, D_MODEL), (N_HEADS * HEAD_DIM) ** -0.5),
        'rel_bias': _normal(ks[22], (N_BUCKETS, N_HEADS), 0.5),
        'ffn_w_in': _normal(ks[23], (DEPTH, D_MODEL, 2 * FFN_DIM), D_MODEL ** -0.5),
        'ffn_w_out': _normal(ks[24], (DEPTH, FFN_DIM, D_MODEL), FFN_DIM ** -0.5),
        'ple_w': _normal(ks[25], (DEPTH, PLE_DIM, D_MODEL), PLE_DIM ** -0.5),
        'ple_gate': _normal(ks[26], (DEPTH, D_MODEL, D_MODEL), D_MODEL ** -0.5),
    }


def reference(x, p, norm_mix, norm_ffn, norm_ple, a_w_in, a_norm_v, a_w_s, a_b_s, a_w_out,
              kv_norm, kv_w, k_norm, cmp_pe_k, cmp_pe_v, cmp_wk1, cmp_wk2, cmp_wv1, cmp_wv2,
              b_w_in, b_q_norm, b_w_out, rel_bias, ffn_w_in, ffn_w_out, ple_w, ple_gate):
    kvs = None
    for i in range(DEPTH):
        h = rmsnorm(x, norm_mix[i])
        if i < N_A_LAYERS:
            x = x + gmlp_mixer(h, a_w_in[i], a_norm_v[i], a_w_s[i], a_b_s[i], a_w_out[i])
        else:
            j = i - N_A_LAYERS
            x = x + nsa_mixer(h, *kvs, b_w_in[j], b_q_norm[j], rel_bias, b_w_out[j])
        x = x + swiglu(rmsnorm(x, norm_ffn[i]), ffn_w_in[i], ffn_w_out[i])
        x = ple_add(x, p[i], norm_ple[i], ple_w[i], ple_gate[i])
        if i == N_A_LAYERS - 1:
            kvs = shared_kv(x, kv_norm, kv_w, k_norm, cmp_pe_k, cmp_pe_v, cmp_wk1, cmp_wk2, cmp_wv1, cmp_wv2)
    return x
```

```python
import functools
import math

import numpy as np
import jax
import jax.numpy as jnp
from jax import lax
from jax.experimental import pallas as pl
from jax.experimental.pallas import tpu as pltpu

F32 = jnp.float32
BF16 = jnp.bfloat16

D_MODEL = 2048
PLE_DIM = 256
FFN_DIM = 5632
GMLP_CHUNK = 128
GMLP_GROUPS = 16
HEAD_DIM = 128
N_HEADS = 16
N_KV_GROUPS = 2
HEADS_PER_GROUP = 8
N_BRANCH = 3
CMP_BLOCK = 32
CMP_STRIDE = 16
SLC_BLOCK = 64
N_SELECT = 16
WINDOW = 512
Q_BLOCK = 128
N_BUCKETS = 32
MAX_DISTANCE = 128
EPS = 1e-6
NEG = -1e30
BIG = 1e30

LANE = 128
KV_TILE = 512
NEAR = WINDOW + Q_BLOCK
CMP_PAD = 128
VMEM_LIMIT = 56 * 1024 * 1024


def _cparams(sem):
    return pltpu.CompilerParams(dimension_semantics=sem, vmem_limit_bytes=VMEM_LIMIT)


def _dot(a, b):
    return jnp.dot(a, b, preferred_element_type=F32)


def _dot_nt(a, b):
    return lax.dot_general(a, b, (((1,), (1,)), ((), ())), preferred_element_type=F32)


def _rms_rows(x, g):
    ms = jnp.mean(x * x, axis=-1, keepdims=True)
    return x * lax.rsqrt(ms + EPS) * g


def _group_rms(acc, gain):
    outs = []
    for c in range(acc.shape[1] // LANE):
        a = acc[:, c * LANE:(c + 1) * LANE]
        outs.append(_rms_rows(a, gain[:, c * LANE:(c + 1) * LANE]))
    return outs[0] if len(outs) == 1 else jnp.concatenate(outs, axis=1)


def _norm_matmul_kernel(x_ref, g_ref, *refs, n_w, n_aux, epilogue, out_dtype):
    w_refs = refs[:n_w]
    aux_refs = refs[n_w:n_w + n_aux]
    o_ref = refs[n_w + n_aux]
    h_scr = refs[n_w + n_aux + 1]

    @pl.when(pl.program_id(1) == 0)
    def _():
        h_scr[...] = _rms_rows(x_ref[...], g_ref[...]).astype(BF16)

    h = h_scr[...]
    accs = [_dot(h, w_ref[...].astype(BF16)) for w_ref in w_refs]
    o_ref[...] = epilogue(accs, aux_refs).astype(out_dtype)


def _norm_matmul(x, gain, w, col_offsets, *, tm, tn, nj, epilogue, out_dtype,
                 aux=(), aux_specs=()):
    s, k = x.shape
    in_specs = [pl.BlockSpec((tm, k), lambda i, j: (i, 0)),
                pl.BlockSpec((1, k), lambda i, j: (0, 0))]
    for c0 in col_offsets:
        in_specs.append(pl.BlockSpec((k, tn), lambda i, j, c0=c0: (0, c0 + j)))
    in_specs += list(aux_specs)
    kern = functools.partial(_norm_matmul_kernel, n_w=len(col_offsets), n_aux=len(aux),
                             epilogue=epilogue, out_dtype=out_dtype)
    return pl.pallas_call(
        kern,
        grid=(s // tm, nj),
        in_specs=in_specs,
        out_specs=pl.BlockSpec((tm, tn), lambda i, j: (i, j)),
        out_shape=jax.ShapeDtypeStruct((s, nj * tn), out_dtype),
        scratch_shapes=[pltpu.VMEM((tm, k), BF16)],
        compiler_params=_cparams(("arbitrary", "arbitrary")),
    )(x, gain.reshape(1, k), *([w] * len(col_offsets)), *aux)


def _ep_gelu(accs, aux):
    return jax.nn.gelu(accs[0])


def _ep_swiglu(accs, aux):
    g, u = accs
    return g * jax.nn.sigmoid(g) * u


def _ep_ple(accs, aux):
    p_ref, wp_ref, x_ref = aux
    pp = _dot(p_ref[...].astype(BF16), wp_ref[...].astype(BF16))
    return x_ref[...] + pp * jax.nn.sigmoid(accs[0])


def _ep_identity(accs, aux):
    return accs[0]


def _ep_sigmoid(accs, aux):
    return jax.nn.sigmoid(accs[0])


def _ep_q(accs, aux):
    (gain_ref,) = aux
    return _group_rms(accs[0], gain_ref[...])


def _ep_kv(accs, aux):
    (gain_ref,) = aux
    acc = accs[0]
    normed = _group_rms(acc, gain_ref[0:1, :])
    is_key = (pl.program_id(1) % 2) == 0
    return jnp.where(is_key, normed, acc)


def _matmul_res_kernel(a_ref, w_ref, x_ref, o_ref):
    o_ref[...] = x_ref[...] + _dot(a_ref[...], w_ref[...].astype(BF16))


def _matmul_res(a, w, resid, *, tm, tn):
    s, k = a.shape
    n = w.shape[1]
    return pl.pallas_call(
        _matmul_res_kernel,
        grid=(s // tm, n // tn),
        in_specs=[pl.BlockSpec((tm, k), lambda i, j: (i, 0)),
                  pl.BlockSpec((k, tn), lambda i, j: (0, j)),
                  pl.BlockSpec((tm, tn), lambda i, j: (i, j))],
        out_specs=pl.BlockSpec((tm, tn), lambda i, j: (i, j)),
        out_shape=jax.ShapeDtypeStruct((s, n), F32),
        compiler_params=_cparams(("arbitrary", "arbitrary")),
    )(a, w, resid)


def _gmlp_out_kernel(z_ref, nv_ref, ws_ref, bs_ref, wo_ref, x_ref, o_ref, y_scr, *, tm):
    width = GMLP_GROUPS * LANE
    n_chunk = tm // GMLP_CHUNK

    @pl.when(pl.program_id(1) == 0)
    def _():
        vn = _rms_rows(z_ref[:, width:], nv_ref[...]).astype(BF16)
        row = lax.broadcasted_iota(jnp.int32, (GMLP_CHUNK, GMLP_CHUNK), 0)
        col = lax.broadcasted_iota(jnp.int32, (GMLP_CHUNK, GMLP_CHUNK), 1)
        causal = col <= row
        for g in range(GMLP_GROUPS):
            cs = slice(g * LANE, (g + 1) * LANE)
            ws = jnp.where(causal, ws_ref[g], 0.0).astype(BF16)
            vg = jnp.concatenate(
                [vn[c * GMLP_CHUNK:(c + 1) * GMLP_CHUNK, cs] for c in range(n_chunk)], axis=1)
            sv = _dot(ws, vg) + bs_ref[:, g:g + 1]
            for c in range(n_chunk):
                rs = slice(c * GMLP_CHUNK, (c + 1) * GMLP_CHUNK)
                y_scr[rs, cs] = (z_ref[rs, cs] * sv[:, c * LANE:(c + 1) * LANE]).astype(BF16)

    o_ref[...] = x_ref[...] + _dot(y_scr[...], wo_ref[...].astype(BF16))


def _gmlp_out(z, norm_v, w_s, b_s_t, w_out, resid, *, tm, tn):
    s = z.shape[0]
    width = GMLP_GROUPS * LANE
    n = w_out.shape[1]
    return pl.pallas_call(
        functools.partial(_gmlp_out_kernel, tm=tm),
        grid=(s // tm, n // tn),
        in_specs=[pl.BlockSpec((tm, 2 * width), lambda i, j: (i, 0)),
                  pl.BlockSpec((1, width), lambda i, j: (0, 0)),
                  pl.BlockSpec((GMLP_GROUPS, GMLP_CHUNK, GMLP_CHUNK), lambda i, j: (0, 0, 0)),
                  pl.BlockSpec((GMLP_CHUNK, GMLP_GROUPS), lambda i, j: (0, 0)),
                  pl.BlockSpec((width, tn), lambda i, j: (0, j)),
                  pl.BlockSpec((tm, tn), lambda i, j: (i, j))],
        out_specs=pl.BlockSpec((tm, tn), lambda i, j: (i, j)),
        out_shape=jax.ShapeDtypeStruct((s, n), F32),
        scratch_shapes=[pltpu.VMEM((tm, width), BF16)],
        compiler_params=_cparams(("arbitrary", "arbitrary")),
    )(z, norm_v.reshape(1, width), w_s, b_s_t, w_out, resid)


def _compress_kernel(h_ref, pe_ref, w1_ref, w2_ref, kn_ref, o_ref, *, n_half):
    half_w = CMP_STRIDE * HEAD_DIM
    hh = h_ref[...]
    pe = pe_ref[...]
    a = _dot((hh + pe[0:1, :]).astype(BF16), w1_ref[0:half_w, :].astype(BF16))
    b = _dot((hh + pe[1:2, :]).astype(BF16), w1_ref[half_w:2 * half_w, :].astype(BF16))
    pre = a + pltpu.roll(b, n_half - 1, 0)
    out = _dot(jax.nn.gelu(pre).astype(BF16), w2_ref[...].astype(BF16))
    is_key = pl.program_id(0) < N_KV_GROUPS
    out = jnp.where(is_key, _rms_rows(out, kn_ref[...]), out)
    row = lax.broadcasted_iota(jnp.int32, out.shape, 0)
    o_ref[...] = jnp.where(row < n_half - 1, out, 0.0)


def _compress(halves, pe, w1, w2, k_norm0):
    n_half = halves.shape[1]
    half_w = CMP_STRIDE * HEAD_DIM
    hid = w1.shape[2]
    return pl.pallas_call(
        functools.partial(_compress_kernel, n_half=n_half),
        grid=(2 * N_KV_GROUPS,),
        in_specs=[pl.BlockSpec((None, n_half, half_w), lambda n: (n, 0, 0)),
                  pl.BlockSpec((None, 2, half_w), lambda n: (n // N_KV_GROUPS, 0, 0)),
                  pl.BlockSpec((None, 2 * half_w, hid), lambda n: (n // N_KV_GROUPS, 0, 0)),
                  pl.BlockSpec((None, hid, HEAD_DIM), lambda n: (n // N_KV_GROUPS, 0, 0)),
                  pl.BlockSpec((1, HEAD_DIM), lambda n: (0, 0))],
        out_specs=pl.BlockSpec((None, n_half, HEAD_DIM), lambda n: (n, 0, 0)),
        out_shape=jax.ShapeDtypeStruct((2 * N_KV_GROUPS, n_half, HEAD_DIM), F32),
        compiler_params=_cparams(("arbitrary",)),
    )(halves, pe, w1, w2, k_norm0.reshape(1, HEAD_DIM))


def _t5_bucket_np(dist):
    n = np.maximum(dist, 0)
    max_exact = N_BUCKETS // 2
    nf = np.maximum(n, 1).astype(np.float32)
    large = max_exact + (np.log(nf / np.float32(max_exact)) / np.float32(math.log(MAX_DISTANCE / max_exact))
                         * np.float32(N_BUCKETS - max_exact)).astype(np.int32)
    large = np.minimum(large, N_BUCKETS - 1)
    return np.where(n < max_exact, n, large).astype(np.int32)


def _bucket_patterns():
    i = np.arange(Q_BLOCK)[:, None]
    c = np.arange(LANE)[None, :]
    d0 = i - c
    d1 = i - c + Q_BLOCK
    dc = i - CMP_STRIDE * (c - (LANE - 8)) - (CMP_BLOCK - 1)
    pats = [np.where(d >= 0, _t5_bucket_np(d), -1) for d in (d0, d1, dc)]
    return np.stack(pats).astype(np.int32)


def _bias_tiles_kernel(tab_ref, pat_ref, o_ref):
    h = pl.program_id(0)
    pat = pat_ref[...]
    acc = jnp.full(pat.shape, NEG, F32)
    for b in range(N_BUCKETS):
        acc = jnp.where(pat == b, tab_ref[b, h], acc)
    o_ref[...] = acc


def _bias_tiles(rel_bias):
    pats = jnp.asarray(_bucket_patterns())
    return pl.pallas_call(
        _bias_tiles_kernel,
        grid=(N_HEADS,),
        in_specs=[pl.BlockSpec(memory_space=pltpu.SMEM),
                  pl.BlockSpec((3, Q_BLOCK, LANE), lambda h: (0, 0, 0))],
        out_specs=pl.BlockSpec((None, 3, Q_BLOCK, LANE), lambda h: (h, 0, 0, 0)),
        out_shape=jax.ShapeDtypeStruct((N_HEADS, 3, Q_BLOCK, LANE), F32),
        compiler_params=_cparams(("arbitrary",)),
    )(rel_bias, pats)


def _expand_sel(sel_bf, start, width):
    n_slc = sel_bf.shape[1]
    blk = lax.broadcasted_iota(jnp.int32, (n_slc, width), 0)
    pos = start + lax.broadcasted_iota(jnp.int32, (n_slc, width), 1)
    expand = jnp.where(blk == (pos >> 6), 1.0, 0.0).astype(BF16)
    return _dot(sel_bf, expand)


def _nsa_kernel(tab_ref, q_ref, gate_ref, kc_ref, vc_ref, ovl_ref, ks_ref, vs_ref, kw_ref, vw_ref,
                tb_ref, o_ref, m_scr, l_scr, acc_scr, p_scr, *, n_slc):
    g = pl.program_id(0)
    qb = pl.program_id(1)
    s = qb * Q_BLOCK
    hpg = HEADS_PER_GROUP
    c_far = [tab_ref[N_BUCKETS - 1, g * hpg + h] for h in range(hpg)]

    q_all = q_ref[...]
    qs = jnp.concatenate([q_all[:, h * LANE:(h + 1) * LANE] for h in range(hpg)], axis=0)
    rows = [slice(h * Q_BLOCK, (h + 1) * Q_BLOCK) for h in range(hpg)]

    n_cp = kc_ref.shape[0]
    near0 = pl.multiple_of(qb * 8 + 8, 8)
    kc = kc_ref[...].astype(BF16)
    vc = vc_ref[...].astype(BF16)
    kn = kc_ref[pl.ds(near0, LANE), :].astype(BF16)
    vn = vc_ref[pl.ds(near0, LANE), :].astype(BF16)
    ovl_n = ovl_ref[pl.ds(near0, LANE), :].astype(BF16)
    sf = _dot_nt(qs, kc)
    sn = _dot_nt(qs, kn)
    kp = lax.broadcasted_iota(jnp.int32, (1, n_cp), 1)
    far_ok = (kp >= CMP_PAD) & (kp < near0)
    cn = lax.broadcasted_iota(jnp.int32, (1, LANE), 1)
    near_ok = cn >= (LANE - 8) - qb * 8
    t_col = s + lax.broadcasted_iota(jnp.int32, (Q_BLOCK, 1), 0)
    row_ok = t_col >= CMP_BLOCK - 1
    pf_list, pn_list = [], []
    psum_f = jnp.zeros((Q_BLOCK, n_cp), F32)
    psum_n = jnp.zeros((Q_BLOCK, LANE), F32)
    for h in range(hpg):
        lf = jnp.where(far_ok, sf[rows[h], :] + c_far[h], NEG)
        ln = jnp.where(near_ok, sn[rows[h], :] + tb_ref[h, 2], NEG)
        m = jnp.maximum(jnp.max(lf, axis=1, keepdims=True), jnp.max(ln, axis=1, keepdims=True))
        ef = jnp.exp(lf - m)
        en = jnp.exp(ln - m)
        den = jnp.sum(ef, axis=1, keepdims=True) + jnp.sum(en, axis=1, keepdims=True)
        inv = jnp.where(row_ok, 1.0 / den, 0.0)
        pf = ef * inv
        pn = en * inv
        psum_f = psum_f + pf
        psum_n = psum_n + pn
        pf_list.append(pf.astype(BF16))
        pn_list.append(pn.astype(BF16))
    o_cmp = _dot(jnp.concatenate(pf_list, axis=0), vc) + _dot(jnp.concatenate(pn_list, axis=0), vn)

    ovl = ovl_ref[...].astype(BF16)
    pf_hi = psum_f.astype(BF16)
    pf_lo = (psum_f - pf_hi.astype(F32)).astype(BF16)
    pn_hi = psum_n.astype(BF16)
    pn_lo = (psum_n - pn_hi.astype(F32)).astype(BF16)
    imp = _dot(pf_hi, ovl) + _dot(pf_lo, ovl) + _dot(pn_hi, ovl_n) + _dot(pn_lo, ovl_n)
    imp_t = imp.T
    blk = lax.broadcasted_iota(jnp.int32, (n_slc, Q_BLOCK), 0)
    cur = (s + lax.broadcasted_iota(jnp.int32, (n_slc, Q_BLOCK), 1)) >> 6
    forced = (blk == 0) | (blk == cur) | (blk == cur - 1)
    score = jnp.where(forced, BIG, jnp.where(blk <= cur, imp_t, -BIG))
    sel_t = jnp.zeros((n_slc, Q_BLOCK), F32)
    blk_f = blk.astype(F32)
    for _ in range(min(N_SELECT, n_slc)):
        top = jnp.max(score, axis=0, keepdims=True)
        first = jnp.min(jnp.where(score == top, blk_f, float(n_slc)), axis=0, keepdims=True)
        pick = blk_f == first
        sel_t = jnp.where(pick, 1.0, sel_t)
        score = jnp.where(pick, -2.0 * BIG, score)
    sel_bf = sel_t.T.astype(BF16)

    def near_logits(sall, h, u, ok):
        sub = sall[rows[h], u * LANE:(u + 1) * LANE]
        dd = NEAR // LANE - 1 - u
        if dd == 0:
            return sub + tb_ref[h, 0]
        if dd == 1:
            bias = tb_ref[h, 1]
            return sub + (bias if ok is None else jnp.where(ok, bias, NEG))
        return sub + (c_far[h] if ok is None else jnp.where(ok, c_far[h], NEG))

    row0 = pl.multiple_of(s, Q_BLOCK)
    k_near = ks_ref[pl.ds(row0, NEAR), :]
    v_near = vs_ref[pl.ds(row0, NEAR), :]
    sall = _dot_nt(qs, k_near)
    sel_near = _expand_sel(sel_bf, s - WINDOW, NEAR)
    n_sub = NEAR // LANE
    for h in range(hpg):
        parts = []
        for u in range(n_sub):
            lg = near_logits(sall, h, u, None)
            parts.append(jnp.where(sel_near[:, u * LANE:(u + 1) * LANE] > 0.5, lg, NEG))
        lg = jnp.concatenate(parts, axis=1)
        m = jnp.max(lg, axis=1, keepdims=True)
        e = jnp.exp(lg - m)
        m_scr[rows[h], :] = m
        l_scr[rows[h], :] = jnp.sum(e, axis=1, keepdims=True)
        p_scr[rows[h], 0:NEAR] = e.astype(BF16)
    acc_scr[...] = _dot(p_scr[:, 0:NEAR], v_near)

    n_far = (jnp.maximum(s - WINDOW, 0) + KV_TILE - 1) // KV_TILE

    def far_body(j, carry):
        start = s - WINDOW - KV_TILE * (j + 1)
        r0 = pl.multiple_of(start + WINDOW, Q_BLOCK)
        k_t = ks_ref[pl.ds(r0, KV_TILE), :]
        v_t = vs_ref[pl.ds(r0, KV_TILE), :]
        st = _dot_nt(qs, k_t)
        sel_t_exp = _expand_sel(sel_bf, start, KV_TILE) > 0.5
        for h in range(hpg):
            lg = jnp.where(sel_t_exp, st[rows[h], :] + c_far[h], NEG)
            m_old = m_scr[rows[h], :]
            m_new = jnp.maximum(m_old, jnp.max(lg, axis=1, keepdims=True))
            alpha = jnp.exp(m_old - m_new)
            e = jnp.exp(lg - m_new)
            m_scr[rows[h], :] = m_new
            l_scr[rows[h], :] = alpha * l_scr[rows[h], :] + jnp.sum(e, axis=1, keepdims=True)
            acc_scr[rows[h], :] = alpha * acc_scr[rows[h], :]
            p_scr[rows[h], 0:KV_TILE] = e.astype(BF16)
        acc_scr[...] += _dot(p_scr[:, 0:KV_TILE], v_t)
        return carry

    lax.fori_loop(0, n_far, far_body, 0)

    k_win = kw_ref[pl.ds(row0, NEAR), :]
    v_win = vw_ref[pl.ds(row0, NEAR), :]
    sw = _dot_nt(qs, k_win)
    ri = lax.broadcasted_iota(jnp.int32, (Q_BLOCK, LANE), 0)
    ci = lax.broadcasted_iota(jnp.int32, (Q_BLOCK, LANE), 1)
    pw_list, lw_list = [], []
    for h in range(hpg):
        parts = []
        for u in range(n_sub):
            ok = qb >= (n_sub - 1 - u)
            lg = near_logits(sw, h, u, ok if u < n_sub - 1 else None)
            if u == 0:
                lg = jnp.where(ri < ci, lg, NEG)
            parts.append(lg)
        lg = jnp.concatenate(parts, axis=1)
        m = jnp.max(lg, axis=1, keepdims=True)
        e = jnp.exp(lg - m)
        lw_list.append(jnp.sum(e, axis=1, keepdims=True))
        pw_list.append(e.astype(BF16))
    o_win = _dot(jnp.concatenate(pw_list, axis=0), v_win)

    gates = gate_ref[...]
    for h in range(hpg):
        g_c = gates[:, 3 * h + 0:3 * h + 1]
        g_s = gates[:, 3 * h + 1:3 * h + 2]
        g_w = gates[:, 3 * h + 2:3 * h + 3]
        out = (g_c * o_cmp[rows[h], :]
               + (g_s / l_scr[rows[h], :]) * acc_scr[rows[h], :]
               + (g_w / lw_list[h]) * o_win[rows[h], :])
        o_ref[:, h * LANE:(h + 1) * LANE] = out.astype(o_ref.dtype)


def _nsa_attention(rel_bias, q, gates, kv_cmp, ovl, kvr_pad, bias_tiles):
    s = q.shape[0]
    n_qb = s // Q_BLOCK
    n_slc = s // SLC_BLOCK
    n_cp = kv_cmp.shape[1]
    s_pad = kvr_pad.shape[0]
    gw = HEADS_PER_GROUP * HEAD_DIM
    kv_spec = lambda c0: pl.BlockSpec((s_pad, HEAD_DIM), lambda g, i, c0=c0: (0, c0 + g))
    return pl.pallas_call(
        functools.partial(_nsa_kernel, n_slc=n_slc),
        grid=(N_KV_GROUPS, n_qb),
        in_specs=[pl.BlockSpec(memory_space=pltpu.SMEM),
                  pl.BlockSpec((Q_BLOCK, gw), lambda g, i: (i, g)),
                  pl.BlockSpec((Q_BLOCK, LANE), lambda g, i: (i, g)),
                  pl.BlockSpec((None, n_cp, HEAD_DIM), lambda g, i: (g, 0, 0)),
                  pl.BlockSpec((None, n_cp, HEAD_DIM), lambda g, i: (N_KV_GROUPS + g, 0, 0)),
                  pl.BlockSpec((n_cp, n_slc), lambda g, i: (0, 0)),
                  kv_spec(0), kv_spec(2), kv_spec(4), kv_spec(6),
                  pl.BlockSpec((HEADS_PER_GROUP, 3, Q_BLOCK, LANE), lambda g, i: (g, 0, 0, 0))],
        out_specs=pl.BlockSpec((Q_BLOCK, gw), lambda g, i: (i, g)),
        out_shape=jax.ShapeDtypeStruct((s, N_KV_GROUPS * gw), BF16),
        scratch_shapes=[pltpu.VMEM((HEADS_PER_GROUP * Q_BLOCK, 1), F32),
                        pltpu.VMEM((HEADS_PER_GROUP * Q_BLOCK, 1), F32),
                        pltpu.VMEM((HEADS_PER_GROUP * Q_BLOCK, HEAD_DIM), F32),
                        pltpu.VMEM((HEADS_PER_GROUP * Q_BLOCK, NEAR), BF16)],
        compiler_params=_cparams(("arbitrary", "arbitrary")),
    )(rel_bias, q, gates, kv_cmp, kv_cmp, ovl, kvr_pad, kvr_pad, kvr_pad, kvr_pad, bias_tiles)


def _overlap_padded(n_half, n_slc):
    n_cmp = n_half - 1
    c0 = np.arange(n_cmp) * CMP_STRIDE
    s0 = np.arange(n_slc) * SLC_BLOCK
    lo = np.maximum(c0[:, None], s0[None, :])
    hi = np.minimum(c0[:, None] + CMP_BLOCK, s0[None, :] + SLC_BLOCK)
    ovl = np.maximum(hi - lo, 0).astype(np.float32) / CMP_BLOCK
    out = np.zeros((CMP_PAD + n_half, n_slc), np.float32)
    out[CMP_PAD:CMP_PAD + n_cmp] = ovl
    return out


def _row_tile(s):
    return min(1024, s)


def _ffn_ple(x, p_i, norm_ffn, w_in, w_out, norm_ple, ple_w, ple_gate):
    s = x.shape[0]
    tm = _row_tile(s)
    tn = 512
    nj = FFN_DIM // tn
    act = _norm_matmul(x, norm_ffn, w_in, (0, nj), tm=tm, tn=tn, nj=nj,
                       epilogue=_ep_swiglu, out_dtype=BF16)
    x = _matmul_res(act, w_out, x, tm=tm, tn=256)
    aux = (p_i, ple_w, x)
    aux_specs = (pl.BlockSpec((tm, PLE_DIM), lambda i, j: (i, 0)),
                 pl.BlockSpec((PLE_DIM, tn), lambda i, j: (0, j)),
                 pl.BlockSpec((tm, tn), lambda i, j: (i, j)))
    return _norm_matmul(x, norm_ple, ple_gate, (0,), tm=tm, tn=tn, nj=D_MODEL // tn,
                        epilogue=_ep_ple, out_dtype=F32, aux=aux, aux_specs=aux_specs)


def _gmlp_layer(x, norm_mix, w_in, norm_v, w_s, b_s, w_out):
    s = x.shape[0]
    tm = _row_tile(s)
    width = GMLP_GROUPS * LANE
    z = _norm_matmul(x, norm_mix, w_in, (0,), tm=tm, tn=512, nj=2 * width // 512,
                     epilogue=_ep_gelu, out_dtype=F32)
    return _gmlp_out(z, norm_v, w_s, jnp.transpose(b_s), w_out, x, tm=min(512, s), tn=512)


def _shared_kv(x, kv_norm, kv_w, k_norm, cmp_pe_k, cmp_pe_v, cmp_wk1, cmp_wk2, cmp_wv1, cmp_wv2):
    s = x.shape[0]
    tm = _row_tile(s)
    gw = N_KV_GROUPS * HEAD_DIM
    kvc = _norm_matmul(x, kv_norm, kv_w, (0,), tm=tm, tn=2 * gw, nj=1,
                       epilogue=_ep_identity, out_dtype=F32)
    gains = jnp.stack([jnp.tile(k_norm[1], N_KV_GROUPS), jnp.ones((gw,), F32),
                       jnp.tile(k_norm[2], N_KV_GROUPS), jnp.ones((gw,), F32)])
    gains = jnp.broadcast_to(gains[:, None, :], (4, 8, gw))
    kvr = _norm_matmul(x, kv_norm, kv_w, (2,), tm=tm, tn=gw, nj=4,
                       epilogue=_ep_kv, out_dtype=BF16, aux=(gains,),
                       aux_specs=(pl.BlockSpec((None, 8, gw), lambda i, j: (j, 0, 0)),))
    n_half = s // CMP_STRIDE
    halves = kvc.reshape(n_half, CMP_STRIDE, 2 * N_KV_GROUPS, HEAD_DIM)
    halves = jnp.transpose(halves, (2, 0, 1, 3)).reshape(2 * N_KV_GROUPS, n_half, CMP_STRIDE * HEAD_DIM)
    pe = jnp.stack([cmp_pe_k, cmp_pe_v]).reshape(2, 2, CMP_STRIDE * HEAD_DIM)
    kv_cmp = _compress(halves, pe, jnp.stack([cmp_wk1, cmp_wv1]), jnp.stack([cmp_wk2, cmp_wv2]), k_norm[0])
    kv_cmp = jnp.pad(kv_cmp, ((0, 0), (CMP_PAD, 0), (0, 0)))
    kvr_pad = jnp.pad(kvr, ((WINDOW, 0), (0, 0)))
    return kv_cmp, kvr_pad


def _nsa_layer(x, norm_mix, w_in, q_norm, w_out, rel_bias, kv_cmp, kvr_pad, ovl, bias_tiles):
    s = x.shape[0]
    tm = _row_tile(s)
    nq = N_HEADS * HEAD_DIM
    scale = HEAD_DIM ** -0.5
    q_gain = jnp.tile(q_norm * scale, 4).reshape(1, 4 * HEAD_DIM)
    q = _norm_matmul(x, norm_mix, w_in, (0,), tm=tm, tn=512, nj=nq // 512,
                     epilogue=_ep_q, out_dtype=BF16, aux=(q_gain,),
                     aux_specs=(pl.BlockSpec((1, 4 * HEAD_DIM), lambda i, j: (0, 0)),))
    w_gate = w_in[:, nq:].reshape(D_MODEL, N_KV_GROUPS, HEADS_PER_GROUP * N_BRANCH)
    w_gate = jnp.pad(w_gate, ((0, 0), (0, 0), (0, LANE - HEADS_PER_GROUP * N_BRANCH)))
    w_gate = w_gate.reshape(D_MODEL, N_KV_GROUPS * LANE)
    gates = _norm_matmul(x, norm_mix, w_gate, (0,), tm=tm, tn=N_KV_GROUPS * LANE, nj=1,
                         epilogue=_ep_sigmoid, out_dtype=F32)
    o = _nsa_attention(rel_bias, q, gates, kv_cmp, ovl, kvr_pad, bias_tiles)
    return _matmul_res(o, w_out, x, tm=tm, tn=512)


def kernel(x, p, norm_mix, norm_ffn, norm_ple, a_w_in, a_norm_v, a_w_s, a_b_s, a_w_out, kv_norm, kv_w, k_norm, cmp_pe_k, cmp_pe_v, cmp_wk1, cmp_wk2, cmp_wv1, cmp_wv2, b_w_in, b_q_norm, b_w_out, rel_bias, ffn_w_in, ffn_w_out, ple_w, ple_gate):
    batch, s, d = x.shape
    depth = norm_mix.shape[0]
    n_a = a_w_in.shape[0]
    outs = []
    for b in range(batch):
        xb = x[b]
        kvs = None
        for i in range(depth):
            if i < n_a:
                xb = _gmlp_layer(xb, norm_mix[i], a_w_in[i], a_norm_v[i], a_w_s[i], a_b_s[i], a_w_out[i])
            else:
                j = i - n_a
                kv_cmp, kvr_pad, ovl, tiles = kvs
                xb = _nsa_layer(xb, norm_mix[i], b_w_in[j], b_q_norm[j], b_w_out[j], rel_bias,
                                kv_cmp, kvr_pad, ovl, tiles)
            xb = _ffn_ple(xb, p[i, b], norm_ffn[i], ffn_w_in[i], ffn_w_out[i],
                          norm_ple[i], ple_w[i], ple_gate[i])
            if i == n_a - 1:
                kv_cmp, kvr_pad = _shared_kv(xb, kv_norm, kv_w, k_norm, cmp_pe_k, cmp_pe_v,
                                             cmp_wk1, cmp_wk2, cmp_wv1, cmp_wv2)
                ovl = jnp.asarray(_overlap_padded(s // CMP_STRIDE, s // SLC_BLOCK))
                kvs = (kv_cmp, kvr_pad, ovl, _bias_tiles(rel_bias))
        outs.append(xb)
    return jnp.stack(outs)
```

```python
import functools
import math

import numpy as np
import jax
import jax.numpy as jnp
from jax import lax
from jax.experimental import pallas as pl
from jax.experimental.pallas import tpu as pltpu

F32 = jnp.float32
BF16 = jnp.bfloat16

D_MODEL = 2048
PLE_DIM = 256
FFN_DIM = 5632
GMLP_CHUNK = 128
GMLP_GROUPS = 16
HEAD_DIM = 128
N_HEADS = 16
N_KV_GROUPS = 2
HEADS_PER_GROUP = 8
N_BRANCH = 3
CMP_BLOCK = 32
CMP_STRIDE = 16
SLC_BLOCK = 64
N_SELECT = 16
WINDOW = 512
Q_BLOCK = 128
N_BUCKETS = 32
MAX_DISTANCE = 128
EPS = 1e-6
NEG = -1e30
BIG = 1e30

LANE = 128
KV_TILE = 512
NEAR = WINDOW + Q_BLOCK
CMP_PAD = 128
VMEM_LIMIT = 56 * 1024 * 1024


def _cparams(sem):
    return pltpu.CompilerParams(dimension_semantics=sem, vmem_limit_bytes=VMEM_LIMIT)


def _dot(a, b):
    return jnp.dot(a, b, preferred_element_type=F32)


def _dot_nt(a, b):
    return lax.dot_general(a, b, (((1,), (1,)), ((), ())), preferred_element_type=F32)


def _rms_rows(x, g):
    ms = jnp.mean(x * x, axis=-1, keepdims=True)
    return x * lax.rsqrt(ms + EPS) * g


def _group_rms(acc, gain):
    outs = []
    for c in range(acc.shape[1] // LANE):
        a = acc[:, c * LANE:(c + 1) * LANE]
        outs.append(_rms_rows(a, gain[:, c * LANE:(c + 1) * LANE]))
    return outs[0] if len(outs) == 1 else jnp.concatenate(outs, axis=1)


def _norm_matmul_kernel(x_ref, g_ref, *refs, n_w, n_aux, epilogue, out_dtype):
    w_refs = refs[:n_w]
    aux_refs = refs[n_w:n_w + n_aux]
    o_ref = refs[n_w + n_aux]
    h_scr = refs[n_w + n_aux + 1]

    @pl.when(pl.program_id(1) == 0)
    def _():
        h_scr[...] = _rms_rows(x_ref[...], g_ref[...]).astype(BF16)

    h = h_scr[...]
    accs = [_dot(h, w_ref[...].astype(BF16)) for w_ref in w_refs]
    o_ref[...] = epilogue(accs, aux_refs).astype(out_dtype)


def _w_spec(w, k, tn, col_of):
    if isinstance(w, tuple):
        layer = w[1]
        return pl.BlockSpec((None, k, tn), lambda i, j: (layer, 0, col_of(j)))
    return pl.BlockSpec((k, tn), lambda i, j: (0, col_of(j)))


def _w_array(w):
    return w[0] if isinstance(w, tuple) else w


def _norm_matmul(x, gain, w, col_offsets, *, tm, tn, nj, epilogue, out_dtype,
                 aux=(), aux_specs=()):
    s, k = x.shape
    in_specs = [pl.BlockSpec((tm, k), lambda i, j: (i, 0)),
                pl.BlockSpec((1, k), lambda i, j: (0, 0))]
    for c0 in col_offsets:
        in_specs.append(_w_spec(w, k, tn, lambda j, c0=c0: c0 + j))
    in_specs += list(aux_specs)
    kern = functools.partial(_norm_matmul_kernel, n_w=len(col_offsets), n_aux=len(aux),
                             epilogue=epilogue, out_dtype=out_dtype)
    return pl.pallas_call(
        kern,
        grid=(s // tm, nj),
        in_specs=in_specs,
        out_specs=pl.BlockSpec((tm, tn), lambda i, j: (i, j)),
        out_shape=jax.ShapeDtypeStruct((s, nj * tn), out_dtype),
        scratch_shapes=[pltpu.VMEM((tm, k), BF16)],
        compiler_params=_cparams(("arbitrary", "arbitrary")),
    )(x, gain.reshape(1, k), *([_w_array(w)] * len(col_offsets)), *aux)


def _ep_gelu(accs, aux):
    return jax.nn.gelu(accs[0])


def _ep_swiglu(accs, aux):
    g, u = accs
    return g * jax.nn.sigmoid(g) * u


def _ep_ple(accs, aux):
    p_ref, wp_ref, x_ref = aux
    pp = _dot(p_ref[...].astype(BF16), wp_ref[...].astype(BF16))
    return x_ref[...] + pp * jax.nn.sigmoid(accs[0])


def _ep_identity(accs, aux):
    return accs[0]


def _ep_sigmoid(accs, aux):
    return jax.nn.sigmoid(accs[0])


def _ep_q(accs, aux):
    (gain_ref,) = aux
    return _group_rms(accs[0], gain_ref[...])


def _ep_kv(accs, aux):
    (gain_ref,) = aux
    acc = accs[0]
    normed = _group_rms(acc, gain_ref[0:1, :])
    is_key = (pl.program_id(1) % 2) == 0
    return jnp.where(is_key, normed, acc)


def _matmul_res_kernel(a_ref, w_ref, x_ref, o_ref):
    o_ref[...] = x_ref[...] + _dot(a_ref[...], w_ref[...].astype(BF16))


def _matmul_res(a, w, resid, *, tm, tn):
    s, k = a.shape
    n = _w_array(w).shape[-1]
    return pl.pallas_call(
        _matmul_res_kernel,
        grid=(s // tm, n // tn),
        in_specs=[pl.BlockSpec((tm, k), lambda i, j: (i, 0)),
                  _w_spec(w, k, tn, lambda j: j),
                  pl.BlockSpec((tm, tn), lambda i, j: (i, j))],
        out_specs=pl.BlockSpec((tm, tn), lambda i, j: (i, j)),
        out_shape=jax.ShapeDtypeStruct((s, n), F32),
        compiler_params=_cparams(("arbitrary", "arbitrary")),
    )(a, _w_array(w), resid)


def _gmlp_out_kernel(z_ref, nv_ref, ws_ref, bs_ref, wo_ref, x_ref, o_ref, y_scr, *, tm):
    width = GMLP_GROUPS * LANE
    n_chunk = tm // GMLP_CHUNK

    @pl.when(pl.program_id(1) == 0)
    def _():
        vn = _rms_rows(z_ref[:, width:], nv_ref[...]).astype(BF16)
        row = lax.broadcasted_iota(jnp.int32, (GMLP_CHUNK, GMLP_CHUNK), 0)
        col = lax.broadcasted_iota(jnp.int32, (GMLP_CHUNK, GMLP_CHUNK), 1)
        causal = col <= row
        for g in range(GMLP_GROUPS):
            cs = slice(g * LANE, (g + 1) * LANE)
            ws = jnp.where(causal, ws_ref[g], 0.0).astype(BF16)
            vg = jnp.concatenate(
                [vn[c * GMLP_CHUNK:(c + 1) * GMLP_CHUNK, cs] for c in range(n_chunk)], axis=1)
            sv = _dot(ws, vg) + bs_ref[:, g:g + 1]
            for c in range(n_chunk):
                rs = slice(c * GMLP_CHUNK, (c + 1) * GMLP_CHUNK)
                y_scr[rs, cs] = (z_ref[rs, cs] * sv[:, c * LANE:(c + 1) * LANE]).astype(BF16)

    o_ref[...] = x_ref[...] + _dot(y_scr[...], wo_ref[...].astype(BF16))


def _gmlp_out(z, norm_v, w_s, b_s_t, w_out, resid, layer, *, tm, tn):
    s = z.shape[0]
    width = GMLP_GROUPS * LANE
    n = w_out.shape[-1]
    return pl.pallas_call(
        functools.partial(_gmlp_out_kernel, tm=tm),
        grid=(s // tm, n // tn),
        in_specs=[pl.BlockSpec((tm, 2 * width), lambda i, j: (i, 0)),
                  pl.BlockSpec((1, width), lambda i, j: (0, 0)),
                  pl.BlockSpec((None, GMLP_GROUPS, GMLP_CHUNK, GMLP_CHUNK), lambda i, j: (layer, 0, 0, 0)),
                  pl.BlockSpec((GMLP_CHUNK, GMLP_GROUPS), lambda i, j: (0, 0)),
                  _w_spec((w_out, layer), width, tn, lambda j: j),
                  pl.BlockSpec((tm, tn), lambda i, j: (i, j))],
        out_specs=pl.BlockSpec((tm, tn), lambda i, j: (i, j)),
        out_shape=jax.ShapeDtypeStruct((s, n), F32),
        scratch_shapes=[pltpu.VMEM((tm, width), BF16)],
        compiler_params=_cparams(("arbitrary", "arbitrary")),
    )(z, norm_v.reshape(1, width), w_s, b_s_t, w_out, resid)


def _compress_kernel(h_ref, pe_ref, w1_ref, w2_ref, kn_ref, o_ref, *, n_half):
    half_w = CMP_STRIDE * HEAD_DIM
    hh = h_ref[...]
    pe = pe_ref[...]
    a = _dot((hh + pe[0:1, :]).astype(BF16), w1_ref[0:half_w, :].astype(BF16))
    b = _dot((hh + pe[1:2, :]).astype(BF16), w1_ref[half_w:2 * half_w, :].astype(BF16))
    pre = a + pltpu.roll(b, n_half - 1, 0)
    out = _dot(jax.nn.gelu(pre).astype(BF16), w2_ref[...].astype(BF16))
    is_key = pl.program_id(0) < N_KV_GROUPS
    out = jnp.where(is_key, _rms_rows(out, kn_ref[...]), out)
    row = lax.broadcasted_iota(jnp.int32, out.shape, 0)
    o_ref[...] = jnp.where(row < n_half - 1, out, 0.0)


def _compress(halves, pe, w1, w2, k_norm0):
    n_half = halves.shape[1]
    half_w = CMP_STRIDE * HEAD_DIM
    hid = w1.shape[2]
    return pl.pallas_call(
        functools.partial(_compress_kernel, n_half=n_half),
        grid=(2 * N_KV_GROUPS,),
        in_specs=[pl.BlockSpec((None, n_half, half_w), lambda n: (n, 0, 0)),
                  pl.BlockSpec((None, 2, half_w), lambda n: (n // N_KV_GROUPS, 0, 0)),
                  pl.BlockSpec((None, 2 * half_w, hid), lambda n: (n // N_KV_GROUPS, 0, 0)),
                  pl.BlockSpec((None, hid, HEAD_DIM), lambda n: (n // N_KV_GROUPS, 0, 0)),
                  pl.BlockSpec((1, HEAD_DIM), lambda n: (0, 0))],
        out_specs=pl.BlockSpec((None, n_half, HEAD_DIM), lambda n: (n, 0, 0)),
        out_shape=jax.ShapeDtypeStruct((2 * N_KV_GROUPS, n_half, HEAD_DIM), F32),
        compiler_params=_cparams(("arbitrary",)),
    )(halves, pe, w1, w2, k_norm0.reshape(1, HEAD_DIM))


def _t5_bucket_np(dist):
    n = np.maximum(dist, 0)
    max_exact = N_BUCKETS // 2
    nf = np.maximum(n, 1).astype(np.float32)
    large = max_exact + (np.log(nf / np.float32(max_exact)) / np.float32(math.log(MAX_DISTANCE / max_exact))
                         * np.float32(N_BUCKETS - max_exact)).astype(np.int32)
    large = np.minimum(large, N_BUCKETS - 1)
    return np.where(n < max_exact, n, large).astype(np.int32)


N_PATTERNS = 4


def _bucket_patterns():
    i = np.arange(Q_BLOCK)[:, None]
    c = np.arange(LANE)[None, :]
    d0 = i - c
    d1 = i - c + Q_BLOCK
    dc = i - CMP_STRIDE * (c - (LANE - 8)) - (CMP_BLOCK - 1)
    pats = [np.where(d >= 0, _t5_bucket_np(d), -1) for d in (d0, d1, dc)]
    pats.append(np.where(i < c, N_BUCKETS - 1, -1))
    return np.stack(pats).astype(np.int32)


def _bias_tiles_kernel(tab_ref, pat_ref, o_ref):
    h = pl.program_id(0)
    pat = pat_ref[...]
    far = tab_ref[N_BUCKETS - 1, h]
    acc = jnp.full(pat.shape, NEG, F32)
    for b in range(N_BUCKETS):
        acc = jnp.where(pat == b, tab_ref[b, h] - far, acc)
    o_ref[...] = acc


def _bias_tiles(rel_bias):
    pats = jnp.asarray(_bucket_patterns())
    return pl.pallas_call(
        _bias_tiles_kernel,
        grid=(N_HEADS,),
        in_specs=[pl.BlockSpec(memory_space=pltpu.SMEM),
                  pl.BlockSpec((N_PATTERNS, Q_BLOCK, LANE), lambda h: (0, 0, 0))],
        out_specs=pl.BlockSpec((None, N_PATTERNS, Q_BLOCK, LANE), lambda h: (h, 0, 0, 0)),
        out_shape=jax.ShapeDtypeStruct((N_HEADS, N_PATTERNS, Q_BLOCK, LANE), F32),
        compiler_params=_cparams(("arbitrary",)),
    )(rel_bias, pats)


def _with_features(qs, feat):
    reps = qs.shape[0] // feat.shape[0]
    return jnp.concatenate([qs, jnp.concatenate([feat] * reps, axis=0)], axis=1)


def _nsa_kernel(q_ref, gate_ref, kc_ref, vc_ref, ovl_ref, ks_ref, vs_ref, kw_ref, vw_ref,
                tb_ref, o_ref, m_scr, l_scr, acc_scr, p_scr, *, n_slc):
    qb = pl.program_id(1)
    s = qb * Q_BLOCK
    hpg = HEADS_PER_GROUP

    q_all = q_ref[...]
    qs = jnp.concatenate([q_all[:, h * LANE:(h + 1) * LANE] for h in range(hpg)], axis=0)
    rows = [slice(h * Q_BLOCK, (h + 1) * Q_BLOCK) for h in range(hpg)]
    lane_f = lax.broadcasted_iota(jnp.int32, (Q_BLOCK, LANE), 1)
    pad_feat = jnp.where(lane_f == LANE - 1, 1.0, 0.0).astype(BF16)

    n_cp = kc_ref.shape[0]
    near0 = pl.multiple_of(qb * 8 + 8, 8)
    kc = kc_ref[...].astype(BF16)
    vc = vc_ref[...].astype(BF16)
    kn = kc_ref[pl.ds(near0, LANE), :].astype(BF16)
    vn = vc_ref[pl.ds(near0, LANE), :].astype(BF16)
    ovl_n = ovl_ref[pl.ds(near0, LANE), :].astype(BF16)
    far_feat = jnp.where((lane_f > qb - 16) | (lane_f == LANE - 1), 1.0, 0.0).astype(BF16)
    sf = _dot_nt(_with_features(qs, far_feat), kc)
    sn = _dot_nt(_with_features(qs, pad_feat), kn)
    t_col = s + lax.broadcasted_iota(jnp.int32, (Q_BLOCK, 1), 0)
    row_ok = t_col >= CMP_BLOCK - 1
    pf_list, pn_list = [], []
    psum_f = jnp.zeros((Q_BLOCK, n_cp), F32)
    psum_n = jnp.zeros((Q_BLOCK, LANE), F32)
    for h in range(hpg):
        lf = sf[rows[h], :]
        ln = sn[rows[h], :] + tb_ref[h, 2]
        m = jnp.maximum(jnp.max(lf, axis=1, keepdims=True), jnp.max(ln, axis=1, keepdims=True))
        ef = jnp.exp(lf - m)
        en = jnp.exp(ln - m)
        den = jnp.sum(ef, axis=1, keepdims=True) + jnp.sum(en, axis=1, keepdims=True)
        inv = jnp.where(row_ok, 1.0 / den, 0.0)
        pf = ef * inv
        pn = en * inv
        psum_f = psum_f + pf
        psum_n = psum_n + pn
        pf_list.append(pf.astype(BF16))
        pn_list.append(pn.astype(BF16))
    o_cmp = _dot(jnp.concatenate(pf_list, axis=0), vc) + _dot(jnp.concatenate(pn_list, axis=0), vn)

    ovl = ovl_ref[...].astype(BF16)
    pf_hi = psum_f.astype(BF16)
    pf_lo = (psum_f - pf_hi.astype(F32)).astype(BF16)
    pn_hi = psum_n.astype(BF16)
    pn_lo = (psum_n - pn_hi.astype(F32)).astype(BF16)
    imp = _dot(pf_hi, ovl) + _dot(pf_lo, ovl) + _dot(pn_hi, ovl_n) + _dot(pn_lo, ovl_n)
    imp_t = imp.T
    blk = lax.broadcasted_iota(jnp.int32, (n_slc, Q_BLOCK), 0)
    cur = (s + lax.broadcasted_iota(jnp.int32, (n_slc, Q_BLOCK), 1)) >> 6
    forced = (blk == 0) | (blk == cur) | (blk == cur - 1)
    valid = blk <= cur
    score = jnp.where(forced, BIG, jnp.where(valid, imp_t, -BIG))
    sel_t = jnp.zeros((n_slc, Q_BLOCK), F32)
    blk_f = blk.astype(F32)
    for _ in range(min(N_SELECT, n_slc)):
        top = jnp.max(score, axis=0, keepdims=True)
        first = jnp.min(jnp.where(score == top, blk_f, float(n_slc)), axis=0, keepdims=True)
        pick = blk_f == first
        sel_t = jnp.where(pick, 1.0, sel_t)
        score = jnp.where(pick, -2.0 * BIG, score)
    drop_t = jnp.where(valid, 1.0 - sel_t, 1.0)
    near_blk = (s - WINDOW) >> 6
    drop_far_t = jnp.where(blk >= near_blk, 1.0, drop_t)
    q_near = _with_features(qs, drop_t.T.astype(BF16))
    q_far = _with_features(qs, drop_far_t.T.astype(BF16))

    def near_logits(sall, h, u, oldest_pat):
        sub = sall[rows[h], u * LANE:(u + 1) * LANE]
        dd = NEAR // LANE - 1 - u
        if dd <= 1:
            return sub + tb_ref[h, dd]
        if u == 0 and oldest_pat is not None:
            return sub + tb_ref[h, oldest_pat]
        return sub

    row0 = pl.multiple_of(s, Q_BLOCK)
    k_near = ks_ref[pl.ds(row0, NEAR), :]
    v_near = vs_ref[pl.ds(row0, NEAR), :]
    sall = _dot_nt(q_near, k_near)
    n_sub = NEAR // LANE
    for h in range(hpg):
        lg = jnp.concatenate([near_logits(sall, h, u, None) for u in range(n_sub)], axis=1)
        m = jnp.max(lg, axis=1, keepdims=True)
        e = jnp.exp(lg - m)
        m_scr[rows[h], :] = m
        l_scr[rows[h], :] = jnp.sum(e, axis=1, keepdims=True)
        p_scr[rows[h], 0:NEAR] = e.astype(BF16)
    acc_scr[...] = _dot(p_scr[:, 0:NEAR], v_near)

    n_far = (jnp.maximum(s - WINDOW, 0) + KV_TILE - 1) // KV_TILE

    def far_body(j, carry):
        r0 = pl.multiple_of(WINDOW + KV_TILE * j, KV_TILE)
        k_t = ks_ref[pl.ds(r0, KV_TILE), :]
        v_t = vs_ref[pl.ds(r0, KV_TILE), :]
        st = _dot_nt(q_far, k_t)
        for h in range(hpg):
            lg = st[rows[h], :]
            m_old = m_scr[rows[h], :]
            m_new = jnp.maximum(m_old, jnp.max(lg, axis=1, keepdims=True))
            alpha = jnp.exp(m_old - m_new)
            e = jnp.exp(lg - m_new)
            m_scr[rows[h], :] = m_new
            l_scr[rows[h], :] = alpha * l_scr[rows[h], :] + jnp.sum(e, axis=1, keepdims=True)
            acc_scr[rows[h], :] = alpha * acc_scr[rows[h], :]
            p_scr[rows[h], 0:KV_TILE] = e.astype(BF16)
        acc_scr[...] += _dot(p_scr[:, 0:KV_TILE], v_t)
        return carry

    lax.fori_loop(0, n_far, far_body, 0)

    k_win = kw_ref[pl.ds(row0, NEAR), :]
    v_win = vw_ref[pl.ds(row0, NEAR), :]
    sw = _dot_nt(_with_features(qs, pad_feat), k_win)
    pw_list, lw_list = [], []
    for h in range(hpg):
        lg = jnp.concatenate([near_logits(sw, h, u, 3) for u in range(n_sub)], axis=1)
        m = jnp.max(lg, axis=1, keepdims=True)
        e = jnp.exp(lg - m)
        lw_list.append(jnp.sum(e, axis=1, keepdims=True))
        pw_list.append(e.astype(BF16))
    o_win = _dot(jnp.concatenate(pw_list, axis=0), v_win)

    gates = gate_ref[...]
    for h in range(hpg):
        g_c = gates[:, 3 * h + 0:3 * h + 1]
        g_s = gates[:, 3 * h + 1:3 * h + 2]
        g_w = gates[:, 3 * h + 2:3 * h + 3]
        out = (g_c * o_cmp[rows[h], :]
               + (g_s / l_scr[rows[h], :]) * acc_scr[rows[h], :]
               + (g_w / lw_list[h]) * o_win[rows[h], :])
        o_ref[:, h * LANE:(h + 1) * LANE] = out.astype(o_ref.dtype)


def _nsa_attention(q, gates, k_cmp, v_cmp, ovl, ks, vs, kw, vw, bias_tiles):
    s = q.shape[0]
    n_qb = s // Q_BLOCK
    n_slc = s // SLC_BLOCK
    n_cp = k_cmp.shape[1]
    gw = HEADS_PER_GROUP * HEAD_DIM
    group_spec = lambda a: pl.BlockSpec((None,) + a.shape[1:], lambda g, i: (g, 0, 0))
    return pl.pallas_call(
        functools.partial(_nsa_kernel, n_slc=n_slc),
        grid=(N_KV_GROUPS, n_qb),
        in_specs=[pl.BlockSpec((Q_BLOCK, gw), lambda g, i: (i, g)),
                  pl.BlockSpec((Q_BLOCK, LANE), lambda g, i: (i, g)),
                  group_spec(k_cmp), group_spec(v_cmp),
                  pl.BlockSpec((n_cp, n_slc), lambda g, i: (0, 0)),
                  group_spec(ks), group_spec(vs), group_spec(kw), group_spec(vw),
                  pl.BlockSpec((HEADS_PER_GROUP, N_PATTERNS, Q_BLOCK, LANE), lambda g, i: (g, 0, 0, 0))],
        out_specs=pl.BlockSpec((Q_BLOCK, gw), lambda g, i: (i, g)),
        out_shape=jax.ShapeDtypeStruct((s, N_KV_GROUPS * gw), BF16),
        scratch_shapes=[pltpu.VMEM((HEADS_PER_GROUP * Q_BLOCK, 1), F32),
                        pltpu.VMEM((HEADS_PER_GROUP * Q_BLOCK, 1), F32),
                        pltpu.VMEM((HEADS_PER_GROUP * Q_BLOCK, HEAD_DIM), F32),
                        pltpu.VMEM((HEADS_PER_GROUP * Q_BLOCK, NEAR), BF16)],
        compiler_params=_cparams(("arbitrary", "arbitrary")),
    )(q, gates, k_cmp, v_cmp, ovl, ks, vs, kw, vw, bias_tiles)


def _cmp_mask_columns(n_half):
    assert n_half // 8 < LANE - 1
    out = np.zeros((CMP_PAD + n_half, LANE), np.float32)
    out[np.arange(CMP_PAD), LANE - 1] = NEG
    k = np.arange(n_half)
    out[CMP_PAD + k, k // 8] = NEG
    return out


def _slc_mask_columns(s):
    n_slc = s // SLC_BLOCK
    assert n_slc - 1 > (WINDOW + Q_BLOCK) // SLC_BLOCK
    out = np.zeros((WINDOW + s, n_slc), np.float32)
    out[np.arange(WINDOW), n_slc - 1] = NEG
    pos = np.arange(s)
    out[WINDOW + pos, pos // SLC_BLOCK] = NEG
    return out


def _pad_mask_columns(s):
    out = np.zeros((WINDOW + s, LANE), np.float32)
    out[np.arange(WINDOW), LANE - 1] = NEG
    return out


def _overlap_padded(n_half, n_slc):
    n_cmp = n_half - 1
    c0 = np.arange(n_cmp) * CMP_STRIDE
    s0 = np.arange(n_slc) * SLC_BLOCK
    lo = np.maximum(c0[:, None], s0[None, :])
    hi = np.minimum(c0[:, None] + CMP_BLOCK, s0[None, :] + SLC_BLOCK)
    ovl = np.maximum(hi - lo, 0).astype(np.float32) / CMP_BLOCK
    out = np.zeros((CMP_PAD + n_half, n_slc), np.float32)
    out[CMP_PAD:CMP_PAD + n_cmp] = ovl
    return out


def _row_tile(s):
    return min(1024, s)


def _ffn_ple(x, p, layer, b, norm_ffn, w_in, w_out, norm_ple, ple_w, ple_gate):
    s = x.shape[0]
    tm = _row_tile(s)
    tn = 512
    nj = FFN_DIM // tn
    act = _norm_matmul(x, norm_ffn, (w_in, layer), (0, nj), tm=tm, tn=tn, nj=nj,
                       epilogue=_ep_swiglu, out_dtype=BF16)
    x = _matmul_res(act, (w_out, layer), x, tm=tm, tn=256)
    aux = (p, ple_w, x)
    aux_specs = (pl.BlockSpec((None, None, tm, PLE_DIM), lambda i, j: (layer, b, i, 0)),
                 pl.BlockSpec((None, PLE_DIM, tn), lambda i, j: (layer, 0, j)),
                 pl.BlockSpec((tm, tn), lambda i, j: (i, j)))
    return _norm_matmul(x, norm_ple, (ple_gate, layer), (0,), tm=tm, tn=tn, nj=D_MODEL // tn,
                        epilogue=_ep_ple, out_dtype=F32, aux=aux, aux_specs=aux_specs)


def _gmlp_layer(x, layer, norm_mix, w_in, norm_v, w_s, b_s, w_out):
    s = x.shape[0]
    tm = _row_tile(s)
    width = GMLP_GROUPS * LANE
    z = _norm_matmul(x, norm_mix, (w_in, layer), (0,), tm=tm, tn=512, nj=2 * width // 512,
                     epilogue=_ep_gelu, out_dtype=F32)
    return _gmlp_out(z, norm_v, w_s, jnp.transpose(b_s), w_out, x, layer, tm=min(512, s), tn=512)


def _shared_kv(x, kv_norm, kv_w, k_norm, cmp_pe_k, cmp_pe_v, cmp_wk1, cmp_wk2, cmp_wv1, cmp_wv2):
    s = x.shape[0]
    tm = _row_tile(s)
    gw = N_KV_GROUPS * HEAD_DIM
    kvc = _norm_matmul(x, kv_norm, kv_w, (0,), tm=tm, tn=2 * gw, nj=1,
                       epilogue=_ep_identity, out_dtype=F32)
    gains = jnp.stack([jnp.tile(k_norm[1], N_KV_GROUPS), jnp.ones((gw,), F32),
                       jnp.tile(k_norm[2], N_KV_GROUPS), jnp.ones((gw,), F32)])
    gains = jnp.broadcast_to(gains[:, None, :], (4, 8, gw))
    kvr = _norm_matmul(x, kv_norm, kv_w, (2,), tm=tm, tn=gw, nj=4,
                       epilogue=_ep_kv, out_dtype=BF16, aux=(gains,),
                       aux_specs=(pl.BlockSpec((None, 8, gw), lambda i, j: (j, 0, 0)),))
    n_half = s // CMP_STRIDE
    halves = kvc.reshape(n_half, CMP_STRIDE, 2 * N_KV_GROUPS, HEAD_DIM)
    halves = jnp.transpose(halves, (2, 0, 1, 3)).reshape(2 * N_KV_GROUPS, n_half, CMP_STRIDE * HEAD_DIM)
    pe = jnp.stack([cmp_pe_k, cmp_pe_v]).reshape(2, 2, CMP_STRIDE * HEAD_DIM)
    kv_cmp = _compress(halves, pe, jnp.stack([cmp_wk1, cmp_wv1]), jnp.stack([cmp_wk2, cmp_wv2]), k_norm[0])
    kv_cmp = jnp.pad(kv_cmp, ((0, 0), (CMP_PAD, 0), (0, 0)))
    cmp_cols = jnp.broadcast_to(jnp.asarray(_cmp_mask_columns(n_half)), (N_KV_GROUPS, CMP_PAD + n_half, LANE))
    k_cmp = jnp.concatenate([kv_cmp[:N_KV_GROUPS], cmp_cols], axis=2)
    v_cmp = kv_cmp[N_KV_GROUPS:]
    kvr = jnp.pad(kvr, ((WINDOW, 0), (0, 0))).reshape(WINDOW + s, 4, N_KV_GROUPS, HEAD_DIM)
    kvr = jnp.transpose(kvr, (1, 2, 0, 3))
    slc_cols = jnp.asarray(_slc_mask_columns(s)).astype(BF16)
    pad_cols = jnp.asarray(_pad_mask_columns(s)).astype(BF16)
    ks = jnp.concatenate([kvr[0], jnp.broadcast_to(slc_cols, (N_KV_GROUPS,) + slc_cols.shape)], axis=2)
    kw = jnp.concatenate([kvr[2], jnp.broadcast_to(pad_cols, (N_KV_GROUPS,) + pad_cols.shape)], axis=2)
    return k_cmp, v_cmp, ks, kvr[1], kw, kvr[3]


def _nsa_layer(x, layer, norm_mix, w_in, q_norm, w_out, kvs):
    s = x.shape[0]
    tm = _row_tile(s)
    nq = N_HEADS * HEAD_DIM
    scale = HEAD_DIM ** -0.5
    q_gain = jnp.tile(q_norm * scale, 4).reshape(1, 4 * HEAD_DIM)
    q = _norm_matmul(x, norm_mix, (w_in, layer), (0,), tm=tm, tn=512, nj=nq // 512,
                     epilogue=_ep_q, out_dtype=BF16, aux=(q_gain,),
                     aux_specs=(pl.BlockSpec((1, 4 * HEAD_DIM), lambda i, j: (0, 0)),))
    w_gate = w_in[layer, :, nq:].reshape(D_MODEL, N_KV_GROUPS, HEADS_PER_GROUP * N_BRANCH)
    w_gate = jnp.pad(w_gate, ((0, 0), (0, 0), (0, LANE - HEADS_PER_GROUP * N_BRANCH)))
    w_gate = w_gate.reshape(D_MODEL, N_KV_GROUPS * LANE)
    gates = _norm_matmul(x, norm_mix, w_gate, (0,), tm=tm, tn=N_KV_GROUPS * LANE, nj=1,
                         epilogue=_ep_sigmoid, out_dtype=F32)
    o = _nsa_attention(q, gates, *kvs)
    return _matmul_res(o, (w_out, layer), x, tm=tm, tn=512)


def kernel(x, p, norm_mix, norm_ffn, norm_ple, a_w_in, a_norm_v, a_w_s, a_b_s, a_w_out, kv_norm, kv_w, k_norm, cmp_pe_k, cmp_pe_v, cmp_wk1, cmp_wk2, cmp_wv1, cmp_wv2, b_w_in, b_q_norm, b_w_out, rel_bias, ffn_w_in, ffn_w_out, ple_w, ple_gate):
    batch, s, d = x.shape
    depth = norm_mix.shape[0]
    n_a = a_w_in.shape[0]
    outs = []
    for b in range(batch):
        xb = x[b]
        kvs = None
        for i in range(depth):
            if i < n_a:
                xb = _gmlp_layer(xb, i, norm_mix[i], a_w_in, a_norm_v[i], a_w_s, a_b_s[i], a_w_out)
            else:
                j = i - n_a
                xb = _nsa_layer(xb, j, norm_mix[i], b_w_in, b_q_norm[j], b_w_out, kvs)
            xb = _ffn_ple(xb, p, i, b, norm_ffn[i], ffn_w_in, ffn_w_out,
                          norm_ple[i], ple_w, ple_gate)
            if i == n_a - 1:
                k_cmp, v_cmp, ks, vs, kw, vw = _shared_kv(xb, kv_norm, kv_w, k_norm, cmp_pe_k, cmp_pe_v,
                                                          cmp_wk1, cmp_wk2, cmp_wv1, cmp_wv2)
                ovl = jnp.asarray(_overlap_padded(s // CMP_STRIDE, s // SLC_BLOCK))
                kvs = (k_cmp, v_cmp, ovl, ks, vs, kw, vw, _bias_tiles(rel_bias))
        outs.append(xb)
    return jnp.stack(outs)
```

```python
import functools
import math

import numpy as np
import jax
import jax.numpy as jnp
from jax import lax
from jax.experimental import pallas as pl
from jax.experimental.pallas import tpu as pltpu

F32 = jnp.float32
BF16 = jnp.bfloat16

D_MODEL = 2048
PLE_DIM = 256
FFN_DIM = 5632
GMLP_CHUNK = 128
GMLP_GROUPS = 16
HEAD_DIM = 128
N_HEADS = 16
N_KV_GROUPS = 2
HEADS_PER_GROUP = 8
N_BRANCH = 3
CMP_BLOCK = 32
CMP_STRIDE = 16
SLC_BLOCK = 64
N_SELECT = 16
WINDOW = 512
Q_BLOCK = 128
N_BUCKETS = 32
MAX_DISTANCE = 128
EPS = 1e-6
NEG = -1e30
BIG = 1e30

LANE = 128
KV_TILE = 512
NEAR = WINDOW + Q_BLOCK
CMP_PAD = 128
VMEM_LIMIT = 56 * 1024 * 1024


def _cparams(sem):
    return pltpu.CompilerParams(dimension_semantics=sem, vmem_limit_bytes=VMEM_LIMIT)


def _dot(a, b):
    return jnp.dot(a, b, preferred_element_type=F32)


def _dot_nt(a, b):
    return lax.dot_general(a, b, (((1,), (1,)), ((), ())), preferred_element_type=F32)


def _rms_rows(x, g):
    ms = jnp.mean(x * x, axis=-1, keepdims=True)
    return x * lax.rsqrt(ms + EPS) * g


def _group_rms(acc, gain):
    outs = []
    for c in range(acc.shape[1] // LANE):
        a = acc[:, c * LANE:(c + 1) * LANE]
        outs.append(_rms_rows(a, gain[:, c * LANE:(c + 1) * LANE]))
    return outs[0] if len(outs) == 1 else jnp.concatenate(outs, axis=1)


def _norm_matmul_kernel(x_ref, g_ref, *refs, n_w, n_aux, epilogue, out_dtype):
    w_refs = refs[:n_w]
    aux_refs = refs[n_w:n_w + n_aux]
    o_ref = refs[n_w + n_aux]
    h_scr = refs[n_w + n_aux + 1]

    @pl.when(pl.program_id(1) == 0)
    def _():
        h_scr[...] = _rms_rows(x_ref[...], g_ref[...]).astype(BF16)

    h = h_scr[...]
    accs = [_dot(h, w_ref[...].astype(BF16)) for w_ref in w_refs]
    o_ref[...] = epilogue(accs, aux_refs).astype(out_dtype)


def _w_spec(w, k, tn, col_of):
    if isinstance(w, tuple):
        layer = w[1]
        return pl.BlockSpec((None, k, tn), lambda i, j: (layer, 0, col_of(j)))
    return pl.BlockSpec((k, tn), lambda i, j: (0, col_of(j)))


def _w_array(w):
    return w[0] if isinstance(w, tuple) else w


def _norm_matmul(x, gain, w, col_offsets, *, tm, tn, nj, epilogue, out_dtype,
                 aux=(), aux_specs=()):
    s, k = x.shape
    in_specs = [pl.BlockSpec((tm, k), lambda i, j: (i, 0)),
                pl.BlockSpec((1, k), lambda i, j: (0, 0))]
    for c0 in col_offsets:
        in_specs.append(_w_spec(w, k, tn, lambda j, c0=c0: c0 + j))
    in_specs += list(aux_specs)
    kern = functools.partial(_norm_matmul_kernel, n_w=len(col_offsets), n_aux=len(aux),
                             epilogue=epilogue, out_dtype=out_dtype)
    return pl.pallas_call(
        kern,
        grid=(s // tm, nj),
        in_specs=in_specs,
        out_specs=pl.BlockSpec((tm, tn), lambda i, j: (i, j)),
        out_shape=jax.ShapeDtypeStruct((s, nj * tn), out_dtype),
        scratch_shapes=[pltpu.VMEM((tm, k), BF16)],
        compiler_params=_cparams(("arbitrary", "arbitrary")),
    )(x, gain.reshape(1, k), *([_w_array(w)] * len(col_offsets)), *aux)


def _ep_gelu(accs, aux):
    return jax.nn.gelu(accs[0])


def _ep_swiglu(accs, aux):
    g, u = accs
    return g * jax.nn.sigmoid(g) * u


def _ep_ple(accs, aux):
    p_ref, wp_ref, x_ref = aux
    pp = _dot(p_ref[...].astype(BF16), wp_ref[...].astype(BF16))
    return x_ref[...] + pp * jax.nn.sigmoid(accs[0])


def _ep_identity(accs, aux):
    return accs[0]


def _ep_sigmoid(accs, aux):
    return jax.nn.sigmoid(accs[0])


def _ep_q(accs, aux):
    (gain_ref,) = aux
    return _group_rms(accs[0], gain_ref[...])


def _ep_kv(accs, aux):
    (gain_ref,) = aux
    acc = accs[0]
    normed = _group_rms(acc, gain_ref[0:1, :])
    is_key = (pl.program_id(1) % 2) == 0
    return jnp.where(is_key, normed, acc)


def _matmul_res_kernel(a_ref, w_ref, x_ref, o_ref):
    o_ref[...] = x_ref[...] + _dot(a_ref[...], w_ref[...].astype(BF16))


def _matmul_res(a, w, resid, *, tm, tn):
    s, k = a.shape
    n = _w_array(w).shape[-1]
    return pl.pallas_call(
        _matmul_res_kernel,
        grid=(s // tm, n // tn),
        in_specs=[pl.BlockSpec((tm, k), lambda i, j: (i, 0)),
                  _w_spec(w, k, tn, lambda j: j),
                  pl.BlockSpec((tm, tn), lambda i, j: (i, j))],
        out_specs=pl.BlockSpec((tm, tn), lambda i, j: (i, j)),
        out_shape=jax.ShapeDtypeStruct((s, n), F32),
        compiler_params=_cparams(("arbitrary", "arbitrary")),
    )(a, _w_array(w), resid)


def _gmlp_out_kernel(z_ref, nv_ref, ws_ref, bs_ref, wo_ref, x_ref, o_ref, y_scr, *, tm):
    width = GMLP_GROUPS * LANE
    n_chunk = tm // GMLP_CHUNK

    @pl.when(pl.program_id(1) == 0)
    def _():
        vn = _rms_rows(z_ref[:, width:], nv_ref[...]).astype(BF16)
        row = lax.broadcasted_iota(jnp.int32, (GMLP_CHUNK, GMLP_CHUNK), 0)
        col = lax.broadcasted_iota(jnp.int32, (GMLP_CHUNK, GMLP_CHUNK), 1)
        causal = col <= row
        for g in range(GMLP_GROUPS):
            cs = slice(g * LANE, (g + 1) * LANE)
            ws = jnp.where(causal, ws_ref[g], 0.0).astype(BF16)
            vg = jnp.concatenate(
                [vn[c * GMLP_CHUNK:(c + 1) * GMLP_CHUNK, cs] for c in range(n_chunk)], axis=1)
            sv = _dot(ws, vg) + bs_ref[:, g:g + 1]
            for c in range(n_chunk):
                rs = slice(c * GMLP_CHUNK, (c + 1) * GMLP_CHUNK)
                y_scr[rs, cs] = (z_ref[rs, cs] * sv[:, c * LANE:(c + 1) * LANE]).astype(BF16)

    o_ref[...] = x_ref[...] + _dot(y_scr[...], wo_ref[...].astype(BF16))


def _gmlp_out(z, norm_v, w_s, b_s_t, w_out, resid, layer, *, tm, tn):
    s = z.shape[0]
    width = GMLP_GROUPS * LANE
    n = w_out.shape[-1]
    return pl.pallas_call(
        functools.partial(_gmlp_out_kernel, tm=tm),
        grid=(s // tm, n // tn),
        in_specs=[pl.BlockSpec((tm, 2 * width), lambda i, j: (i, 0)),
                  pl.BlockSpec((1, width), lambda i, j: (0, 0)),
                  pl.BlockSpec((None, GMLP_GROUPS, GMLP_CHUNK, GMLP_CHUNK), lambda i, j: (layer, 0, 0, 0)),
                  pl.BlockSpec((GMLP_CHUNK, GMLP_GROUPS), lambda i, j: (0, 0)),
                  _w_spec((w_out, layer), width, tn, lambda j: j),
                  pl.BlockSpec((tm, tn), lambda i, j: (i, j))],
        out_specs=pl.BlockSpec((tm, tn), lambda i, j: (i, j)),
        out_shape=jax.ShapeDtypeStruct((s, n), F32),
        scratch_shapes=[pltpu.VMEM((tm, width), BF16)],
        compiler_params=_cparams(("arbitrary", "arbitrary")),
    )(z, norm_v.reshape(1, width), w_s, b_s_t, w_out, resid)


def _compress_kernel(h_ref, pe_ref, w1_ref, w2_ref, kn_ref, o_ref, *, n_half):
    half_w = CMP_STRIDE * HEAD_DIM
    hh = h_ref[...]
    pe = pe_ref[...]
    a = _dot((hh + pe[0:1, :]).astype(BF16), w1_ref[0:half_w, :].astype(BF16))
    b = _dot((hh + pe[1:2, :]).astype(BF16), w1_ref[half_w:2 * half_w, :].astype(BF16))
    pre = a + pltpu.roll(b, n_half - 1, 0)
    out = _dot(jax.nn.gelu(pre).astype(BF16), w2_ref[...].astype(BF16))
    is_key = pl.program_id(0) < N_KV_GROUPS
    out = jnp.where(is_key, _rms_rows(out, kn_ref[...]), out)
    row = lax.broadcasted_iota(jnp.int32, out.shape, 0)
    o_ref[...] = jnp.where(row < n_half - 1, out, 0.0)


def _compress(halves, pe, w1, w2, k_norm0):
    n_half = halves.shape[1]
    half_w = CMP_STRIDE * HEAD_DIM
    hid = w1.shape[2]
    return pl.pallas_call(
        functools.partial(_compress_kernel, n_half=n_half),
        grid=(2 * N_KV_GROUPS,),
        in_specs=[pl.BlockSpec((None, n_half, half_w), lambda n: (n, 0, 0)),
                  pl.BlockSpec((None, 2, half_w), lambda n: (n // N_KV_GROUPS, 0, 0)),
                  pl.BlockSpec((None, 2 * half_w, hid), lambda n: (n // N_KV_GROUPS, 0, 0)),
                  pl.BlockSpec((None, hid, HEAD_DIM), lambda n: (n // N_KV_GROUPS, 0, 0)),
                  pl.BlockSpec((1, HEAD_DIM), lambda n: (0, 0))],
        out_specs=pl.BlockSpec((None, n_half, HEAD_DIM), lambda n: (n, 0, 0)),
        out_shape=jax.ShapeDtypeStruct((2 * N_KV_GROUPS, n_half, HEAD_DIM), F32),
        compiler_params=_cparams(("arbitrary",)),
    )(halves, pe, w1, w2, k_norm0.reshape(1, HEAD_DIM))


def _t5_bucket_np(dist):
    n = np.maximum(dist, 0)
    max_exact = N_BUCKETS // 2
    nf = np.maximum(n, 1).astype(np.float32)
    large = max_exact + (np.log(nf / np.float32(max_exact)) / np.float32(math.log(MAX_DISTANCE / max_exact))
                         * np.float32(N_BUCKETS - max_exact)).astype(np.int32)
    large = np.minimum(large, N_BUCKETS - 1)
    return np.where(n < max_exact, n, large).astype(np.int32)


N_PATTERNS = 4


def _bucket_patterns():
    i = np.arange(Q_BLOCK)[:, None]
    c = np.arange(LANE)[None, :]
    d0 = i - c
    d1 = i - c + Q_BLOCK
    dc = i - CMP_STRIDE * (c - (LANE - 8)) - (CMP_BLOCK - 1)
    pats = [np.where(d >= 0, _t5_bucket_np(d), -1) for d in (d0, d1, dc)]
    pats.append(np.where(i < c, N_BUCKETS - 1, -1))
    return np.stack([p.T for p in pats]).astype(np.int32)


def _bias_tiles_kernel(tab_ref, pat_ref, o_ref):
    h = pl.program_id(0)
    pat = pat_ref[...]
    far = tab_ref[N_BUCKETS - 1, h]
    acc = jnp.full(pat.shape, NEG, F32)
    for b in range(N_BUCKETS):
        acc = jnp.where(pat == b, tab_ref[b, h] - far, acc)
    o_ref[...] = acc


def _bias_tiles(rel_bias):
    pats = jnp.asarray(_bucket_patterns())
    return pl.pallas_call(
        _bias_tiles_kernel,
        grid=(N_HEADS,),
        in_specs=[pl.BlockSpec(memory_space=pltpu.SMEM),
                  pl.BlockSpec((N_PATTERNS, Q_BLOCK, LANE), lambda h: (0, 0, 0))],
        out_specs=pl.BlockSpec((None, N_PATTERNS, Q_BLOCK, LANE), lambda h: (h, 0, 0, 0)),
        out_shape=jax.ShapeDtypeStruct((N_HEADS, N_PATTERNS, Q_BLOCK, LANE), F32),
        compiler_params=_cparams(("arbitrary",)),
    )(rel_bias, pats)


def _with_features(qs, feat):
    reps = qs.shape[0] // feat.shape[0]
    return jnp.concatenate([qs, jnp.concatenate([feat] * reps, axis=0)], axis=1)


def _softmax_cols(blocks):
    m = blocks[0].max(axis=0, keepdims=True)
    for b in blocks[1:]:
        m = jnp.maximum(m, b.max(axis=0, keepdims=True))
    es = [jnp.exp(b - m) for b in blocks]
    den = es[0].sum(axis=0, keepdims=True)
    for e in es[1:]:
        den = den + e.sum(axis=0, keepdims=True)
    return es, m, den


def _nsa_kernel(q_ref, gt_ref, kc_ref, vc_ref, vct_ref, ovl_ref, ovlt_ref, ks_ref, vsf_ref, vsn_ref,
                kw_ref, vwn_ref, tb_ref, o_ref, m_scr, l_scr, acc_scr, *, n_slc):
    qb = pl.program_id(1)
    s = qb * Q_BLOCK
    hpg = HEADS_PER_GROUP

    q_all = q_ref[...]
    qs = jnp.concatenate([q_all[:, h * LANE:(h + 1) * LANE] for h in range(hpg)], axis=0)
    cols = [slice(h * Q_BLOCK, (h + 1) * Q_BLOCK) for h in range(hpg)]
    lane_f = lax.broadcasted_iota(jnp.int32, (Q_BLOCK, LANE), 1)
    pad_feat = jnp.where(lane_f == LANE - 1, 1.0, 0.0).astype(BF16)
    q_pad = _with_features(qs, pad_feat)

    near0 = pl.multiple_of(qb * 8 + 8, 8)
    kc = kc_ref[...].astype(BF16)
    kn = kc_ref[pl.ds(near0, LANE), :].astype(BF16)
    vn_t = vc_ref[pl.ds(near0, LANE), :].T.astype(BF16)
    ovl_n_t = ovl_ref[pl.ds(near0, LANE), :].T.astype(BF16)
    far_feat = jnp.where((lane_f > qb - 16) | (lane_f == LANE - 1), 1.0, 0.0).astype(BF16)
    sf = _dot_nt(kc, _with_features(qs, far_feat))
    sn = _dot_nt(kn, q_pad)
    t_row = s + lax.broadcasted_iota(jnp.int32, (1, Q_BLOCK), 1)
    row_ok = t_row >= CMP_BLOCK - 1
    pf_list, pn_list = [], []
    psum_f = jnp.zeros((sf.shape[0], Q_BLOCK), F32)
    psum_n = jnp.zeros((LANE, Q_BLOCK), F32)
    for h in range(hpg):
        (ef, en), _, den = _softmax_cols([sf[:, cols[h]], sn[:, cols[h]] + tb_ref[h, 2]])
        inv = jnp.where(row_ok, 1.0 / den, 0.0)
        pf = ef * inv
        pn = en * inv
        psum_f = psum_f + pf
        psum_n = psum_n + pn
        pf_list.append(pf.astype(BF16))
        pn_list.append(pn.astype(BF16))
    o_cmp = (_dot(vct_ref[...], jnp.concatenate(pf_list, axis=1))
             + _dot(vn_t, jnp.concatenate(pn_list, axis=1)))

    ovl_t = ovlt_ref[...]
    pf_hi = psum_f.astype(BF16)
    pf_lo = (psum_f - pf_hi.astype(F32)).astype(BF16)
    pn_hi = psum_n.astype(BF16)
    pn_lo = (psum_n - pn_hi.astype(F32)).astype(BF16)
    imp_t = _dot(ovl_t, pf_hi) + _dot(ovl_t, pf_lo) + _dot(ovl_n_t, pn_hi) + _dot(ovl_n_t, pn_lo)
    blk = lax.broadcasted_iota(jnp.int32, (n_slc, Q_BLOCK), 0)
    cur = (s + lax.broadcasted_iota(jnp.int32, (n_slc, Q_BLOCK), 1)) >> 6
    forced = (blk == 0) | (blk == cur) | (blk == cur - 1)
    valid = blk <= cur
    score = jnp.where(forced, BIG, jnp.where(valid, imp_t, -BIG))
    sel_t = jnp.zeros((n_slc, Q_BLOCK), F32)
    blk_f = blk.astype(F32)
    for _ in range(min(N_SELECT, n_slc)):
        top = jnp.max(score, axis=0, keepdims=True)
        first = jnp.min(jnp.where(score == top, blk_f, float(n_slc)), axis=0, keepdims=True)
        pick = blk_f == first
        sel_t = jnp.where(pick, 1.0, sel_t)
        score = jnp.where(pick, -2.0 * BIG, score)
    drop_t = jnp.where(valid, 1.0 - sel_t, 1.0)
    near_blk = (s - WINDOW) >> 6
    drop_far_t = jnp.where(blk >= near_blk, 1.0, drop_t)
    q_near = _with_features(qs, drop_t.T.astype(BF16))
    q_far = _with_features(qs, drop_far_t.T.astype(BF16))

    n_sub = NEAR // LANE

    def near_softmax(logits, oldest_pat):
        ps, ms, dens = [], [], []
        for h in range(hpg):
            blocks = [logits[u * LANE:(u + 1) * LANE, cols[h]] for u in range(n_sub)]
            blocks[n_sub - 1] = blocks[n_sub - 1] + tb_ref[h, 0]
            blocks[n_sub - 2] = blocks[n_sub - 2] + tb_ref[h, 1]
            if oldest_pat is not None:
                blocks[0] = blocks[0] + tb_ref[h, oldest_pat]
            es, m, den = _softmax_cols(blocks)
            ps.append(jnp.concatenate(es, axis=0).astype(BF16))
            ms.append(m)
            dens.append(den)
        return jnp.concatenate(ps, axis=1), jnp.concatenate(ms, axis=1), jnp.concatenate(dens, axis=1)

    def near_values(vn_ref):
        blocks = vn_ref[pl.ds(qb, n_sub)]
        return jnp.concatenate([blocks[u] for u in range(n_sub)], axis=1)

    row0 = pl.multiple_of(s, Q_BLOCK)
    p_near, m_near, l_near = near_softmax(_dot_nt(ks_ref[pl.ds(row0, NEAR), :], q_near), None)
    m_scr[...] = m_near
    l_scr[...] = l_near
    acc_scr[...] = _dot(near_values(vsn_ref), p_near)

    n_far = (jnp.maximum(s - WINDOW, 0) + KV_TILE - 1) // KV_TILE

    def far_body(j, carry):
        r0 = pl.multiple_of(WINDOW + KV_TILE * j, KV_TILE)
        st = _dot_nt(ks_ref[pl.ds(r0, KV_TILE), :], q_far)
        m_old = m_scr[...]
        m_new = jnp.concatenate(
            [jnp.maximum(m_old[:, cols[h]], st[:, cols[h]].max(axis=0, keepdims=True)) for h in range(hpg)], axis=1)
        es = [jnp.exp(st[:, cols[h]] - m_new[:, cols[h]]) for h in range(hpg)]
        den = jnp.concatenate([e.sum(axis=0, keepdims=True) for e in es], axis=1)
        p_t = jnp.concatenate([e.astype(BF16) for e in es], axis=1)
        alpha = jnp.exp(m_old - m_new)
        m_scr[...] = m_new
        l_scr[...] = alpha * l_scr[...] + den
        acc_scr[...] = alpha * acc_scr[...] + _dot(vsf_ref[j], p_t)
        return carry

    lax.fori_loop(0, n_far, far_body, 0)

    p_win, _, l_win = near_softmax(_dot_nt(kw_ref[pl.ds(row0, NEAR), :], q_pad), 3)
    o_win = _dot(near_values(vwn_ref), p_win)

    gates = gt_ref[...]
    g_c, g_s, g_w = [jnp.concatenate([gates[3 * h + br:3 * h + br + 1, :] for h in range(hpg)], axis=1)
                     for br in range(N_BRANCH)]
    out_t = g_c * o_cmp + (g_s / l_scr[...]) * acc_scr[...] + (g_w / l_win) * o_win
    for h in range(hpg):
        o_ref[:, cols[h]] = out_t[:, cols[h]].T.astype(o_ref.dtype)


def _nsa_attention(q, gates_t, k_cmp, v_cmp, v_cmp_t, ovl, ovl_t, ks, vs_far, vs_near, kw, vw_near, bias_tiles):
    s = q.shape[0]
    n_qb = s // Q_BLOCK
    n_slc = s // SLC_BLOCK
    gw = HEADS_PER_GROUP * HEAD_DIM
    lanes = HEADS_PER_GROUP * Q_BLOCK

    def group_spec(a):
        zeros = (0,) * (a.ndim - 1)
        return pl.BlockSpec((None,) + a.shape[1:], lambda g, i: (g,) + zeros)

    def whole_spec(a):
        zeros = (0,) * a.ndim
        return pl.BlockSpec(a.shape, lambda g, i: zeros)

    return pl.pallas_call(
        functools.partial(_nsa_kernel, n_slc=n_slc),
        grid=(N_KV_GROUPS, n_qb),
        in_specs=[pl.BlockSpec((Q_BLOCK, gw), lambda g, i: (i, g)),
                  pl.BlockSpec((LANE, Q_BLOCK), lambda g, i: (g, i)),
                  group_spec(k_cmp), group_spec(v_cmp), group_spec(v_cmp_t),
                  whole_spec(ovl), whole_spec(ovl_t),
                  group_spec(ks), group_spec(vs_far), group_spec(vs_near),
                  group_spec(kw), group_spec(vw_near),
                  pl.BlockSpec((HEADS_PER_GROUP, N_PATTERNS, LANE, Q_BLOCK), lambda g, i: (g, 0, 0, 0))],
        out_specs=pl.BlockSpec((Q_BLOCK, gw), lambda g, i: (i, g)),
        out_shape=jax.ShapeDtypeStruct((s, N_KV_GROUPS * gw), BF16),
        scratch_shapes=[pltpu.VMEM((1, lanes), F32),
                        pltpu.VMEM((1, lanes), F32),
                        pltpu.VMEM((HEAD_DIM, lanes), F32)],
        compiler_params=_cparams(("arbitrary", "arbitrary")),
    )(q, gates_t, k_cmp, v_cmp, v_cmp_t, ovl, ovl_t, ks, vs_far, vs_near, kw, vw_near, bias_tiles)


def _cmp_mask_columns(n_half):
    assert n_half // 8 < LANE - 1
    out = np.zeros((CMP_PAD + n_half, LANE), np.float32)
    out[np.arange(CMP_PAD), LANE - 1] = NEG
    k = np.arange(n_half)
    out[CMP_PAD + k, k // 8] = NEG
    return out


def _slc_mask_columns(s):
    n_slc = s // SLC_BLOCK
    assert n_slc - 1 > (WINDOW + Q_BLOCK) // SLC_BLOCK
    out = np.zeros((WINDOW + s, n_slc), np.float32)
    out[np.arange(WINDOW), n_slc - 1] = NEG
    pos = np.arange(s)
    out[WINDOW + pos, pos // SLC_BLOCK] = NEG
    return out


def _pad_mask_columns(s):
    out = np.zeros((WINDOW + s, LANE), np.float32)
    out[np.arange(WINDOW), LANE - 1] = NEG
    return out


def _overlap_padded(n_half, n_slc):
    n_cmp = n_half - 1
    c0 = np.arange(n_cmp) * CMP_STRIDE
    s0 = np.arange(n_slc) * SLC_BLOCK
    lo = np.maximum(c0[:, None], s0[None, :])
    hi = np.minimum(c0[:, None] + CMP_BLOCK, s0[None, :] + SLC_BLOCK)
    ovl = np.maximum(hi - lo, 0).astype(np.float32) / CMP_BLOCK
    out = np.zeros((CMP_PAD + n_half, n_slc), np.float32)
    out[CMP_PAD:CMP_PAD + n_cmp] = ovl
    return out


def _row_tile(s):
    return min(1024, s)


def _ffn_ple(x, p, layer, b, norm_ffn, w_in, w_out, norm_ple, ple_w, ple_gate):
    s = x.shape[0]
    tm = _row_tile(s)
    tn = 512
    nj = FFN_DIM // tn
    act = _norm_matmul(x, norm_ffn, (w_in, layer), (0, nj), tm=tm, tn=tn, nj=nj,
                       epilogue=_ep_swiglu, out_dtype=BF16)
    x = _matmul_res(act, (w_out, layer), x, tm=tm, tn=256)
    aux = (p, ple_w, x)
    aux_specs = (pl.BlockSpec((None, None, tm, PLE_DIM), lambda i, j: (layer, b, i, 0)),
                 pl.BlockSpec((None, PLE_DIM, tn), lambda i, j: (layer, 0, j)),
                 pl.BlockSpec((tm, tn), lambda i, j: (i, j)))
    return _norm_matmul(x, norm_ple, (ple_gate, layer), (0,), tm=tm, tn=tn, nj=D_MODEL // tn,
                        epilogue=_ep_ple, out_dtype=F32, aux=aux, aux_specs=aux_specs)


def _gmlp_layer(x, layer, norm_mix, w_in, norm_v, w_s, b_s, w_out):
    s = x.shape[0]
    tm = _row_tile(s)
    width = GMLP_GROUPS * LANE
    z = _norm_matmul(x, norm_mix, (w_in, layer), (0,), tm=tm, tn=512, nj=2 * width // 512,
                     epilogue=_ep_gelu, out_dtype=F32)
    return _gmlp_out(z, norm_v, w_s, jnp.transpose(b_s), w_out, x, layer, tm=min(512, s), tn=512)


def _shared_kv(x, kv_norm, kv_w, k_norm, cmp_pe_k, cmp_pe_v, cmp_wk1, cmp_wk2, cmp_wv1, cmp_wv2):
    s = x.shape[0]
    tm = _row_tile(s)
    gw = N_KV_GROUPS * HEAD_DIM
    kvc = _norm_matmul(x, kv_norm, kv_w, (0,), tm=tm, tn=2 * gw, nj=1,
                       epilogue=_ep_identity, out_dtype=F32)
    gains = jnp.stack([jnp.tile(k_norm[1], N_KV_GROUPS), jnp.ones((gw,), F32),
                       jnp.tile(k_norm[2], N_KV_GROUPS), jnp.ones((gw,), F32)])
    gains = jnp.broadcast_to(gains[:, None, :], (4, 8, gw))
    kvr = _norm_matmul(x, kv_norm, kv_w, (2,), tm=tm, tn=gw, nj=4,
                       epilogue=_ep_kv, out_dtype=BF16, aux=(gains,),
                       aux_specs=(pl.BlockSpec((None, 8, gw), lambda i, j: (j, 0, 0)),))
    n_half = s // CMP_STRIDE
    halves = kvc.reshape(n_half, CMP_STRIDE, 2 * N_KV_GROUPS, HEAD_DIM)
    halves = jnp.transpose(halves, (2, 0, 1, 3)).reshape(2 * N_KV_GROUPS, n_half, CMP_STRIDE * HEAD_DIM)
    pe = jnp.stack([cmp_pe_k, cmp_pe_v]).reshape(2, 2, CMP_STRIDE * HEAD_DIM)
    kv_cmp = _compress(halves, pe, jnp.stack([cmp_wk1, cmp_wv1]), jnp.stack([cmp_wk2, cmp_wv2]), k_norm[0])
    kv_cmp = jnp.pad(kv_cmp, ((0, 0), (CMP_PAD, 0), (0, 0)))
    cmp_cols = jnp.broadcast_to(jnp.asarray(_cmp_mask_columns(n_half)), (N_KV_GROUPS, CMP_PAD + n_half, LANE))
    k_cmp = jnp.concatenate([kv_cmp[:N_KV_GROUPS], cmp_cols], axis=2)
    v_cmp = kv_cmp[N_KV_GROUPS:]
    v_cmp_t = jnp.transpose(v_cmp, (0, 2, 1)).astype(BF16)
    s_pad = WINDOW + s
    kvr = jnp.pad(kvr, ((WINDOW, 0), (0, 0))).reshape(s_pad, 4, N_KV_GROUPS, HEAD_DIM)
    kvr = jnp.transpose(kvr, (1, 2, 0, 3))
    slc_cols = jnp.asarray(_slc_mask_columns(s)).astype(BF16)
    pad_cols = jnp.asarray(_pad_mask_columns(s)).astype(BF16)
    ks = jnp.concatenate([kvr[0], jnp.broadcast_to(slc_cols, (N_KV_GROUPS,) + slc_cols.shape)], axis=2)
    kw = jnp.concatenate([kvr[2], jnp.broadcast_to(pad_cols, (N_KV_GROUPS,) + pad_cols.shape)], axis=2)

    def key_tiles_t(v, tile):
        return jnp.transpose(v.reshape(N_KV_GROUPS, -1, tile, HEAD_DIM), (0, 1, 3, 2))

    vs_far = key_tiles_t(kvr[1][:, WINDOW:], KV_TILE)
    return k_cmp, v_cmp, v_cmp_t, ks, vs_far, key_tiles_t(kvr[1], LANE), kw, key_tiles_t(kvr[3], LANE)


def _nsa_layer(x, layer, norm_mix, w_in, q_norm, w_out, kvs):
    s = x.shape[0]
    tm = _row_tile(s)
    nq = N_HEADS * HEAD_DIM
    scale = HEAD_DIM ** -0.5
    q_gain = jnp.tile(q_norm * scale, 4).reshape(1, 4 * HEAD_DIM)
    q = _norm_matmul(x, norm_mix, (w_in, layer), (0,), tm=tm, tn=512, nj=nq // 512,
                     epilogue=_ep_q, out_dtype=BF16, aux=(q_gain,),
                     aux_specs=(pl.BlockSpec((1, 4 * HEAD_DIM), lambda i, j: (0, 0)),))
    w_gate = w_in[layer, :, nq:].reshape(D_MODEL, N_KV_GROUPS, HEADS_PER_GROUP * N_BRANCH)
    w_gate = jnp.pad(w_gate, ((0, 0), (0, 0), (0, LANE - HEADS_PER_GROUP * N_BRANCH)))
    w_gate = w_gate.reshape(D_MODEL, N_KV_GROUPS * LANE)
    gates = _norm_matmul(x, norm_mix, w_gate, (0,), tm=tm, tn=N_KV_GROUPS * LANE, nj=1,
                         epilogue=_ep_sigmoid, out_dtype=F32)
    o = _nsa_attention(q, jnp.transpose(gates), *kvs)
    return _matmul_res(o, (w_out, layer), x, tm=tm, tn=512)


def kernel(x, p, norm_mix, norm_ffn, norm_ple, a_w_in, a_norm_v, a_w_s, a_b_s, a_w_out, kv_norm, kv_w, k_norm, cmp_pe_k, cmp_pe_v, cmp_wk1, cmp_wk2, cmp_wv1, cmp_wv2, b_w_in, b_q_norm, b_w_out, rel_bias, ffn_w_in, ffn_w_out, ple_w, ple_gate):
    batch, s, d = x.shape
    depth = norm_mix.shape[0]
    n_a = a_w_in.shape[0]
    outs = []
    for b in range(batch):
        xb = x[b]
        kvs = None
        for i in range(depth):
            if i < n_a:
                xb = _gmlp_layer(xb, i, norm_mix[i], a_w_in, a_norm_v[i], a_w_s, a_b_s[i], a_w_out)
            else:
                j = i - n_a
                xb = _nsa_layer(xb, j, norm_mix[i], b_w_in, b_q_norm[j], b_w_out, kvs)
            xb = _ffn_ple(xb, p, i, b, norm_ffn[i], ffn_w_in, ffn_w_out,
                          norm_ple[i], ple_w, ple_gate)
            if i == n_a - 1:
                k_cmp, v_cmp, v_cmp_t, ks, vs_far, vs_near, kw, vw_near = _shared_kv(
                    xb, kv_norm, kv_w, k_norm, cmp_pe_k, cmp_pe_v, cmp_wk1, cmp_wk2, cmp_wv1, cmp_wv2)
                ovl = _overlap_padded(s // CMP_STRIDE, s // SLC_BLOCK)
                kvs = (k_cmp, v_cmp, v_cmp_t, jnp.asarray(ovl), jnp.asarray(ovl.T).astype(BF16),
                       ks, vs_far, vs_near, kw, vw_near, _bias_tiles(rel_bias))
        outs.append(xb)
    return jnp.stack(outs)
```

```python
import functools
import math

import numpy as np
import jax
import jax.numpy as jnp
from jax import lax
from jax.experimental import pallas as pl
from jax.experimental.pallas import tpu as pltpu

F32 = jnp.float32
BF16 = jnp.bfloat16

D_MODEL = 2048
PLE_DIM = 256
FFN_DIM = 5632
GMLP_CHUNK = 128
GMLP_GROUPS = 16
HEAD_DIM = 128
N_HEADS = 16
N_KV_GROUPS = 2
HEADS_PER_GROUP = 8
N_BRANCH = 3
CMP_BLOCK = 32
CMP_STRIDE = 16
SLC_BLOCK = 64
N_SELECT = 16
WINDOW = 512
Q_BLOCK = 128
N_BUCKETS = 32
MAX_DISTANCE = 128
EPS = 1e-6
NEG = -1e30
BIG = 1e30
LOG2E = math.log2(math.e)

LANE = 128
KV_TILE = 512
NEAR = WINDOW + Q_BLOCK
CMP_PAD = 128
VMEM_LIMIT = 56 * 1024 * 1024


def _cparams(sem):
    return pltpu.CompilerParams(dimension_semantics=sem, vmem_limit_bytes=VMEM_LIMIT)


def _dot(a, b):
    return jnp.dot(a, b, preferred_element_type=F32)


def _dot_nt(a, b):
    return lax.dot_general(a, b, (((1,), (1,)), ((), ())), preferred_element_type=F32)


def _rms_rows(x, g):
    ms = jnp.mean(x * x, axis=-1, keepdims=True)
    return x * lax.rsqrt(ms + EPS) * g


def _group_rms(acc, gain):
    outs = []
    for c in range(acc.shape[1] // LANE):
        a = acc[:, c * LANE:(c + 1) * LANE]
        outs.append(_rms_rows(a, gain[:, c * LANE:(c + 1) * LANE]))
    return outs[0] if len(outs) == 1 else jnp.concatenate(outs, axis=1)


def _norm_matmul_kernel(x_ref, g_ref, *refs, n_w, n_aux, epilogue, out_dtype):
    w_refs = refs[:n_w]
    aux_refs = refs[n_w:n_w + n_aux]
    o_ref = refs[n_w + n_aux]
    h_scr = refs[n_w + n_aux + 1]

    @pl.when(pl.program_id(1) == 0)
    def _():
        h_scr[...] = _rms_rows(x_ref[...], g_ref[...]).astype(BF16)

    h = h_scr[...]
    accs = [_dot(h, w_ref[...].astype(BF16)) for w_ref in w_refs]
    o_ref[...] = epilogue(accs, aux_refs + (x_ref,)).astype(out_dtype)


def _w_spec(w, k, tn, col_of):
    if isinstance(w, tuple):
        layer = w[1]
        return pl.BlockSpec((None, k, tn), lambda i, j: (layer, 0, col_of(j)))
    return pl.BlockSpec((k, tn), lambda i, j: (0, col_of(j)))


def _w_array(w):
    return w[0] if isinstance(w, tuple) else w


def _norm_matmul(x, gain, w, col_offsets, *, tm, tn, nj, epilogue, out_dtype,
                 aux=(), aux_specs=()):
    s, k = x.shape
    in_specs = [pl.BlockSpec((tm, k), lambda i, j: (i, 0)),
                pl.BlockSpec((1, k), lambda i, j: (0, 0))]
    for c0 in col_offsets:
        in_specs.append(_w_spec(w, k, tn, lambda j, c0=c0: c0 + j))
    in_specs += list(aux_specs)
    kern = functools.partial(_norm_matmul_kernel, n_w=len(col_offsets), n_aux=len(aux),
                             epilogue=epilogue, out_dtype=out_dtype)
    return pl.pallas_call(
        kern,
        grid=(s // tm, nj),
        in_specs=in_specs,
        out_specs=pl.BlockSpec((tm, tn), lambda i, j: (i, j)),
        out_shape=jax.ShapeDtypeStruct((s, nj * tn), out_dtype),
        scratch_shapes=[pltpu.VMEM((tm, k), BF16)],
        compiler_params=_cparams(("arbitrary", "arbitrary")),
    )(x, gain.reshape(1, k), *([_w_array(w)] * len(col_offsets)), *aux)


def _ep_gelu(accs, aux):
    return jax.nn.gelu(accs[0])


def _ep_swiglu(accs, aux):
    g, u = accs
    return g * jax.nn.sigmoid(g) * u


def _ep_ple(accs, aux):
    p_ref, wp_ref, x_ref = aux
    tn = accs[0].shape[1]
    col0 = pl.multiple_of(pl.program_id(1) * tn, tn)
    pp = _dot(p_ref[...].astype(BF16), wp_ref[...].astype(BF16))
    return x_ref[:, pl.ds(col0, tn)] + pp * jax.nn.sigmoid(accs[0])


def _ep_identity(accs, aux):
    return accs[0]


def _ep_sigmoid(accs, aux):
    return jax.nn.sigmoid(accs[0])


def _ep_q(accs, aux):
    gain_ref = aux[0]
    return _group_rms(accs[0], gain_ref[...])


def _ep_kv(accs, aux):
    gain_ref = aux[0]
    acc = accs[0]
    normed = _group_rms(acc, gain_ref[0:1, :])
    is_key = (pl.program_id(1) % 2) == 0
    return jnp.where(is_key, normed, acc)


def _matmul_res_kernel(a_ref, w_ref, x_ref, o_ref):
    o_ref[...] = x_ref[...] + _dot(a_ref[...], w_ref[...].astype(BF16))


def _matmul_res(a, w, resid, *, tm, tn):
    s, k = a.shape
    n = _w_array(w).shape[-1]
    return pl.pallas_call(
        _matmul_res_kernel,
        grid=(s // tm, n // tn),
        in_specs=[pl.BlockSpec((tm, k), lambda i, j: (i, 0)),
                  _w_spec(w, k, tn, lambda j: j),
                  pl.BlockSpec((tm, tn), lambda i, j: (i, j))],
        out_specs=pl.BlockSpec((tm, tn), lambda i, j: (i, j)),
        out_shape=jax.ShapeDtypeStruct((s, n), F32),
        compiler_params=_cparams(("arbitrary", "arbitrary")),
    )(a, _w_array(w), resid)


def _gmlp_out_kernel(z_ref, nv_ref, ws_ref, bs_ref, wo_ref, x_ref, o_ref, y_scr, *, tm):
    width = GMLP_GROUPS * LANE
    n_chunk = tm // GMLP_CHUNK

    @pl.when(pl.program_id(1) == 0)
    def _():
        vn = _rms_rows(z_ref[:, width:], nv_ref[...]).astype(BF16)
        row = lax.broadcasted_iota(jnp.int32, (GMLP_CHUNK, GMLP_CHUNK), 0)
        col = lax.broadcasted_iota(jnp.int32, (GMLP_CHUNK, GMLP_CHUNK), 1)
        causal = col <= row
        for g in range(GMLP_GROUPS):
            cs = slice(g * LANE, (g + 1) * LANE)
            ws = jnp.where(causal, ws_ref[g], 0.0).astype(BF16)
            vg = jnp.concatenate(
                [vn[c * GMLP_CHUNK:(c + 1) * GMLP_CHUNK, cs] for c in range(n_chunk)], axis=1)
            sv = _dot(ws, vg) + bs_ref[:, g:g + 1]
            for c in range(n_chunk):
                rs = slice(c * GMLP_CHUNK, (c + 1) * GMLP_CHUNK)
                y_scr[rs, cs] = (z_ref[rs, cs] * sv[:, c * LANE:(c + 1) * LANE]).astype(BF16)

    o_ref[...] = x_ref[...] + _dot(y_scr[...], wo_ref[...].astype(BF16))


def _gmlp_out(z, norm_v, w_s, b_s_t, w_out, resid, layer, *, tm, tn):
    s = z.shape[0]
    width = GMLP_GROUPS * LANE
    n = w_out.shape[-1]
    return pl.pallas_call(
        functools.partial(_gmlp_out_kernel, tm=tm),
        grid=(s // tm, n // tn),
        in_specs=[pl.BlockSpec((tm, 2 * width), lambda i, j: (i, 0)),
                  pl.BlockSpec((1, width), lambda i, j: (0, 0)),
                  pl.BlockSpec((None, GMLP_GROUPS, GMLP_CHUNK, GMLP_CHUNK), lambda i, j: (layer, 0, 0, 0)),
                  pl.BlockSpec((GMLP_CHUNK, GMLP_GROUPS), lambda i, j: (0, 0)),
                  _w_spec((w_out, layer), width, tn, lambda j: j),
                  pl.BlockSpec((tm, tn), lambda i, j: (i, j))],
        out_specs=pl.BlockSpec((tm, tn), lambda i, j: (i, j)),
        out_shape=jax.ShapeDtypeStruct((s, n), F32),
        scratch_shapes=[pltpu.VMEM((tm, width), BF16)],
        compiler_params=_cparams(("arbitrary", "arbitrary")),
    )(z, norm_v.reshape(1, width), w_s, b_s_t, w_out, resid)


def _compress_kernel(h_ref, pe_ref, w1_ref, w2_ref, kn_ref, o_ref, *, n_half):
    half_w = CMP_STRIDE * HEAD_DIM
    hh = h_ref[...]
    pe = pe_ref[...]
    a = _dot((hh + pe[0:1, :]).astype(BF16), w1_ref[0:half_w, :].astype(BF16))
    b = _dot((hh + pe[1:2, :]).astype(BF16), w1_ref[half_w:2 * half_w, :].astype(BF16))
    pre = a + pltpu.roll(b, n_half - 1, 0)
    out = _dot(jax.nn.gelu(pre).astype(BF16), w2_ref[...].astype(BF16))
    is_key = pl.program_id(0) < N_KV_GROUPS
    out = jnp.where(is_key, _rms_rows(out, kn_ref[...]), out)
    row = lax.broadcasted_iota(jnp.int32, out.shape, 0)
    o_ref[...] = jnp.where(row < n_half - 1, out, 0.0)


def _compress(halves, pe, w1, w2, k_norm0):
    n_half = halves.shape[1]
    half_w = CMP_STRIDE * HEAD_DIM
    hid = w1.shape[2]
    return pl.pallas_call(
        functools.partial(_compress_kernel, n_half=n_half),
        grid=(2 * N_KV_GROUPS,),
        in_specs=[pl.BlockSpec((None, n_half, half_w), lambda n: (n, 0, 0)),
                  pl.BlockSpec((None, 2, half_w), lambda n: (n // N_KV_GROUPS, 0, 0)),
                  pl.BlockSpec((None, 2 * half_w, hid), lambda n: (n // N_KV_GROUPS, 0, 0)),
                  pl.BlockSpec((None, hid, HEAD_DIM), lambda n: (n // N_KV_GROUPS, 0, 0)),
                  pl.BlockSpec((1, HEAD_DIM), lambda n: (0, 0))],
        out_specs=pl.BlockSpec((None, n_half, HEAD_DIM), lambda n: (n, 0, 0)),
        out_shape=jax.ShapeDtypeStruct((2 * N_KV_GROUPS, n_half, HEAD_DIM), F32),
        compiler_params=_cparams(("arbitrary",)),
    )(halves, pe, w1, w2, k_norm0.reshape(1, HEAD_DIM))


def _t5_bucket_np(dist):
    n = np.maximum(dist, 0)
    max_exact = N_BUCKETS // 2
    nf = np.maximum(n, 1).astype(np.float32)
    large = max_exact + (np.log(nf / np.float32(max_exact)) / np.float32(math.log(MAX_DISTANCE / max_exact))
                         * np.float32(N_BUCKETS - max_exact)).astype(np.int32)
    large = np.minimum(large, N_BUCKETS - 1)
    return np.where(n < max_exact, n, large).astype(np.int32)


N_PATTERNS = 4


def _bucket_patterns():
    i = np.arange(Q_BLOCK)[:, None]
    c = np.arange(LANE)[None, :]
    d0 = i - c
    d1 = i - c + Q_BLOCK
    dc = i - CMP_STRIDE * (c - (LANE - 8)) - (CMP_BLOCK - 1)
    pats = [np.where(d >= 0, _t5_bucket_np(d), -1) for d in (d0, d1, dc)]
    pats.append(np.where(i < c, N_BUCKETS - 1, -1))
    return np.stack([p.T for p in pats]).astype(np.int32)


def _bias_tiles_kernel(tab_ref, pat_ref, o_ref):
    h = pl.program_id(0)
    pat = pat_ref[...]
    far = tab_ref[N_BUCKETS - 1, h]
    acc = jnp.full(pat.shape, NEG, F32)
    for b in range(N_BUCKETS):
        acc = jnp.where(pat == b, (tab_ref[b, h] - far) * LOG2E, acc)
    o_ref[...] = acc


def _bias_tiles(rel_bias):
    pats = jnp.asarray(_bucket_patterns())
    return pl.pallas_call(
        _bias_tiles_kernel,
        grid=(N_HEADS,),
        in_specs=[pl.BlockSpec(memory_space=pltpu.SMEM),
                  pl.BlockSpec((N_PATTERNS, Q_BLOCK, LANE), lambda h: (0, 0, 0))],
        out_specs=pl.BlockSpec((None, N_PATTERNS, Q_BLOCK, LANE), lambda h: (h, 0, 0, 0)),
        out_shape=jax.ShapeDtypeStruct((N_HEADS, N_PATTERNS, Q_BLOCK, LANE), F32),
        compiler_params=_cparams(("arbitrary",)),
    )(rel_bias, pats)


def _with_features(qs, feat):
    reps = qs.shape[0] // feat.shape[0]
    return jnp.concatenate([qs, jnp.concatenate([feat] * reps, axis=0)], axis=1)


def _softmax_cols(blocks):
    m = blocks[0].max(axis=0, keepdims=True)
    for b in blocks[1:]:
        m = jnp.maximum(m, b.max(axis=0, keepdims=True))
    es = [jnp.exp2(b - m) for b in blocks]
    den = es[0].sum(axis=0, keepdims=True)
    for e in es[1:]:
        den = den + e.sum(axis=0, keepdims=True)
    return es, m, den


def _nsa_kernel(q_ref, gt_ref, kc_ref, vc_ref, vct_ref, ovl_ref, ovlt_ref, ks_ref, vsf_ref, vsn_ref,
                kw_ref, vwn_ref, tb_ref, o_ref, m_scr, l_scr, acc_scr, sa_scr, sb_scr, *, n_slc):
    qb = pl.program_id(1)
    s = qb * Q_BLOCK
    hpg = HEADS_PER_GROUP

    q_all = q_ref[...]
    qs = jnp.concatenate([q_all[:, h * LANE:(h + 1) * LANE] for h in range(hpg)], axis=0)
    cols = [slice(h * Q_BLOCK, (h + 1) * Q_BLOCK) for h in range(hpg)]
    lane_f = lax.broadcasted_iota(jnp.int32, (Q_BLOCK, LANE), 1)
    pad_feat = jnp.where(lane_f == LANE - 1, 1.0, 0.0).astype(BF16)
    q_pad = _with_features(qs, pad_feat)

    n_sub = NEAR // LANE
    row0 = pl.multiple_of(s, Q_BLOCK)

    def near_softmax(logits, oldest_pat):
        ps, ms, dens = [], [], []
        for h in range(hpg):
            blocks = [logits[u * LANE:(u + 1) * LANE, cols[h]] for u in range(n_sub)]
            blocks[n_sub - 1] = blocks[n_sub - 1] + tb_ref[h, 0]
            blocks[n_sub - 2] = blocks[n_sub - 2] + tb_ref[h, 1]
            if oldest_pat is not None:
                blocks[0] = blocks[0] + tb_ref[h, oldest_pat]
            es, m, den = _softmax_cols(blocks)
            ps.append(jnp.concatenate(es, axis=0).astype(BF16))
            ms.append(m)
            dens.append(den)
        return jnp.concatenate(ps, axis=1), jnp.concatenate(ms, axis=1), jnp.concatenate(dens, axis=1)

    def near_values(vn_ref):
        blocks = vn_ref[pl.ds(qb, n_sub)]
        return jnp.concatenate([blocks[u] for u in range(n_sub)], axis=1)

    p_win, _, l_win = near_softmax(_dot_nt(kw_ref[pl.ds(row0, NEAR), :], q_pad), 3)
    o_win = _dot(near_values(vwn_ref), p_win)

    near0 = pl.multiple_of(qb * 8 + 8, 8)
    kc = kc_ref[CMP_PAD:, :].astype(BF16)
    kn = kc_ref[pl.ds(near0, LANE), :].astype(BF16)
    vn_t = vc_ref[pl.ds(near0, LANE), :].T.astype(BF16)
    ovl_n_t = ovl_ref[pl.ds(near0, LANE), :].T.astype(BF16)
    far_feat = jnp.where(lane_f > qb - 16, 1.0, 0.0).astype(BF16)
    sf = _dot_nt(kc, _with_features(qs, far_feat))
    sn = _dot_nt(kn, q_pad)
    t_row = s + lax.broadcasted_iota(jnp.int32, (1, Q_BLOCK), 1)
    row_ok = t_row >= CMP_BLOCK - 1
    pf_list, pn_list = [], []
    psum_f = jnp.zeros((sf.shape[0], Q_BLOCK), F32)
    psum_n = jnp.zeros((LANE, Q_BLOCK), F32)
    for h in range(hpg):
        (ef, en), _, den = _softmax_cols([sf[:, cols[h]], sn[:, cols[h]] + tb_ref[h, 2]])
        inv = jnp.where(row_ok, 1.0 / den, 0.0)
        pf = ef * inv
        pn = en * inv
        psum_f = psum_f + pf
        psum_n = psum_n + pn
        pf_list.append(pf.astype(BF16))
        pn_list.append(pn.astype(BF16))
    o_cmp = (_dot(vct_ref[:, CMP_PAD:], jnp.concatenate(pf_list, axis=1))
             + _dot(vn_t, jnp.concatenate(pn_list, axis=1)))

    ovl_t = ovlt_ref[:, CMP_PAD:]
    pf_hi = psum_f.astype(BF16)
    pf_lo = (psum_f - pf_hi.astype(F32)).astype(BF16)
    pn_hi = psum_n.astype(BF16)
    pn_lo = (psum_n - pn_hi.astype(F32)).astype(BF16)
    imp_t = _dot(ovl_t, pf_hi) + _dot(ovl_t, pf_lo) + _dot(ovl_n_t, pn_hi) + _dot(ovl_n_t, pn_lo)
    blk = lax.broadcasted_iota(jnp.int32, (n_slc, Q_BLOCK), 0)
    cur = (s + lax.broadcasted_iota(jnp.int32, (n_slc, Q_BLOCK), 1)) >> 6
    forced = (blk == 0) | (blk == cur) | (blk == cur - 1)
    valid = blk <= cur
    score = jnp.where(forced, BIG, jnp.where(valid, imp_t, -BIG))
    sel_t = jnp.zeros((n_slc, Q_BLOCK), F32)
    blk_f = blk.astype(F32)
    for _ in range(min(N_SELECT, n_slc)):
        top = jnp.max(score, axis=0, keepdims=True)
        first = jnp.min(jnp.where(score == top, blk_f, float(n_slc)), axis=0, keepdims=True)
        pick = blk_f == first
        sel_t = jnp.where(pick, 1.0, sel_t)
        score = jnp.where(pick, -2.0 * BIG, score)
    drop_t = jnp.where(valid, 1.0 - sel_t, 1.0)
    near_blk = (s - WINDOW) >> 6
    drop_far_t = jnp.where(blk >= near_blk, 1.0, drop_t)
    q_near = _with_features(qs, drop_t.T.astype(BF16))
    q_far = _with_features(qs, drop_far_t.T.astype(BF16))

    p_near, m_near, l_near = near_softmax(_dot_nt(ks_ref[pl.ds(row0, NEAR), :], q_near), None)
    m_scr[...] = m_near
    l_scr[...] = l_near
    acc_scr[...] = _dot(near_values(vsn_ref), p_near)

    gates = gt_ref[...]
    g_c, g_s, g_w = [jnp.concatenate([gates[3 * h + br:3 * h + br + 1, :] for h in range(hpg)], axis=1)
                     for br in range(N_BRANCH)]
    out_cw = g_c * o_cmp + (g_w / l_win) * o_win

    n_far = (jnp.maximum(s - WINDOW, 0) + KV_TILE - 1) // KV_TILE

    n_tiles = vsf_ref.shape[0]

    def far_logits(t):
        r0 = pl.multiple_of(WINDOW + KV_TILE * t, KV_TILE)
        return _dot_nt(ks_ref[pl.ds(r0, KV_TILE), :], q_far)

    def far_update(st_ref, t):
        m_old = m_scr[...]
        sts = [st_ref[:, cols[h]] for h in range(hpg)]
        m_new = jnp.concatenate(
            [jnp.maximum(m_old[:, cols[h]], sts[h].max(axis=0, keepdims=True)) for h in range(hpg)], axis=1)
        es = [jnp.exp2(sts[h] - m_new[:, cols[h]]) for h in range(hpg)]
        den = jnp.concatenate([e.sum(axis=0, keepdims=True) for e in es], axis=1)
        p_t = jnp.concatenate([e.astype(BF16) for e in es], axis=1)
        alpha = jnp.exp2(m_old - m_new)
        m_scr[...] = m_new
        l_scr[...] = alpha * l_scr[...] + den
        acc_scr[...] = alpha * acc_scr[...] + _dot(vsf_ref[t], p_t)

    sa_scr[...] = far_logits(0)

    def far_body(i, carry):
        t0 = 2 * i
        sb_scr[...] = far_logits(t0 + 1)
        far_update(sa_scr, t0)
        sa_scr[...] = far_logits(jnp.minimum(t0 + 2, n_tiles - 1))
        far_update(sb_scr, t0 + 1)
        return carry

    lax.fori_loop(0, (n_far + 1) // 2, far_body, 0)

    out_t = out_cw + (g_s / l_scr[...]) * acc_scr[...]
    for h in range(hpg):
        o_ref[:, cols[h]] = out_t[:, cols[h]].T.astype(o_ref.dtype)


def _nsa_attention(q, gates_t, k_cmp, v_cmp, v_cmp_t, ovl, ovl_t, ks, vs_far, vs_near, kw, vw_near, bias_tiles):
    s = q.shape[0]
    n_qb = s // Q_BLOCK
    n_slc = s // SLC_BLOCK
    gw = HEADS_PER_GROUP * HEAD_DIM
    lanes = HEADS_PER_GROUP * Q_BLOCK

    def group_spec(a):
        zeros = (0,) * (a.ndim - 1)
        return pl.BlockSpec((None,) + a.shape[1:], lambda g, i: (g,) + zeros)

    def whole_spec(a):
        zeros = (0,) * a.ndim
        return pl.BlockSpec(a.shape, lambda g, i: zeros)

    return pl.pallas_call(
        functools.partial(_nsa_kernel, n_slc=n_slc),
        grid=(N_KV_GROUPS, n_qb),
        in_specs=[pl.BlockSpec((Q_BLOCK, gw), lambda g, i: (i, g)),
                  pl.BlockSpec((LANE, Q_BLOCK), lambda g, i: (g, i)),
                  group_spec(k_cmp), group_spec(v_cmp), group_spec(v_cmp_t),
                  whole_spec(ovl), whole_spec(ovl_t),
                  group_spec(ks), group_spec(vs_far), group_spec(vs_near),
                  group_spec(kw), group_spec(vw_near),
                  pl.BlockSpec((HEADS_PER_GROUP, N_PATTERNS, LANE, Q_BLOCK), lambda g, i: (g, 0, 0, 0))],
        out_specs=pl.BlockSpec((Q_BLOCK, gw), lambda g, i: (i, g)),
        out_shape=jax.ShapeDtypeStruct((s, N_KV_GROUPS * gw), BF16),
        scratch_shapes=[pltpu.VMEM((1, lanes), F32),
                        pltpu.VMEM((1, lanes), F32),
                        pltpu.VMEM((HEAD_DIM, lanes), F32),
                        pltpu.VMEM((KV_TILE, lanes), F32),
                        pltpu.VMEM((KV_TILE, lanes), F32)],
        compiler_params=_cparams(("arbitrary", "arbitrary")),
    )(q, gates_t, k_cmp, v_cmp, v_cmp_t, ovl, ovl_t, ks, vs_far, vs_near, kw, vw_near, bias_tiles)


def _cmp_mask_columns(n_half):
    assert n_half // 8 < LANE - 1
    out = np.zeros((CMP_PAD + n_half, LANE), np.float32)
    out[np.arange(CMP_PAD), LANE - 1] = NEG
    k = np.arange(n_half)
    out[CMP_PAD + k, k // 8] = NEG
    return out


def _slc_mask_columns(s):
    n_slc = s // SLC_BLOCK
    assert n_slc - 1 > (WINDOW + Q_BLOCK) // SLC_BLOCK
    out = np.zeros((WINDOW + s, n_slc), np.float32)
    out[np.arange(WINDOW), n_slc - 1] = NEG
    pos = np.arange(s)
    out[WINDOW + pos, pos // SLC_BLOCK] = NEG
    return out


def _pad_mask_columns(s):
    out = np.zeros((WINDOW + s, LANE), np.float32)
    out[np.arange(WINDOW), LANE - 1] = NEG
    return out


def _overlap_padded(n_half, n_slc):
    n_cmp = n_half - 1
    c0 = np.arange(n_cmp) * CMP_STRIDE
    s0 = np.arange(n_slc) * SLC_BLOCK
    lo = np.maximum(c0[:, None], s0[None, :])
    hi = np.minimum(c0[:, None] + CMP_BLOCK, s0[None, :] + SLC_BLOCK)
    ovl = np.maximum(hi - lo, 0).astype(np.float32) / CMP_BLOCK
    out = np.zeros((CMP_PAD + n_half, n_slc), np.float32)
    out[CMP_PAD:CMP_PAD + n_cmp] = ovl
    return out


def _row_tile(s):
    return min(1024, s)


def _ffn_ple(x, p, layer, b, norm_ffn, w_in, w_out, norm_ple, ple_w, ple_gate):
    s = x.shape[0]
    tm = _row_tile(s)
    tn = 512
    nj = FFN_DIM // tn
    act = _norm_matmul(x, norm_ffn, (w_in, layer), (0, nj), tm=tm, tn=tn, nj=nj,
                       epilogue=_ep_swiglu, out_dtype=BF16)
    x = _matmul_res(act, (w_out, layer), x, tm=tm, tn=256)
    aux = (p, ple_w)
    aux_specs = (pl.BlockSpec((None, None, tm, PLE_DIM), lambda i, j: (layer, b, i, 0)),
                 pl.BlockSpec((None, PLE_DIM, tn), lambda i, j: (layer, 0, j)))
    return _norm_matmul(x, norm_ple, (ple_gate, layer), (0,), tm=tm, tn=tn, nj=D_MODEL // tn,
                        epilogue=_ep_ple, out_dtype=F32, aux=aux, aux_specs=aux_specs)


def _gmlp_layer(x, layer, norm_mix, w_in, norm_v, w_s, b_s, w_out):
    s = x.shape[0]
    tm = _row_tile(s)
    width = GMLP_GROUPS * LANE
    z = _norm_matmul(x, norm_mix, (w_in, layer), (0,), tm=tm, tn=512, nj=2 * width // 512,
                     epilogue=_ep_gelu, out_dtype=F32)
    return _gmlp_out(z, norm_v, w_s, jnp.transpose(b_s), w_out, x, layer, tm=min(512, s), tn=512)


def _shared_kv(x, kv_norm, kv_w, k_norm, cmp_pe_k, cmp_pe_v, cmp_wk1, cmp_wk2, cmp_wv1, cmp_wv2):
    s = x.shape[0]
    tm = _row_tile(s)
    gw = N_KV_GROUPS * HEAD_DIM
    kvc = _norm_matmul(x, kv_norm, kv_w, (0,), tm=tm, tn=2 * gw, nj=1,
                       epilogue=_ep_identity, out_dtype=F32)
    gains = jnp.stack([jnp.tile(k_norm[1], N_KV_GROUPS), jnp.ones((gw,), F32),
                       jnp.tile(k_norm[2], N_KV_GROUPS), jnp.ones((gw,), F32)])
    gains = jnp.broadcast_to(gains[:, None, :], (4, 8, gw))
    kvr = _norm_matmul(x, kv_norm, kv_w, (2,), tm=tm, tn=gw, nj=4,
                       epilogue=_ep_kv, out_dtype=BF16, aux=(gains,),
                       aux_specs=(pl.BlockSpec((None, 8, gw), lambda i, j: (j, 0, 0)),))
    n_half = s // CMP_STRIDE
    halves = kvc.reshape(n_half, CMP_STRIDE, 2 * N_KV_GROUPS, HEAD_DIM)
    halves = jnp.transpose(halves, (2, 0, 1, 3)).reshape(2 * N_KV_GROUPS, n_half, CMP_STRIDE * HEAD_DIM)
    pe = jnp.stack([cmp_pe_k, cmp_pe_v]).reshape(2, 2, CMP_STRIDE * HEAD_DIM)
    kv_cmp = _compress(halves, pe, jnp.stack([cmp_wk1, cmp_wv1]), jnp.stack([cmp_wk2, cmp_wv2]), k_norm[0])
    kv_cmp = jnp.pad(kv_cmp, ((0, 0), (CMP_PAD, 0), (0, 0)))
    cmp_cols = jnp.broadcast_to(jnp.asarray(_cmp_mask_columns(n_half)), (N_KV_GROUPS, CMP_PAD + n_half, LANE))
    k_cmp = jnp.concatenate([kv_cmp[:N_KV_GROUPS], cmp_cols], axis=2)
    v_cmp = kv_cmp[N_KV_GROUPS:]
    v_cmp_t = jnp.transpose(v_cmp, (0, 2, 1)).astype(BF16)
    s_pad = WINDOW + s
    kvr = jnp.pad(kvr, ((WINDOW, 0), (0, 0))).reshape(s_pad, 4, N_KV_GROUPS, HEAD_DIM)
    kvr = jnp.transpose(kvr, (1, 2, 0, 3))
    slc_cols = jnp.asarray(_slc_mask_columns(s)).astype(BF16)
    pad_cols = jnp.asarray(_pad_mask_columns(s)).astype(BF16)
    ks = jnp.concatenate([kvr[0], jnp.broadcast_to(slc_cols, (N_KV_GROUPS,) + slc_cols.shape)], axis=2)
    kw = jnp.concatenate([kvr[2], jnp.broadcast_to(pad_cols, (N_KV_GROUPS,) + pad_cols.shape)], axis=2)

    def key_tiles_t(v, tile):
        return jnp.transpose(v.reshape(N_KV_GROUPS, -1, tile, HEAD_DIM), (0, 1, 3, 2))

    vs_far = key_tiles_t(kvr[1][:, WINDOW:], KV_TILE)
    return k_cmp, v_cmp, v_cmp_t, ks, vs_far, key_tiles_t(kvr[1], LANE), kw, key_tiles_t(kvr[3], LANE)


def _nsa_layer(x, layer, norm_mix, w_in, q_norm, w_out, kvs):
    s = x.shape[0]
    tm = _row_tile(s)
    nq = N_HEADS * HEAD_DIM
    scale = HEAD_DIM ** -0.5 * LOG2E
    q_gain = jnp.tile(q_norm * scale, 4).reshape(1, 4 * HEAD_DIM)
    q = _norm_matmul(x, norm_mix, (w_in, layer), (0,), tm=tm, tn=512, nj=nq // 512,
                     epilogue=_ep_q, out_dtype=BF16, aux=(q_gain,),
                     aux_specs=(pl.BlockSpec((1, 4 * HEAD_DIM), lambda i, j: (0, 0)),))
    w_gate = w_in[layer, :, nq:].reshape(D_MODEL, N_KV_GROUPS, HEADS_PER_GROUP * N_BRANCH)
    w_gate = jnp.pad(w_gate, ((0, 0), (0, 0), (0, LANE - HEADS_PER_GROUP * N_BRANCH)))
    w_gate = w_gate.reshape(D_MODEL, N_KV_GROUPS * LANE)
    gates = _norm_matmul(x, norm_mix, w_gate, (0,), tm=tm, tn=N_KV_GROUPS * LANE, nj=1,
                         epilogue=_ep_sigmoid, out_dtype=F32)
    o = _nsa_attention(q, jnp.transpose(gates), *kvs)
    return _matmul_res(o, (w_out, layer), x, tm=tm, tn=512)


def kernel(x, p, norm_mix, norm_ffn, norm_ple, a_w_in, a_norm_v, a_w_s, a_b_s, a_w_out, kv_norm, kv_w, k_norm, cmp_pe_k, cmp_pe_v, cmp_wk1, cmp_wk2, cmp_wv1, cmp_wv2, b_w_in, b_q_norm, b_w_out, rel_bias, ffn_w_in, ffn_w_out, ple_w, ple_gate):
    batch, s, d = x.shape
    depth = norm_mix.shape[0]
    n_a = a_w_in.shape[0]
    a_w_in, a_w_out, b_w_in, b_w_out, kv_w, ple_gate = (
        w.astype(BF16) for w in (a_w_in, a_w_out, b_w_in, b_w_out, kv_w, ple_gate))
    outs = []
    for b in range(batch):
        xb = x[b]
        kvs = None
        for i in range(depth):
            if i < n_a:
                xb = _gmlp_layer(xb, i, norm_mix[i], a_w_in, a_norm_v[i], a_w_s, a_b_s[i], a_w_out)
            else:
                j = i - n_a
                xb = _nsa_layer(xb, j, norm_mix[i], b_w_in, b_q_norm[j], b_w_out, kvs)
            xb = _ffn_ple(xb, p, i, b, norm_ffn[i], ffn_w_in, ffn_w_out,
                          norm_ple[i], ple_w, ple_gate)
            if i == n_a - 1:
                k_cmp, v_cmp, v_cmp_t, ks, vs_far, vs_near, kw, vw_near = _shared_kv(
                    xb, kv_norm, kv_w, k_norm, cmp_pe_k, cmp_pe_v, cmp_wk1, cmp_wk2, cmp_wv1, cmp_wv2)
                ovl = _overlap_padded(s // CMP_STRIDE, s // SLC_BLOCK)
                kvs = (k_cmp, v_cmp, v_cmp_t, jnp.asarray(ovl), jnp.asarray(ovl.T).astype(BF16),
                       ks, vs_far, vs_near, kw, vw_near, _bias_tiles(rel_bias))
        outs.append(xb)
    return jnp.stack(outs)
```

```python
import functools
import math

import numpy as np
import jax
import jax.numpy as jnp
from jax import lax
from jax.experimental import pallas as pl
from jax.experimental.pallas import tpu as pltpu

F32 = jnp.float32
BF16 = jnp.bfloat16

D_MODEL = 2048
PLE_DIM = 256
FFN_DIM = 5632
GMLP_CHUNK = 128
GMLP_GROUPS = 16
HEAD_DIM = 128
N_HEADS = 16
N_KV_GROUPS = 2
HEADS_PER_GROUP = 8
N_BRANCH = 3
CMP_BLOCK = 32
CMP_STRIDE = 16
SLC_BLOCK = 64
N_SELECT = 16
WINDOW = 512
Q_BLOCK = 128
N_BUCKETS = 32
MAX_DISTANCE = 128
EPS = 1e-6
NEG = -1e30
BIG = 1e30
LOG2E = math.log2(math.e)

LANE = 128
KV_TILE = 512
NEAR = WINDOW + Q_BLOCK
CMP_PAD = 128
VMEM_LIMIT = 56 * 1024 * 1024


def _cparams(sem):
    return pltpu.CompilerParams(dimension_semantics=sem, vmem_limit_bytes=VMEM_LIMIT)


def _dot(a, b):
    return jnp.dot(a, b, preferred_element_type=F32)


def _dot_nt(a, b):
    return lax.dot_general(a, b, (((1,), (1,)), ((), ())), preferred_element_type=F32)


def _rms_rows(x, g):
    ms = jnp.mean(x * x, axis=-1, keepdims=True)
    return x * lax.rsqrt(ms + EPS) * g


def _group_rms(acc, gain):
    outs = []
    for c in range(acc.shape[1] // LANE):
        a = acc[:, c * LANE:(c + 1) * LANE]
        outs.append(_rms_rows(a, gain[:, c * LANE:(c + 1) * LANE]))
    return outs[0] if len(outs) == 1 else jnp.concatenate(outs, axis=1)


def _norm_matmul_kernel(x_ref, g_ref, *refs, n_w, n_aux, epilogue, out_dtype):
    w_refs = refs[:n_w]
    aux_refs = refs[n_w:n_w + n_aux]
    o_ref = refs[n_w + n_aux]
    h_scr = refs[n_w + n_aux + 1]

    @pl.when(pl.program_id(1) == 0)
    def _():
        h_scr[...] = _rms_rows(x_ref[...], g_ref[...]).astype(BF16)

    h = h_scr[...]
    accs = [_dot(h, w_ref[...].astype(BF16)) for w_ref in w_refs]
    o_ref[...] = epilogue(accs, aux_refs + (x_ref,)).astype(out_dtype)


def _w_spec(w, k, tn, col_of):
    if isinstance(w, tuple):
        layer = w[1]
        return pl.BlockSpec((None, k, tn), lambda i, j: (layer, 0, col_of(j)))
    return pl.BlockSpec((k, tn), lambda i, j: (0, col_of(j)))


def _w_array(w):
    return w[0] if isinstance(w, tuple) else w


def _norm_matmul(x, gain, w, col_offsets, *, tm, tn, nj, epilogue, out_dtype,
                 aux=(), aux_specs=()):
    s, k = x.shape
    in_specs = [pl.BlockSpec((tm, k), lambda i, j: (i, 0)),
                pl.BlockSpec((1, k), lambda i, j: (0, 0))]
    for c0 in col_offsets:
        in_specs.append(_w_spec(w, k, tn, lambda j, c0=c0: c0 + j))
    in_specs += list(aux_specs)
    kern = functools.partial(_norm_matmul_kernel, n_w=len(col_offsets), n_aux=len(aux),
                             epilogue=epilogue, out_dtype=out_dtype)
    return pl.pallas_call(
        kern,
        grid=(s // tm, nj),
        in_specs=in_specs,
        out_specs=pl.BlockSpec((tm, tn), lambda i, j: (i, j)),
        out_shape=jax.ShapeDtypeStruct((s, nj * tn), out_dtype),
        scratch_shapes=[pltpu.VMEM((tm, k), BF16)],
        compiler_params=_cparams(("arbitrary", "arbitrary")),
    )(x, gain.reshape(1, k), *([_w_array(w)] * len(col_offsets)), *aux)


def _ep_swiglu(accs, aux):
    g, u = accs
    return g * jax.nn.sigmoid(g) * u


def _ep_ple(accs, aux):
    p_ref, wp_ref, x_ref = aux
    tn = accs[0].shape[1]
    col0 = pl.multiple_of(pl.program_id(1) * tn, tn)
    pp = _dot(p_ref[...].astype(BF16), wp_ref[...].astype(BF16))
    return x_ref[:, pl.ds(col0, tn)] + pp * jax.nn.sigmoid(accs[0])


def _ep_identity(accs, aux):
    return accs[0]


def _ep_sigmoid(accs, aux):
    return jax.nn.sigmoid(accs[0])


def _ep_q(accs, aux):
    gain_ref = aux[0]
    return _group_rms(accs[0], gain_ref[...])


def _ep_kv(accs, aux):
    gain_ref = aux[0]
    acc = accs[0]
    normed = _group_rms(acc, gain_ref[0:1, :])
    is_key = (pl.program_id(1) % 2) == 0
    return jnp.where(is_key, normed, acc)


def _matmul_res_kernel(a_ref, w_ref, x_ref, o_ref):
    o_ref[...] = x_ref[...] + _dot(a_ref[...], w_ref[...].astype(BF16))


def _matmul_res(a, w, resid, *, tm, tn):
    s, k = a.shape
    n = _w_array(w).shape[-1]
    return pl.pallas_call(
        _matmul_res_kernel,
        grid=(s // tm, n // tn),
        in_specs=[pl.BlockSpec((tm, k), lambda i, j: (i, 0)),
                  _w_spec(w, k, tn, lambda j: j),
                  pl.BlockSpec((tm, tn), lambda i, j: (i, j))],
        out_specs=pl.BlockSpec((tm, tn), lambda i, j: (i, j)),
        out_shape=jax.ShapeDtypeStruct((s, n), F32),
        compiler_params=_cparams(("arbitrary", "arbitrary")),
    )(a, _w_array(w), resid)


GMLP_TN = 512


def _gmlp_kernel(x_ref, g_ref, win_ref, nv_ref, ws_ref, bs_ref, wo_ref, o_ref, h_scr, z_scr, y_scr, *, tm):
    width = GMLP_GROUPS * LANE
    n_in = 2 * width // GMLP_TN
    per_tile = GMLP_TN // LANE
    n_chunk = tm // GMLP_CHUNK
    j = pl.program_id(1)

    @pl.when(j == 0)
    def _():
        h_scr[...] = _rms_rows(x_ref[...], g_ref[...]).astype(BF16)

    @pl.when(j < n_in)
    def _():
        z_scr[j] = jax.nn.gelu(_dot(h_scr[...], win_ref[...].astype(BF16)))

    @pl.when(j == n_in)
    def _():
        v_tiles = range(n_in // 2, n_in)
        ms = sum(jnp.sum(z_scr[t] * z_scr[t], axis=-1, keepdims=True) for t in v_tiles) / width
        inv = lax.rsqrt(ms + EPS)
        row = lax.broadcasted_iota(jnp.int32, (GMLP_CHUNK, GMLP_CHUNK), 0)
        col = lax.broadcasted_iota(jnp.int32, (GMLP_CHUNK, GMLP_CHUNK), 1)
        causal = col <= row
        for g in range(GMLP_GROUPS):
            t, ls = g // per_tile, slice((g % per_tile) * LANE, (g % per_tile + 1) * LANE)
            cs = slice(g * LANE, (g + 1) * LANE)
            vn = (z_scr[n_in // 2 + t, :, ls] * inv * nv_ref[:, cs]).astype(BF16)
            ws = jnp.where(causal, ws_ref[g], 0.0).astype(BF16)
            vg = jnp.concatenate(
                [vn[c * GMLP_CHUNK:(c + 1) * GMLP_CHUNK, :] for c in range(n_chunk)], axis=1)
            sv = _dot(ws, vg) + bs_ref[:, g:g + 1]
            for c in range(n_chunk):
                rs = slice(c * GMLP_CHUNK, (c + 1) * GMLP_CHUNK)
                y_scr[rs, cs] = (z_scr[t, rs, ls] * sv[:, c * LANE:(c + 1) * LANE]).astype(BF16)

    @pl.when(j >= n_in)
    def _():
        col0 = pl.multiple_of((j - n_in) * GMLP_TN, GMLP_TN)
        o_ref[...] = x_ref[:, pl.ds(col0, GMLP_TN)] + _dot(y_scr[...], wo_ref[...].astype(BF16))


def _gmlp(x, gain, w_in, norm_v, w_s, b_s_t, w_out, layer, *, tm):
    s, d = x.shape
    width = GMLP_GROUPS * LANE
    n_in = 2 * width // GMLP_TN
    n_out = d // GMLP_TN
    return pl.pallas_call(
        functools.partial(_gmlp_kernel, tm=tm),
        grid=(s // tm, n_in + n_out),
        in_specs=[pl.BlockSpec((tm, d), lambda i, j: (i, 0), pipeline_mode=pl.Buffered(1)),
                  pl.BlockSpec((1, d), lambda i, j: (0, 0)),
                  pl.BlockSpec((None, d, GMLP_TN), lambda i, j: (layer, 0, jnp.minimum(j, n_in - 1))),
                  pl.BlockSpec((1, width), lambda i, j: (0, 0)),
                  pl.BlockSpec((None, GMLP_GROUPS, GMLP_CHUNK, GMLP_CHUNK), lambda i, j: (layer, 0, 0, 0)),
                  pl.BlockSpec((GMLP_CHUNK, GMLP_GROUPS), lambda i, j: (0, 0)),
                  pl.BlockSpec((None, width, GMLP_TN), lambda i, j: (layer, 0, jnp.maximum(j - n_in, 0)))],
        out_specs=pl.BlockSpec((tm, GMLP_TN), lambda i, j: (i, jnp.maximum(j - n_in, 0))),
        out_shape=jax.ShapeDtypeStruct((s, d), F32),
        scratch_shapes=[pltpu.VMEM((tm, d), BF16),
                        pltpu.VMEM((n_in, tm, GMLP_TN), F32),
                        pltpu.VMEM((tm, width), BF16)],
        compiler_params=_cparams(("arbitrary", "arbitrary")),
    )(x, gain.reshape(1, d), w_in, norm_v.reshape(1, width), w_s, b_s_t, w_out)


def _compress_kernel(h_ref, pe_ref, w1_ref, w2_ref, kn_ref, o_ref, *, n_half):
    half_w = CMP_STRIDE * HEAD_DIM
    hh = h_ref[...]
    pe = pe_ref[...]
    a = _dot((hh + pe[0:1, :]).astype(BF16), w1_ref[0:half_w, :].astype(BF16))
    b = _dot((hh + pe[1:2, :]).astype(BF16), w1_ref[half_w:2 * half_w, :].astype(BF16))
    pre = a + pltpu.roll(b, n_half - 1, 0)
    out = _dot(jax.nn.gelu(pre).astype(BF16), w2_ref[...].astype(BF16))
    is_key = pl.program_id(0) < N_KV_GROUPS
    out = jnp.where(is_key, _rms_rows(out, kn_ref[...]), out)
    row = lax.broadcasted_iota(jnp.int32, out.shape, 0)
    o_ref[...] = jnp.where(row < n_half - 1, out, 0.0)


def _compress(halves, pe, w1, w2, k_norm0):
    n_half = halves.shape[1]
    half_w = CMP_STRIDE * HEAD_DIM
    hid = w1.shape[2]
    return pl.pallas_call(
        functools.partial(_compress_kernel, n_half=n_half),
        grid=(2 * N_KV_GROUPS,),
        in_specs=[pl.BlockSpec((None, n_half, half_w), lambda n: (n, 0, 0)),
                  pl.BlockSpec((None, 2, half_w), lambda n: (n // N_KV_GROUPS, 0, 0)),
                  pl.BlockSpec((None, 2 * half_w, hid), lambda n: (n // N_KV_GROUPS, 0, 0)),
                  pl.BlockSpec((None, hid, HEAD_DIM), lambda n: (n // N_KV_GROUPS, 0, 0)),
                  pl.BlockSpec((1, HEAD_DIM), lambda n: (0, 0))],
        out_specs=pl.BlockSpec((None, n_half, HEAD_DIM), lambda n: (n, 0, 0)),
        out_shape=jax.ShapeDtypeStruct((2 * N_KV_GROUPS, n_half, HEAD_DIM), F32),
        compiler_params=_cparams(("arbitrary",)),
    )(halves, pe, w1, w2, k_norm0.reshape(1, HEAD_DIM))


def _t5_bucket_np(dist):
    n = np.maximum(dist, 0)
    max_exact = N_BUCKETS // 2
    nf = np.maximum(n, 1).astype(np.float32)
    large = max_exact + (np.log(nf / np.float32(max_exact)) / np.float32(math.log(MAX_DISTANCE / max_exact))
                         * np.float32(N_BUCKETS - max_exact)).astype(np.int32)
    large = np.minimum(large, N_BUCKETS - 1)
    return np.where(n < max_exact, n, large).astype(np.int32)


N_PATTERNS = 4


def _bucket_patterns():
    i = np.arange(Q_BLOCK)[:, None]
    c = np.arange(LANE)[None, :]
    d0 = i - c
    d1 = i - c + Q_BLOCK
    dc = i - CMP_STRIDE * (c - (LANE - 8)) - (CMP_BLOCK - 1)
    pats = [np.where(d >= 0, _t5_bucket_np(d), -1) for d in (d0, d1, dc)]
    pats.append(np.where(i < c, N_BUCKETS - 1, -1))
    return np.stack([p.T for p in pats]).astype(np.int32)


def _bias_tiles_kernel(tab_ref, pat_ref, o_ref):
    h = pl.program_id(0)
    pat = pat_ref[...]
    far = tab_ref[N_BUCKETS - 1, h]
    acc = jnp.full(pat.shape, NEG, F32)
    for b in range(N_BUCKETS):
        acc = jnp.where(pat == b, (tab_ref[b, h] - far) * LOG2E, acc)
    o_ref[...] = acc


def _bias_tiles(rel_bias):
    pats = jnp.asarray(_bucket_patterns())
    return pl.pallas_call(
        _bias_tiles_kernel,
        grid=(N_HEADS,),
        in_specs=[pl.BlockSpec(memory_space=pltpu.SMEM),
                  pl.BlockSpec((N_PATTERNS, Q_BLOCK, LANE), lambda h: (0, 0, 0))],
        out_specs=pl.BlockSpec((None, N_PATTERNS, Q_BLOCK, LANE), lambda h: (h, 0, 0, 0)),
        out_shape=jax.ShapeDtypeStruct((N_HEADS, N_PATTERNS, Q_BLOCK, LANE), F32),
        compiler_params=_cparams(("arbitrary",)),
    )(rel_bias, pats)


def _with_features(qs, feat):
    reps = qs.shape[0] // feat.shape[0]
    return jnp.concatenate([qs, jnp.concatenate([feat] * reps, axis=0)], axis=1)


def _softmax_cols(blocks):
    m = blocks[0].max(axis=0, keepdims=True)
    for b in blocks[1:]:
        m = jnp.maximum(m, b.max(axis=0, keepdims=True))
    es = [jnp.exp2(b - m) for b in blocks]
    den = es[0].sum(axis=0, keepdims=True)
    for e in es[1:]:
        den = den + e.sum(axis=0, keepdims=True)
    return es, m, den


def _nsa_kernel(q_ref, gt_ref, kc_ref, vc_ref, vct_ref, ovl_ref, ovlt_ref, ks_ref, vsf_ref, vsn_ref,
                kw_ref, vwn_ref, tb_ref, o_ref, m_scr, l_scr, acc_scr, sa_scr, sb_scr, *, n_slc):
    qb = pl.program_id(1)
    s = qb * Q_BLOCK
    hpg = HEADS_PER_GROUP

    q_all = q_ref[...]
    qs = jnp.concatenate([q_all[:, h * LANE:(h + 1) * LANE] for h in range(hpg)], axis=0)
    cols = [slice(h * Q_BLOCK, (h + 1) * Q_BLOCK) for h in range(hpg)]
    lane_f = lax.broadcasted_iota(jnp.int32, (Q_BLOCK, LANE), 1)
    pad_feat = jnp.where(lane_f == LANE - 1, 1.0, 0.0).astype(BF16)
    q_pad = _with_features(qs, pad_feat)

    n_sub = NEAR // LANE
    row0 = pl.multiple_of(s, Q_BLOCK)

    n_pair = hpg // 2
    pair_rows = [slice(pr * 2 * Q_BLOCK, (pr + 1) * 2 * Q_BLOCK) for pr in range(n_pair)]
    half = [slice(0, Q_BLOCK), slice(Q_BLOCK, 2 * Q_BLOCK)]

    def run_stages(stages):
        pending = stages[0][0]()
        for i, (_, consume) in enumerate(stages):
            cur = pending
            if i + 1 < len(stages):
                pending = stages[i + 1][0]()
            consume(cur)

    def near_stages(k_ref, q_aug, vn_ref, oldest_pat, res):
        k_aug = k_ref[pl.ds(row0, NEAR), :]
        v_blocks = vn_ref[pl.ds(qb, n_sub)]
        v_t = jnp.concatenate([v_blocks[u] for u in range(n_sub)], axis=1)

        def stage(pr):
            def issue():
                return _dot_nt(k_aug, q_aug[pair_rows[pr], :])

            def consume(logits):
                ps = []
                for hh in range(2):
                    h = 2 * pr + hh
                    blocks = [logits[u * LANE:(u + 1) * LANE, half[hh]] for u in range(n_sub)]
                    blocks[n_sub - 1] = blocks[n_sub - 1] + tb_ref[h, 0]
                    blocks[n_sub - 2] = blocks[n_sub - 2] + tb_ref[h, 1]
                    if oldest_pat is not None:
                        blocks[0] = blocks[0] + tb_ref[h, oldest_pat]
                    es, m, den = _softmax_cols(blocks)
                    ps.append(jnp.concatenate(es, axis=0).astype(BF16))
                    res["m"].append(m)
                    res["l"].append(den)
                res["o"].append(_dot(v_t, jnp.concatenate(ps, axis=1)))
            return issue, consume
        return [stage(pr) for pr in range(n_pair)]

    near0 = pl.multiple_of(qb * 8 + 8, 8)
    kc = kc_ref[CMP_PAD:, :].astype(BF16)
    kn = kc_ref[pl.ds(near0, LANE), :].astype(BF16)
    vn_t = vc_ref[pl.ds(near0, LANE), :].T.astype(BF16)
    ovl_n_t = ovl_ref[pl.ds(near0, LANE), :].T.astype(BF16)
    far_feat = jnp.where(lane_f > qb - 16, 1.0, 0.0).astype(BF16)
    q_cmp = _with_features(qs, far_feat)
    vc_t = vct_ref[:, CMP_PAD:]
    t_row = s + lax.broadcasted_iota(jnp.int32, (1, Q_BLOCK), 1)
    row_ok = t_row >= CMP_BLOCK - 1
    cmp_res = {"o": [], "psum_f": jnp.zeros((kc.shape[0], Q_BLOCK), F32), "psum_n": jnp.zeros((LANE, Q_BLOCK), F32)}

    def cmp_stage(pr):
        def issue():
            return _dot_nt(kc, q_cmp[pair_rows[pr], :]), _dot_nt(kn, q_pad[pair_rows[pr], :])

        def consume(logits):
            sf, sn = logits
            pfs, pns = [], []
            for hh in range(2):
                (ef, en), _, den = _softmax_cols([sf[:, half[hh]], sn[:, half[hh]] + tb_ref[2 * pr + hh, 2]])
                inv = jnp.where(row_ok, 1.0 / den, 0.0)
                pf = ef * inv
                pn = en * inv
                cmp_res["psum_f"] = cmp_res["psum_f"] + pf
                cmp_res["psum_n"] = cmp_res["psum_n"] + pn
                pfs.append(pf.astype(BF16))
                pns.append(pn.astype(BF16))
            cmp_res["o"].append(_dot(vc_t, jnp.concatenate(pfs, axis=1)) + _dot(vn_t, jnp.concatenate(pns, axis=1)))
        return issue, consume

    blk = lax.broadcasted_iota(jnp.int32, (n_slc, Q_BLOCK), 0)
    blk_f = blk.astype(F32)
    cur = (s + lax.broadcasted_iota(jnp.int32, (n_slc, Q_BLOCK), 1)) >> 6
    forced = (blk == 0) | (blk == cur) | (blk == cur - 1)
    valid = blk <= cur
    topk = {}

    def importance_stage():
        def consume(_):
            psum_f, psum_n = cmp_res["psum_f"], cmp_res["psum_n"]
            ovl_t = ovlt_ref[:, CMP_PAD:]
            pf_hi = psum_f.astype(BF16)
            pf_lo = (psum_f - pf_hi.astype(F32)).astype(BF16)
            pn_hi = psum_n.astype(BF16)
            pn_lo = (psum_n - pn_hi.astype(F32)).astype(BF16)
            imp_t = _dot(ovl_t, pf_hi) + _dot(ovl_t, pf_lo) + _dot(ovl_n_t, pn_hi) + _dot(ovl_n_t, pn_lo)
            topk["score"] = jnp.where(valid & ~forced, imp_t, -BIG)
            topk["sel"] = jnp.where(forced, 1.0, 0.0)
        return (lambda: None), consume

    def topk_stage(rounds):
        def consume(_):
            score, sel_t = topk["score"], topk["sel"]
            for _r in range(rounds):
                top = jnp.max(score, axis=0, keepdims=True)
                first = jnp.min(jnp.where(score == top, blk_f, float(n_slc)), axis=0, keepdims=True)
                pick = blk_f == first
                sel_t = jnp.where(pick, 1.0, sel_t)
                score = jnp.where(pick, -2.0 * BIG, score)
            topk["score"], topk["sel"] = score, sel_t
        return (lambda: None), consume

    win_res = {"o": [], "m": [], "l": []}
    win_stages = near_stages(kw_ref, q_pad, vwn_ref, 3, win_res)
    free_picks = max(min(N_SELECT, n_slc) - 3, 0)
    rounds = [free_picks // n_pair + (1 if pr < free_picks % n_pair else 0) for pr in range(n_pair)]
    stages = [cmp_stage(pr) for pr in range(n_pair)] + [importance_stage()]
    for pr in range(n_pair):
        stages += [win_stages[pr], topk_stage(rounds[pr])]
    run_stages(stages)
    o_win = jnp.concatenate(win_res["o"], axis=1)
    l_win = jnp.concatenate(win_res["l"], axis=1)
    o_cmp = jnp.concatenate(cmp_res["o"], axis=1)
    sel_t = topk["sel"]
    drop_t = jnp.where(valid, 1.0 - sel_t, 1.0)
    near_blk = (s - WINDOW) >> 6
    drop_far_t = jnp.where(blk >= near_blk, 1.0, drop_t)
    q_near = _with_features(qs, drop_t.T.astype(BF16))
    q_far = _with_features(qs, drop_far_t.T.astype(BF16))

    def far_logits(t):
        r0 = pl.multiple_of(WINDOW + KV_TILE * t, KV_TILE)
        return _dot_nt(ks_ref[pl.ds(r0, KV_TILE), :], q_far)

    sa_scr[...] = far_logits(0)
    slc_res = {"o": [], "m": [], "l": []}
    run_stages(near_stages(ks_ref, q_near, vsn_ref, None, slc_res))
    m_scr[...] = jnp.concatenate(slc_res["m"], axis=1)
    l_scr[...] = jnp.concatenate(slc_res["l"], axis=1)
    acc_scr[...] = jnp.concatenate(slc_res["o"], axis=1)

    gates = gt_ref[...]
    g_c, g_s, g_w = [jnp.concatenate([gates[3 * h + br:3 * h + br + 1, :] for h in range(hpg)], axis=1)
                     for br in range(N_BRANCH)]
    out_cw = g_c * o_cmp + (g_w / l_win) * o_win

    n_far = (jnp.maximum(s - WINDOW, 0) + KV_TILE - 1) // KV_TILE

    n_tiles = vsf_ref.shape[0]

    def far_update(st_ref, t):
        m_old = m_scr[...]
        sts = [st_ref[:, cols[h]] for h in range(hpg)]
        m_new = jnp.concatenate(
            [jnp.maximum(m_old[:, cols[h]], sts[h].max(axis=0, keepdims=True)) for h in range(hpg)], axis=1)
        es = [jnp.exp2(sts[h] - m_new[:, cols[h]]) for h in range(hpg)]
        den = jnp.concatenate([e.sum(axis=0, keepdims=True) for e in es], axis=1)
        p_t = jnp.concatenate([e.astype(BF16) for e in es], axis=1)
        alpha = jnp.exp2(m_old - m_new)
        m_scr[...] = m_new
        l_scr[...] = alpha * l_scr[...] + den
        acc_scr[...] = alpha * acc_scr[...] + _dot(vsf_ref[t], p_t)

    def far_body(i, carry):
        t0 = 2 * i
        sb_scr[...] = far_logits(t0 + 1)
        far_update(sa_scr, t0)
        sa_scr[...] = far_logits(jnp.minimum(t0 + 2, n_tiles - 1))
        far_update(sb_scr, t0 + 1)
        return carry

    lax.fori_loop(0, (n_far + 1) // 2, far_body, 0)

    out_t = out_cw + (g_s / l_scr[...]) * acc_scr[...]
    for h in range(hpg):
        o_ref[:, cols[h]] = out_t[:, cols[h]].T.astype(o_ref.dtype)


def _nsa_attention(q, gates_t, k_cmp, v_cmp, v_cmp_t, ovl, ovl_t, ks, vs_far, vs_near, kw, vw_near, bias_tiles):
    s = q.shape[0]
    n_qb = s // Q_BLOCK
    n_slc = s // SLC_BLOCK
    gw = HEADS_PER_GROUP * HEAD_DIM
    lanes = HEADS_PER_GROUP * Q_BLOCK

    def group_spec(a):
        zeros = (0,) * (a.ndim - 1)
        return pl.BlockSpec((None,) + a.shape[1:], lambda g, i: (g,) + zeros)

    def whole_spec(a):
        zeros = (0,) * a.ndim
        return pl.BlockSpec(a.shape, lambda g, i: zeros)

    return pl.pallas_call(
        functools.partial(_nsa_kernel, n_slc=n_slc),
        grid=(N_KV_GROUPS, n_qb),
        in_specs=[pl.BlockSpec((Q_BLOCK, gw), lambda g, i: (i, g)),
                  pl.BlockSpec((LANE, Q_BLOCK), lambda g, i: (g, i)),
                  group_spec(k_cmp), group_spec(v_cmp), group_spec(v_cmp_t),
                  whole_spec(ovl), whole_spec(ovl_t),
                  group_spec(ks), group_spec(vs_far), group_spec(vs_near),
                  group_spec(kw), group_spec(vw_near),
                  pl.BlockSpec((HEADS_PER_GROUP, N_PATTERNS, LANE, Q_BLOCK), lambda g, i: (g, 0, 0, 0))],
        out_specs=pl.BlockSpec((Q_BLOCK, gw), lambda g, i: (i, g)),
        out_shape=jax.ShapeDtypeStruct((s, N_KV_GROUPS * gw), BF16),
        scratch_shapes=[pltpu.VMEM((1, lanes), F32),
                        pltpu.VMEM((1, lanes), F32),
                        pltpu.VMEM((HEAD_DIM, lanes), F32),
                        pltpu.VMEM((KV_TILE, lanes), F32),
                        pltpu.VMEM((KV_TILE, lanes), F32)],
        compiler_params=_cparams(("arbitrary", "arbitrary")),
    )(q, gates_t, k_cmp, v_cmp, v_cmp_t, ovl, ovl_t, ks, vs_far, vs_near, kw, vw_near, bias_tiles)


def _cmp_mask_columns(n_half):
    assert n_half // 8 < LANE - 1
    out = np.zeros((CMP_PAD + n_half, LANE), np.float32)
    out[np.arange(CMP_PAD), LANE - 1] = NEG
    k = np.arange(n_half)
    out[CMP_PAD + k, k // 8] = NEG
    return out


def _slc_mask_columns(s):
    n_slc = s // SLC_BLOCK
    assert n_slc - 1 > (WINDOW + Q_BLOCK) // SLC_BLOCK
    out = np.zeros((WINDOW + s, n_slc), np.float32)
    out[np.arange(WINDOW), n_slc - 1] = NEG
    pos = np.arange(s)
    out[WINDOW + pos, pos // SLC_BLOCK] = NEG
    return out


def _pad_mask_columns(s):
    out = np.zeros((WINDOW + s, LANE), np.float32)
    out[np.arange(WINDOW), LANE - 1] = NEG
    return out


def _overlap_padded(n_half, n_slc):
    n_cmp = n_half - 1
    c0 = np.arange(n_cmp) * CMP_STRIDE
    s0 = np.arange(n_slc) * SLC_BLOCK
    lo = np.maximum(c0[:, None], s0[None, :])
    hi = np.minimum(c0[:, None] + CMP_BLOCK, s0[None, :] + SLC_BLOCK)
    ovl = np.maximum(hi - lo, 0).astype(np.float32) / CMP_BLOCK
    out = np.zeros((CMP_PAD + n_half, n_slc), np.float32)
    out[CMP_PAD:CMP_PAD + n_cmp] = ovl
    return out


def _row_tile(s):
    return min(1024, s)


def _ffn_ple(x, p, layer, b, norm_ffn, w_in, w_out, norm_ple, ple_w, ple_gate):
    s = x.shape[0]
    tm = _row_tile(s)
    tn = 512
    nj = FFN_DIM // tn
    act = _norm_matmul(x, norm_ffn, (w_in, layer), (0, nj), tm=tm, tn=tn, nj=nj,
                       epilogue=_ep_swiglu, out_dtype=BF16)
    x = _matmul_res(act, (w_out, layer), x, tm=tm, tn=256)
    aux = (p, ple_w)
    aux_specs = (pl.BlockSpec((None, None, tm, PLE_DIM), lambda i, j: (layer, b, i, 0)),
                 pl.BlockSpec((None, PLE_DIM, tn), lambda i, j: (layer, 0, j)))
    return _norm_matmul(x, norm_ple, (ple_gate, layer), (0,), tm=tm, tn=tn, nj=D_MODEL // tn,
                        epilogue=_ep_ple, out_dtype=F32, aux=aux, aux_specs=aux_specs)


def _gmlp_layer(x, layer, norm_mix, w_in, norm_v, w_s, b_s, w_out):
    return _gmlp(x, norm_mix, w_in, norm_v, w_s, jnp.transpose(b_s), w_out, layer, tm=_row_tile(x.shape[0]))


def _shared_kv(x, kv_norm, kv_w, k_norm, cmp_pe_k, cmp_pe_v, cmp_wk1, cmp_wk2, cmp_wv1, cmp_wv2):
    s = x.shape[0]
    tm = _row_tile(s)
    gw = N_KV_GROUPS * HEAD_DIM
    kvc = _norm_matmul(x, kv_norm, kv_w, (0,), tm=tm, tn=2 * gw, nj=1,
                       epilogue=_ep_identity, out_dtype=F32)
    gains = jnp.stack([jnp.tile(k_norm[1], N_KV_GROUPS), jnp.ones((gw,), F32),
                       jnp.tile(k_norm[2], N_KV_GROUPS), jnp.ones((gw,), F32)])
    gains = jnp.broadcast_to(gains[:, None, :], (4, 8, gw))
    kvr = _norm_matmul(x, kv_norm, kv_w, (2,), tm=tm, tn=gw, nj=4,
                       epilogue=_ep_kv, out_dtype=BF16, aux=(gains,),
                       aux_specs=(pl.BlockSpec((None, 8, gw), lambda i, j: (j, 0, 0)),))
    n_half = s // CMP_STRIDE
    halves = kvc.reshape(n_half, CMP_STRIDE, 2 * N_KV_GROUPS, HEAD_DIM)
    halves = jnp.transpose(halves, (2, 0, 1, 3)).reshape(2 * N_KV_GROUPS, n_half, CMP_STRIDE * HEAD_DIM)
    pe = jnp.stack([cmp_pe_k, cmp_pe_v]).reshape(2, 2, CMP_STRIDE * HEAD_DIM)
    kv_cmp = _compress(halves, pe, jnp.stack([cmp_wk1, cmp_wv1]), jnp.stack([cmp_wk2, cmp_wv2]), k_norm[0])
    kv_cmp = jnp.pad(kv_cmp, ((0, 0), (CMP_PAD, 0), (0, 0)))
    cmp_cols = jnp.broadcast_to(jnp.asarray(_cmp_mask_columns(n_half)), (N_KV_GROUPS, CMP_PAD + n_half, LANE))
    k_cmp = jnp.concatenate([kv_cmp[:N_KV_GROUPS], cmp_cols], axis=2)
    v_cmp = kv_cmp[N_KV_GROUPS:]
    v_cmp_t = jnp.transpose(v_cmp, (0, 2, 1)).astype(BF16)
    s_pad = WINDOW + s
    kvr = jnp.pad(kvr, ((WINDOW, 0), (0, 0))).reshape(s_pad, 4, N_KV_GROUPS, HEAD_DIM)
    kvr = jnp.transpose(kvr, (1, 2, 0, 3))
    slc_cols = jnp.asarray(_slc_mask_columns(s)).astype(BF16)
    pad_cols = jnp.asarray(_pad_mask_columns(s)).astype(BF16)
    ks = jnp.concatenate([kvr[0], jnp.broadcast_to(slc_cols, (N_KV_GROUPS,) + slc_cols.shape)], axis=2)
    kw = jnp.concatenate([kvr[2], jnp.broadcast_to(pad_cols, (N_KV_GROUPS,) + pad_cols.shape)], axis=2)

    def key_tiles_t(v, tile):
        return jnp.transpose(v.reshape(N_KV_GROUPS, -1, tile, HEAD_DIM), (0, 1, 3, 2))

    vs_far = key_tiles_t(kvr[1][:, WINDOW:], KV_TILE)
    return k_cmp, v_cmp, v_cmp_t, ks, vs_far, key_tiles_t(kvr[1], LANE), kw, key_tiles_t(kvr[3], LANE)


def _nsa_layer(x, layer, norm_mix, w_in, q_norm, w_out, kvs):
    s = x.shape[0]
    tm = _row_tile(s)
    nq = N_HEADS * HEAD_DIM
    scale = HEAD_DIM ** -0.5 * LOG2E
    q_gain = jnp.tile(q_norm * scale, 4).reshape(1, 4 * HEAD_DIM)
    q = _norm_matmul(x, norm_mix, (w_in, layer), (0,), tm=tm, tn=512, nj=nq // 512,
                     epilogue=_ep_q, out_dtype=BF16, aux=(q_gain,),
                     aux_specs=(pl.BlockSpec((1, 4 * HEAD_DIM), lambda i, j: (0, 0)),))
    w_gate = w_in[layer, :, nq:].reshape(D_MODEL, N_KV_GROUPS, HEADS_PER_GROUP * N_BRANCH)
    w_gate = jnp.pad(w_gate, ((0, 0), (0, 0), (0, LANE - HEADS_PER_GROUP * N_BRANCH)))
    w_gate = w_gate.reshape(D_MODEL, N_KV_GROUPS * LANE)
    gates = _norm_matmul(x, norm_mix, w_gate, (0,), tm=tm, tn=N_KV_GROUPS * LANE, nj=1,
                         epilogue=_ep_sigmoid, out_dtype=F32)
    o = _nsa_attention(q, jnp.transpose(gates), *kvs)
    return _matmul_res(o, (w_out, layer), x, tm=tm, tn=512)


def kernel(x, p, norm_mix, norm_ffn, norm_ple, a_w_in, a_norm_v, a_w_s, a_b_s, a_w_out, kv_norm, kv_w, k_norm, cmp_pe_k, cmp_pe_v, cmp_wk1, cmp_wk2, cmp_wv1, cmp_wv2, b_w_in, b_q_norm, b_w_out, rel_bias, ffn_w_in, ffn_w_out, ple_w, ple_gate):
    batch, s, d = x.shape
    depth = norm_mix.shape[0]
    n_a = a_w_in.shape[0]
    a_w_out, b_w_in, b_w_out, kv_w, ple_gate = (
        w.astype(BF16) for w in (a_w_out, b_w_in, b_w_out, kv_w, ple_gate))
    outs = []
    for b in range(batch):
        xb = x.reshape(s, d) if batch == 1 else x[b]
        kvs = None
        for i in range(depth):
            if i < n_a:
                xb = _gmlp_layer(xb, i, norm_mix[i], a_w_in, a_norm_v[i], a_w_s, a_b_s[i], a_w_out)
            else:
                j = i - n_a
                xb = _nsa_layer(xb, j, norm_mix[i], b_w_in, b_q_norm[j], b_w_out, kvs)
            xb = _ffn_ple(xb, p, i, b, norm_ffn[i], ffn_w_in, ffn_w_out,
                          norm_ple[i], ple_w, ple_gate)
            if i == n_a - 1:
                k_cmp, v_cmp, v_cmp_t, ks, vs_far, vs_near, kw, vw_near = _shared_kv(
                    xb, kv_norm, kv_w, k_norm, cmp_pe_k, cmp_pe_v, cmp_wk1, cmp_wk2, cmp_wv1, cmp_wv2)
                ovl = _overlap_padded(s // CMP_STRIDE, s // SLC_BLOCK)
                kvs = (k_cmp, v_cmp, v_cmp_t, jnp.asarray(ovl), jnp.asarray(ovl.T).astype(BF16),
                       ks, vs_far, vs_near, kw, vw_near, _bias_tiles(rel_bias))
        outs.append(xb)
    return outs[0].reshape(1, s, d) if batch == 1 else jnp.stack(outs)
```

```python
import functools
import math

import numpy as np
import jax
import jax.numpy as jnp
from jax import lax
from jax.experimental import pallas as pl
from jax.experimental.pallas import tpu as pltpu

F32 = jnp.float32
BF16 = jnp.bfloat16

D_MODEL = 2048
PLE_DIM = 256
FFN_DIM = 5632
GMLP_CHUNK = 128
GMLP_GROUPS = 16
HEAD_DIM = 128
N_HEADS = 16
N_KV_GROUPS = 2
HEADS_PER_GROUP = 8
N_BRANCH = 3
CMP_BLOCK = 32
CMP_STRIDE = 16
SLC_BLOCK = 64
N_SELECT = 16
WINDOW = 512
Q_BLOCK = 128
N_BUCKETS = 32
MAX_DISTANCE = 128
EPS = 1e-6
NEG = -1e30
BIG = 1e30
LOG2E = math.log2(math.e)

LANE = 128
KV_TILE = 512
NEAR = WINDOW + Q_BLOCK
CMP_PAD = 128
SUM_ROWS = 16
VMEM_LIMIT = 56 * 1024 * 1024


def _cparams(sem):
    return pltpu.CompilerParams(dimension_semantics=sem, vmem_limit_bytes=VMEM_LIMIT)


def _dot(a, b):
    return jnp.dot(a, b, preferred_element_type=F32)


def _dot_nt(a, b):
    return lax.dot_general(a, b, (((1,), (1,)), ((), ())), preferred_element_type=F32)


def _rms_rows(x, g):
    ms = jnp.mean(x * x, axis=-1, keepdims=True)
    return x * lax.rsqrt(ms + EPS) * g


def _group_rms(acc, gain):
    outs = []
    for c in range(acc.shape[1] // LANE):
        a = acc[:, c * LANE:(c + 1) * LANE]
        outs.append(_rms_rows(a, gain[:, c * LANE:(c + 1) * LANE]))
    return outs[0] if len(outs) == 1 else jnp.concatenate(outs, axis=1)


def _norm_matmul_kernel(x_ref, g_ref, *refs, n_w, n_aux, epilogue, out_dtype):
    w_refs = refs[:n_w]
    aux_refs = refs[n_w:n_w + n_aux]
    o_ref = refs[n_w + n_aux]
    h_scr = refs[n_w + n_aux + 1]

    @pl.when(pl.program_id(1) == 0)
    def _():
        h_scr[...] = _rms_rows(x_ref[...], g_ref[...]).astype(BF16)

    h = h_scr[...]
    accs = [_dot(h, w_ref[...].astype(BF16)) for w_ref in w_refs]
    o_ref[...] = epilogue(accs, aux_refs + (x_ref,)).astype(out_dtype)


def _w_spec(w, k, tn, col_of):
    if isinstance(w, tuple):
        layer = w[1]
        return pl.BlockSpec((None, k, tn), lambda i, j: (layer, 0, col_of(j)))
    return pl.BlockSpec((k, tn), lambda i, j: (0, col_of(j)))


def _w_array(w):
    return w[0] if isinstance(w, tuple) else w


def _norm_matmul(x, gain, w, col_offsets, *, tm, tn, nj, epilogue, out_dtype,
                 aux=(), aux_specs=()):
    s, k = x.shape
    in_specs = [pl.BlockSpec((tm, k), lambda i, j: (i, 0)),
                pl.BlockSpec((1, k), lambda i, j: (0, 0))]
    for c0 in col_offsets:
        in_specs.append(_w_spec(w, k, tn, lambda j, c0=c0: c0 + j))
    in_specs += list(aux_specs)
    kern = functools.partial(_norm_matmul_kernel, n_w=len(col_offsets), n_aux=len(aux),
                             epilogue=epilogue, out_dtype=out_dtype)
    return pl.pallas_call(
        kern,
        grid=(s // tm, nj),
        in_specs=in_specs,
        out_specs=pl.BlockSpec((tm, tn), lambda i, j: (i, j)),
        out_shape=jax.ShapeDtypeStruct((s, nj * tn), out_dtype),
        scratch_shapes=[pltpu.VMEM((tm, k), BF16)],
        compiler_params=_cparams(("arbitrary", "arbitrary")),
    )(x, gain.reshape(1, k), *([_w_array(w)] * len(col_offsets)), *aux)


def _ep_swiglu(accs, aux):
    g, u = accs
    return g * jax.nn.sigmoid(g) * u


def _ep_ple(accs, aux):
    p_ref, wp_ref, x_ref = aux
    tn = accs[0].shape[1]
    col0 = pl.multiple_of(pl.program_id(1) * tn, tn)
    pp = _dot(p_ref[...].astype(BF16), wp_ref[...].astype(BF16))
    return x_ref[:, pl.ds(col0, tn)] + pp * jax.nn.sigmoid(accs[0])


def _ep_identity(accs, aux):
    return accs[0]


def _ep_sigmoid(accs, aux):
    return jax.nn.sigmoid(accs[0])


def _ep_q(accs, aux):
    gain_ref = aux[0]
    return _group_rms(accs[0], gain_ref[...])


def _ep_kv(accs, aux):
    gain_ref = aux[0]
    acc = accs[0]
    normed = _group_rms(acc, gain_ref[0:1, :])
    is_key = (pl.program_id(1) % 2) == 0
    return jnp.where(is_key, normed, acc)


def _matmul_res_kernel(a_ref, w_ref, x_ref, o_ref):
    o_ref[...] = x_ref[...] + _dot(a_ref[...], w_ref[...].astype(BF16))


def _matmul_res(a, w, resid, *, tm, tn):
    s, k = a.shape
    n = _w_array(w).shape[-1]
    return pl.pallas_call(
        _matmul_res_kernel,
        grid=(s // tm, n // tn),
        in_specs=[pl.BlockSpec((tm, k), lambda i, j: (i, 0)),
                  _w_spec(w, k, tn, lambda j: j),
                  pl.BlockSpec((tm, tn), lambda i, j: (i, j))],
        out_specs=pl.BlockSpec((tm, tn), lambda i, j: (i, j)),
        out_shape=jax.ShapeDtypeStruct((s, n), F32),
        compiler_params=_cparams(("arbitrary", "arbitrary")),
    )(a, _w_array(w), resid)


GMLP_TN = 512


def _gmlp_kernel(x_ref, g_ref, win_ref, nv_ref, ws_ref, bs_ref, wo_ref, o_ref, h_scr, z_scr, y_scr, *, tm):
    width = GMLP_GROUPS * LANE
    n_in = 2 * width // GMLP_TN
    per_tile = GMLP_TN // LANE
    n_chunk = tm // GMLP_CHUNK
    j = pl.program_id(1)

    @pl.when(j == 0)
    def _():
        h_scr[...] = _rms_rows(x_ref[...], g_ref[...]).astype(BF16)

    @pl.when(j < n_in)
    def _():
        z_scr[j] = jax.nn.gelu(_dot(h_scr[...], win_ref[...].astype(BF16)))

    @pl.when(j == n_in)
    def _():
        v_tiles = range(n_in // 2, n_in)
        ms = sum(jnp.sum(z_scr[t] * z_scr[t], axis=-1, keepdims=True) for t in v_tiles) / width
        inv = lax.rsqrt(ms + EPS)
        row = lax.broadcasted_iota(jnp.int32, (GMLP_CHUNK, GMLP_CHUNK), 0)
        col = lax.broadcasted_iota(jnp.int32, (GMLP_CHUNK, GMLP_CHUNK), 1)
        causal = col <= row
        for g in range(GMLP_GROUPS):
            t, ls = g // per_tile, slice((g % per_tile) * LANE, (g % per_tile + 1) * LANE)
            cs = slice(g * LANE, (g + 1) * LANE)
            vn = (z_scr[n_in // 2 + t, :, ls] * inv * nv_ref[:, cs]).astype(BF16)
            ws = jnp.where(causal, ws_ref[g], 0.0).astype(BF16)
            vg = jnp.concatenate(
                [vn[c * GMLP_CHUNK:(c + 1) * GMLP_CHUNK, :] for c in range(n_chunk)], axis=1)
            sv = _dot(ws, vg) + bs_ref[:, g:g + 1]
            for c in range(n_chunk):
                rs = slice(c * GMLP_CHUNK, (c + 1) * GMLP_CHUNK)
                y_scr[rs, cs] = (z_scr[t, rs, ls] * sv[:, c * LANE:(c + 1) * LANE]).astype(BF16)

    @pl.when(j >= n_in)
    def _():
        col0 = pl.multiple_of((j - n_in) * GMLP_TN, GMLP_TN)
        o_ref[...] = x_ref[:, pl.ds(col0, GMLP_TN)] + _dot(y_scr[...], wo_ref[...].astype(BF16))


def _gmlp(x, gain, w_in, norm_v, w_s, b_s_t, w_out, layer, *, tm):
    s, d = x.shape
    width = GMLP_GROUPS * LANE
    n_in = 2 * width // GMLP_TN
    n_out = d // GMLP_TN
    return pl.pallas_call(
        functools.partial(_gmlp_kernel, tm=tm),
        grid=(s // tm, n_in + n_out),
        in_specs=[pl.BlockSpec((tm, d), lambda i, j: (i, 0), pipeline_mode=pl.Buffered(1)),
                  pl.BlockSpec((1, d), lambda i, j: (0, 0)),
                  pl.BlockSpec((None, d, GMLP_TN), lambda i, j: (layer, 0, jnp.minimum(j, n_in - 1))),
                  pl.BlockSpec((1, width), lambda i, j: (0, 0)),
                  pl.BlockSpec((None, GMLP_GROUPS, GMLP_CHUNK, GMLP_CHUNK), lambda i, j: (layer, 0, 0, 0)),
                  pl.BlockSpec((GMLP_CHUNK, GMLP_GROUPS), lambda i, j: (0, 0)),
                  pl.BlockSpec((None, width, GMLP_TN), lambda i, j: (layer, 0, jnp.maximum(j - n_in, 0)))],
        out_specs=pl.BlockSpec((tm, GMLP_TN), lambda i, j: (i, jnp.maximum(j - n_in, 0))),
        out_shape=jax.ShapeDtypeStruct((s, d), F32),
        scratch_shapes=[pltpu.VMEM((tm, d), BF16),
                        pltpu.VMEM((n_in, tm, GMLP_TN), F32),
                        pltpu.VMEM((tm, width), BF16)],
        compiler_params=_cparams(("arbitrary", "arbitrary")),
    )(x, gain.reshape(1, d), w_in, norm_v.reshape(1, width), w_s, b_s_t, w_out)


def _compress_kernel(h_ref, pe_ref, w1_ref, w2_ref, kn_ref, o_ref, *, n_half):
    half_w = CMP_STRIDE * HEAD_DIM
    hh = h_ref[...]
    pe = pe_ref[...]
    a = _dot((hh + pe[0:1, :]).astype(BF16), w1_ref[0:half_w, :].astype(BF16))
    b = _dot((hh + pe[1:2, :]).astype(BF16), w1_ref[half_w:2 * half_w, :].astype(BF16))
    pre = a + pltpu.roll(b, n_half - 1, 0)
    out = _dot(jax.nn.gelu(pre).astype(BF16), w2_ref[...].astype(BF16))
    is_key = pl.program_id(0) < N_KV_GROUPS
    out = jnp.where(is_key, _rms_rows(out, kn_ref[...]), out)
    row = lax.broadcasted_iota(jnp.int32, out.shape, 0)
    o_ref[...] = jnp.where(row < n_half - 1, out, 0.0)


def _compress(halves, pe, w1, w2, k_norm0):
    n_half = halves.shape[1]
    half_w = CMP_STRIDE * HEAD_DIM
    hid = w1.shape[2]
    return pl.pallas_call(
        functools.partial(_compress_kernel, n_half=n_half),
        grid=(2 * N_KV_GROUPS,),
        in_specs=[pl.BlockSpec((None, n_half, half_w), lambda n: (n, 0, 0)),
                  pl.BlockSpec((None, 2, half_w), lambda n: (n // N_KV_GROUPS, 0, 0)),
                  pl.BlockSpec((None, 2 * half_w, hid), lambda n: (n // N_KV_GROUPS, 0, 0)),
                  pl.BlockSpec((None, hid, HEAD_DIM), lambda n: (n // N_KV_GROUPS, 0, 0)),
                  pl.BlockSpec((1, HEAD_DIM), lambda n: (0, 0))],
        out_specs=pl.BlockSpec((None, n_half, HEAD_DIM), lambda n: (n, 0, 0)),
        out_shape=jax.ShapeDtypeStruct((2 * N_KV_GROUPS, n_half, HEAD_DIM), F32),
        compiler_params=_cparams(("arbitrary",)),
    )(halves, pe, w1, w2, k_norm0.reshape(1, HEAD_DIM))


def _t5_bucket_np(dist):
    n = np.maximum(dist, 0)
    max_exact = N_BUCKETS // 2
    nf = np.maximum(n, 1).astype(np.float32)
    large = max_exact + (np.log(nf / np.float32(max_exact)) / np.float32(math.log(MAX_DISTANCE / max_exact))
                         * np.float32(N_BUCKETS - max_exact)).astype(np.int32)
    large = np.minimum(large, N_BUCKETS - 1)
    return np.where(n < max_exact, n, large).astype(np.int32)


N_PATTERNS = 4


def _bucket_patterns():
    i = np.arange(Q_BLOCK)[:, None]
    c = np.arange(LANE)[None, :]
    d0 = i - c
    d1 = i - c + Q_BLOCK
    dc = i - CMP_STRIDE * (c - (LANE - 8)) - (CMP_BLOCK - 1)
    pats = [np.where(d >= 0, _t5_bucket_np(d), -1) for d in (d0, d1, dc)]
    pats.append(np.where(i < c, N_BUCKETS - 1, -1))
    return np.stack([p.T for p in pats]).astype(np.int32)


def _bias_tiles_kernel(tab_ref, pat_ref, o_ref):
    h = pl.program_id(0)
    pat = pat_ref[...]
    far = tab_ref[N_BUCKETS - 1, h]
    acc = jnp.full(pat.shape, NEG, F32)
    for b in range(N_BUCKETS):
        acc = jnp.where(pat == b, (tab_ref[b, h] - far) * LOG2E, acc)
    o_ref[...] = acc


def _bias_tiles(rel_bias):
    pats = jnp.asarray(_bucket_patterns())
    return pl.pallas_call(
        _bias_tiles_kernel,
        grid=(N_HEADS,),
        in_specs=[pl.BlockSpec(memory_space=pltpu.SMEM),
                  pl.BlockSpec((N_PATTERNS, Q_BLOCK, LANE), lambda h: (0, 0, 0))],
        out_specs=pl.BlockSpec((None, N_PATTERNS, Q_BLOCK, LANE), lambda h: (h, 0, 0, 0)),
        out_shape=jax.ShapeDtypeStruct((N_HEADS, N_PATTERNS, Q_BLOCK, LANE), F32),
        compiler_params=_cparams(("arbitrary",)),
    )(rel_bias, pats)


def _with_features(qs, feat):
    reps = qs.shape[0] // feat.shape[0]
    return jnp.concatenate([qs, jnp.concatenate([feat] * reps, axis=0)], axis=1)


def _softmax_cols(blocks):
    m = blocks[0].max(axis=0, keepdims=True)
    for b in blocks[1:]:
        m = jnp.maximum(m, b.max(axis=0, keepdims=True))
    es = [jnp.exp2(b - m) for b in blocks]
    den = es[0].sum(axis=0, keepdims=True)
    for e in es[1:]:
        den = den + e.sum(axis=0, keepdims=True)
    return es, m, den


def _nsa_kernel(q_ref, gt_ref, kc_ref, vc_ref, vct_ref, ovl_ref, ovlt_ref, ks_ref, vsf_ref, vsn_ref,
                kw_ref, vwn_ref, tb_ref, o_ref, m_scr, acc_scr, sa_scr, sb_scr, *, n_slc):
    qb = pl.program_id(1)
    s = qb * Q_BLOCK
    hpg = HEADS_PER_GROUP

    q_all = q_ref[...]
    qs = jnp.concatenate([q_all[:, h * LANE:(h + 1) * LANE] for h in range(hpg)], axis=0)
    cols = [slice(h * Q_BLOCK, (h + 1) * Q_BLOCK) for h in range(hpg)]
    lane_f = lax.broadcasted_iota(jnp.int32, (Q_BLOCK, LANE), 1)
    pad_feat = jnp.where(lane_f == LANE - 1, 1.0, 0.0).astype(BF16)
    q_pad = _with_features(qs, pad_feat)

    n_sub = NEAR // LANE
    row0 = pl.multiple_of(s, Q_BLOCK)

    n_pair = hpg // 2
    pair_rows = [slice(pr * 2 * Q_BLOCK, (pr + 1) * 2 * Q_BLOCK) for pr in range(n_pair)]
    half = [slice(0, Q_BLOCK), slice(Q_BLOCK, 2 * Q_BLOCK)]

    def run_stages(stages):
        pending = stages[0][0]()
        for i, (_, consume) in enumerate(stages):
            cur = pending
            if i + 1 < len(stages):
                pending = stages[i + 1][0]()
            consume(cur)

    def near_stages(k_ref, q_aug, vn_ref, oldest_pat, res):
        k_aug = k_ref[pl.ds(row0, NEAR), :]
        v_blocks = vn_ref[pl.ds(qb, n_sub)]
        v_t = jnp.concatenate([v_blocks[u] for u in range(n_sub)], axis=1)

        def stage(pr):
            def issue():
                return _dot_nt(k_aug, q_aug[pair_rows[pr], :])

            def consume(logits):
                ps = []
                for hh in range(2):
                    h = 2 * pr + hh
                    blocks = [logits[u * LANE:(u + 1) * LANE, half[hh]] for u in range(n_sub)]
                    blocks[n_sub - 1] = blocks[n_sub - 1] + tb_ref[h, 0]
                    blocks[n_sub - 2] = blocks[n_sub - 2] + tb_ref[h, 1]
                    if oldest_pat is not None:
                        blocks[0] = blocks[0] + tb_ref[h, oldest_pat]
                    m = blocks[0].max(axis=0, keepdims=True)
                    for b in blocks[1:]:
                        m = jnp.maximum(m, b.max(axis=0, keepdims=True))
                    ps.append(jnp.concatenate([jnp.exp2((b - m).astype(BF16)) for b in blocks], axis=0))
                    res["m"].append(m)
                res["o"].append(_dot(v_t, jnp.concatenate(ps, axis=1)))
            return issue, consume
        return [stage(pr) for pr in range(n_pair)]

    near0 = pl.multiple_of(qb * 8 + 8, 8)
    kc = kc_ref[CMP_PAD:, :].astype(BF16)
    kn = kc_ref[pl.ds(near0, LANE), :].astype(BF16)
    vn_t = vc_ref[pl.ds(near0, LANE), :].T.astype(BF16)
    ovl_n_t = ovl_ref[pl.ds(near0, LANE), :].T.astype(BF16)
    far_feat = jnp.where(lane_f > qb - 16, 1.0, 0.0).astype(BF16)
    q_cmp = _with_features(qs, far_feat)
    vc_t = vct_ref[:, CMP_PAD:]
    t_row = s + lax.broadcasted_iota(jnp.int32, (1, Q_BLOCK), 1)
    row_ok = t_row >= CMP_BLOCK - 1
    cmp_res = {"o": [], "psum_f": jnp.zeros((kc.shape[0], Q_BLOCK), F32), "psum_n": jnp.zeros((LANE, Q_BLOCK), F32)}

    def cmp_stage(pr):
        def issue():
            return _dot_nt(kc, q_cmp[pair_rows[pr], :]), _dot_nt(kn, q_pad[pair_rows[pr], :])

        def consume(logits):
            sf, sn = logits
            pfs, pns = [], []
            for hh in range(2):
                (ef, en), _, den = _softmax_cols([sf[:, half[hh]], sn[:, half[hh]] + tb_ref[2 * pr + hh, 2]])
                inv = jnp.where(row_ok, 1.0 / den, 0.0)
                pf = ef * inv
                pn = en * inv
                cmp_res["psum_f"] = cmp_res["psum_f"] + pf
                cmp_res["psum_n"] = cmp_res["psum_n"] + pn
                pfs.append(pf.astype(BF16))
                pns.append(pn.astype(BF16))
            cmp_res["o"].append(_dot(vc_t, jnp.concatenate(pfs, axis=1)) + _dot(vn_t, jnp.concatenate(pns, axis=1)))
        return issue, consume

    blk = lax.broadcasted_iota(jnp.int32, (n_slc, Q_BLOCK), 0)
    blk_f = blk.astype(F32)
    cur = (s + lax.broadcasted_iota(jnp.int32, (n_slc, Q_BLOCK), 1)) >> 6
    forced = (blk == 0) | (blk == cur) | (blk == cur - 1)
    valid = blk <= cur
    topk = {}

    def importance_stage():
        def consume(_):
            psum_f, psum_n = cmp_res["psum_f"], cmp_res["psum_n"]
            ovl_t = ovlt_ref[:, CMP_PAD:]
            pf_hi = psum_f.astype(BF16)
            pf_lo = (psum_f - pf_hi.astype(F32)).astype(BF16)
            pn_hi = psum_n.astype(BF16)
            pn_lo = (psum_n - pn_hi.astype(F32)).astype(BF16)
            imp_t = _dot(ovl_t, pf_hi) + _dot(ovl_t, pf_lo) + _dot(ovl_n_t, pn_hi) + _dot(ovl_n_t, pn_lo)
            topk["score"] = jnp.where(valid & ~forced, imp_t, -BIG)
            topk["sel"] = jnp.where(forced, 1.0, 0.0)
        return (lambda: None), consume

    def topk_stage(rounds):
        def consume(_):
            score, sel_t = topk["score"], topk["sel"]
            for _r in range(rounds):
                top = jnp.max(score, axis=0, keepdims=True)
                first = jnp.min(jnp.where(score == top, blk_f, float(n_slc)), axis=0, keepdims=True)
                pick = blk_f == first
                sel_t = jnp.where(pick, 1.0, sel_t)
                score = jnp.where(pick, -2.0 * BIG, score)
            topk["score"], topk["sel"] = score, sel_t
        return (lambda: None), consume

    win_res = {"o": [], "m": []}
    win_stages = near_stages(kw_ref, q_pad, vwn_ref, 3, win_res)
    free_picks = max(min(N_SELECT, n_slc) - 3, 0)
    rounds = [free_picks // n_pair + (1 if pr < free_picks % n_pair else 0) for pr in range(n_pair)]
    stages = [cmp_stage(pr) for pr in range(n_pair)] + [importance_stage()]
    for pr in range(n_pair):
        stages += [win_stages[pr], topk_stage(rounds[pr])]
    run_stages(stages)
    o_win = jnp.concatenate(win_res["o"], axis=1)
    o_cmp = jnp.concatenate(cmp_res["o"], axis=1)
    sel_t = topk["sel"]
    drop_t = jnp.where(valid, 1.0 - sel_t, 1.0)
    near_blk = (s - WINDOW) >> 6
    drop_far_t = jnp.where(blk >= near_blk, 1.0, drop_t)
    q_near = _with_features(qs, drop_t.T.astype(BF16))
    q_far = _with_features(qs, drop_far_t.T.astype(BF16))

    def far_logits(t):
        r0 = pl.multiple_of(WINDOW + KV_TILE * t, KV_TILE)
        return _dot_nt(ks_ref[pl.ds(r0, KV_TILE), :], q_far)

    sa_scr[...] = far_logits(0)
    slc_res = {"o": [], "m": []}
    run_stages(near_stages(ks_ref, q_near, vsn_ref, None, slc_res))
    m_scr[...] = jnp.concatenate(slc_res["m"], axis=1)
    acc_scr[...] = jnp.concatenate(slc_res["o"], axis=1)

    gates = gt_ref[...]
    g_c, g_s, g_w = [jnp.concatenate([gates[3 * h + br:3 * h + br + 1, :] for h in range(hpg)], axis=1)
                     for br in range(N_BRANCH)]
    out_cw = g_c * o_cmp + (g_w / o_win[HEAD_DIM:HEAD_DIM + 1, :]) * o_win[:HEAD_DIM, :]

    n_far = (jnp.maximum(s - WINDOW, 0) + KV_TILE - 1) // KV_TILE

    n_tiles = vsf_ref.shape[0]

    def far_probs(st_ref):
        m_old = m_scr[...]
        sts = [st_ref[:, cols[h]] for h in range(hpg)]
        m_new = jnp.concatenate(
            [jnp.maximum(m_old[:, cols[h]], sts[h].max(axis=0, keepdims=True)) for h in range(hpg)], axis=1)
        p_t = jnp.concatenate(
            [jnp.exp2((sts[h] - m_new[:, cols[h]]).astype(BF16)) for h in range(hpg)], axis=1)
        m_scr[...] = m_new
        return p_t, jnp.exp2(m_old - m_new)

    def far_accumulate(t, p_t, alpha):
        acc_scr[...] = alpha * acc_scr[...] + _dot(vsf_ref[t], p_t)

    def far_body(i, carry):
        t0 = 2 * i
        sb_scr[...] = far_logits(t0 + 1)
        p_a, alpha_a = far_probs(sa_scr)
        sa_scr[...] = far_logits(jnp.minimum(t0 + 2, n_tiles - 1))
        far_accumulate(t0, p_a, alpha_a)
        p_b, alpha_b = far_probs(sb_scr)
        far_accumulate(t0 + 1, p_b, alpha_b)
        return carry

    lax.fori_loop(0, (n_far + 1) // 2, far_body, 0)

    out_t = out_cw + (g_s / acc_scr[HEAD_DIM:HEAD_DIM + 1, :]) * acc_scr[:HEAD_DIM, :]
    for h in range(hpg):
        o_ref[:, cols[h]] = out_t[:, cols[h]].T.astype(o_ref.dtype)


def _nsa_attention(q, gates_t, k_cmp, v_cmp, v_cmp_t, ovl, ovl_t, ks, vs_far, vs_near, kw, vw_near, bias_tiles):
    s = q.shape[0]
    n_qb = s // Q_BLOCK
    n_slc = s // SLC_BLOCK
    gw = HEADS_PER_GROUP * HEAD_DIM
    lanes = HEADS_PER_GROUP * Q_BLOCK

    def group_spec(a):
        zeros = (0,) * (a.ndim - 1)
        return pl.BlockSpec((None,) + a.shape[1:], lambda g, i: (g,) + zeros)

    def whole_spec(a):
        zeros = (0,) * a.ndim
        return pl.BlockSpec(a.shape, lambda g, i: zeros)

    return pl.pallas_call(
        functools.partial(_nsa_kernel, n_slc=n_slc),
        grid=(N_KV_GROUPS, n_qb),
        in_specs=[pl.BlockSpec((Q_BLOCK, gw), lambda g, i: (i, g)),
                  pl.BlockSpec((LANE, Q_BLOCK), lambda g, i: (g, i)),
                  group_spec(k_cmp), group_spec(v_cmp), group_spec(v_cmp_t),
                  whole_spec(ovl), whole_spec(ovl_t),
                  group_spec(ks), group_spec(vs_far), group_spec(vs_near),
                  group_spec(kw), group_spec(vw_near),
                  pl.BlockSpec((HEADS_PER_GROUP, N_PATTERNS, LANE, Q_BLOCK), lambda g, i: (g, 0, 0, 0))],
        out_specs=pl.BlockSpec((Q_BLOCK, gw), lambda g, i: (i, g)),
        out_shape=jax.ShapeDtypeStruct((s, N_KV_GROUPS * gw), BF16),
        scratch_shapes=[pltpu.VMEM((1, lanes), F32),
                        pltpu.VMEM((HEAD_DIM + SUM_ROWS, lanes), F32),
                        pltpu.VMEM((KV_TILE, lanes), F32),
                        pltpu.VMEM((KV_TILE, lanes), F32)],
        compiler_params=_cparams(("arbitrary", "arbitrary")),
    )(q, gates_t, k_cmp, v_cmp, v_cmp_t, ovl, ovl_t, ks, vs_far, vs_near, kw, vw_near, bias_tiles)


def _cmp_mask_columns(n_half):
    assert n_half // 8 < LANE - 1
    out = np.zeros((CMP_PAD + n_half, LANE), np.float32)
    out[np.arange(CMP_PAD), LANE - 1] = NEG
    k = np.arange(n_half)
    out[CMP_PAD + k, k // 8] = NEG
    return out


def _slc_mask_columns(s):
    n_slc = s // SLC_BLOCK
    assert n_slc - 1 > (WINDOW + Q_BLOCK) // SLC_BLOCK
    out = np.zeros((WINDOW + s, n_slc), np.float32)
    out[np.arange(WINDOW), n_slc - 1] = NEG
    pos = np.arange(s)
    out[WINDOW + pos, pos // SLC_BLOCK] = NEG
    return out


def _pad_mask_columns(s):
    out = np.zeros((WINDOW + s, LANE), np.float32)
    out[np.arange(WINDOW), LANE - 1] = NEG
    return out


def _overlap_padded(n_half, n_slc):
    n_cmp = n_half - 1
    c0 = np.arange(n_cmp) * CMP_STRIDE
    s0 = np.arange(n_slc) * SLC_BLOCK
    lo = np.maximum(c0[:, None], s0[None, :])
    hi = np.minimum(c0[:, None] + CMP_BLOCK, s0[None, :] + SLC_BLOCK)
    ovl = np.maximum(hi - lo, 0).astype(np.float32) / CMP_BLOCK
    out = np.zeros((CMP_PAD + n_half, n_slc), np.float32)
    out[CMP_PAD:CMP_PAD + n_cmp] = ovl
    return out


def _row_tile(s):
    return min(1024, s)


def _ffn_ple(x, p, layer, b, norm_ffn, w_in, w_out, norm_ple, ple_w, ple_gate):
    s = x.shape[0]
    tm = _row_tile(s)
    tn = 512
    nj = FFN_DIM // tn
    act = _norm_matmul(x, norm_ffn, (w_in, layer), (0, nj), tm=tm, tn=tn, nj=nj,
                       epilogue=_ep_swiglu, out_dtype=BF16)
    x = _matmul_res(act, (w_out, layer), x, tm=tm, tn=256)
    aux = (p, ple_w)
    aux_specs = (pl.BlockSpec((None, None, tm, PLE_DIM), lambda i, j: (layer, b, i, 0)),
                 pl.BlockSpec((None, PLE_DIM, tn), lambda i, j: (layer, 0, j)))
    return _norm_matmul(x, norm_ple, (ple_gate, layer), (0,), tm=tm, tn=tn, nj=D_MODEL // tn,
                        epilogue=_ep_ple, out_dtype=F32, aux=aux, aux_specs=aux_specs)


def _gmlp_layer(x, layer, norm_mix, w_in, norm_v, w_s, b_s, w_out):
    return _gmlp(x, norm_mix, w_in, norm_v, w_s, jnp.transpose(b_s), w_out, layer, tm=_row_tile(x.shape[0]))


def _shared_kv(x, kv_norm, kv_w, k_norm, cmp_pe_k, cmp_pe_v, cmp_wk1, cmp_wk2, cmp_wv1, cmp_wv2):
    s = x.shape[0]
    tm = _row_tile(s)
    gw = N_KV_GROUPS * HEAD_DIM
    kvc = _norm_matmul(x, kv_norm, kv_w, (0,), tm=tm, tn=2 * gw, nj=1,
                       epilogue=_ep_identity, out_dtype=F32)
    gains = jnp.stack([jnp.tile(k_norm[1], N_KV_GROUPS), jnp.ones((gw,), F32),
                       jnp.tile(k_norm[2], N_KV_GROUPS), jnp.ones((gw,), F32)])
    gains = jnp.broadcast_to(gains[:, None, :], (4, 8, gw))
    kvr = _norm_matmul(x, kv_norm, kv_w, (2,), tm=tm, tn=gw, nj=4,
                       epilogue=_ep_kv, out_dtype=BF16, aux=(gains,),
                       aux_specs=(pl.BlockSpec((None, 8, gw), lambda i, j: (j, 0, 0)),))
    n_half = s // CMP_STRIDE
    halves = kvc.reshape(n_half, CMP_STRIDE, 2 * N_KV_GROUPS, HEAD_DIM)
    halves = jnp.transpose(halves, (2, 0, 1, 3)).reshape(2 * N_KV_GROUPS, n_half, CMP_STRIDE * HEAD_DIM)
    pe = jnp.stack([cmp_pe_k, cmp_pe_v]).reshape(2, 2, CMP_STRIDE * HEAD_DIM)
    kv_cmp = _compress(halves, pe, jnp.stack([cmp_wk1, cmp_wv1]), jnp.stack([cmp_wk2, cmp_wv2]), k_norm[0])
    kv_cmp = jnp.pad(kv_cmp, ((0, 0), (CMP_PAD, 0), (0, 0)))
    cmp_cols = jnp.broadcast_to(jnp.asarray(_cmp_mask_columns(n_half)), (N_KV_GROUPS, CMP_PAD + n_half, LANE))
    k_cmp = jnp.concatenate([kv_cmp[:N_KV_GROUPS], cmp_cols], axis=2)
    v_cmp = kv_cmp[N_KV_GROUPS:]
    v_cmp_t = jnp.transpose(v_cmp, (0, 2, 1)).astype(BF16)
    s_pad = WINDOW + s
    kvr = jnp.pad(kvr, ((WINDOW, 0), (0, 0))).reshape(s_pad, 4, N_KV_GROUPS, HEAD_DIM)
    kvr = jnp.transpose(kvr, (1, 2, 0, 3))
    slc_cols = jnp.asarray(_slc_mask_columns(s)).astype(BF16)
    pad_cols = jnp.asarray(_pad_mask_columns(s)).astype(BF16)
    ks = jnp.concatenate([kvr[0], jnp.broadcast_to(slc_cols, (N_KV_GROUPS,) + slc_cols.shape)], axis=2)
    kw = jnp.concatenate([kvr[2], jnp.broadcast_to(pad_cols, (N_KV_GROUPS,) + pad_cols.shape)], axis=2)

    def key_tiles_t(v, tile):
        v_t = jnp.transpose(v.reshape(N_KV_GROUPS, -1, tile, HEAD_DIM), (0, 1, 3, 2))
        return jnp.concatenate([v_t, jnp.ones(v_t.shape[:2] + (SUM_ROWS, tile), v_t.dtype)], axis=2)

    vs_far = key_tiles_t(kvr[1][:, WINDOW:], KV_TILE)
    return k_cmp, v_cmp, v_cmp_t, ks, vs_far, key_tiles_t(kvr[1], LANE), kw, key_tiles_t(kvr[3], LANE)


def _nsa_layer(x, layer, norm_mix, w_in, q_norm, w_out, kvs):
    s = x.shape[0]
    tm = _row_tile(s)
    nq = N_HEADS * HEAD_DIM
    scale = HEAD_DIM ** -0.5 * LOG2E
    q_gain = jnp.tile(q_norm * scale, 4).reshape(1, 4 * HEAD_DIM)
    q = _norm_matmul(x, norm_mix, (w_in, layer), (0,), tm=tm, tn=512, nj=nq // 512,
                     epilogue=_ep_q, out_dtype=BF16, aux=(q_gain,),
                     aux_specs=(pl.BlockSpec((1, 4 * HEAD_DIM), lambda i, j: (0, 0)),))
    w_gate = w_in[layer, :, nq:].reshape(D_MODEL, N_KV_GROUPS, HEADS_PER_GROUP * N_BRANCH)
    w_gate = jnp.pad(w_gate, ((0, 0), (0, 0), (0, LANE - HEADS_PER_GROUP * N_BRANCH)))
    w_gate = w_gate.reshape(D_MODEL, N_KV_GROUPS * LANE)
    gates = _norm_matmul(x, norm_mix, w_gate, (0,), tm=tm, tn=N_KV_GROUPS * LANE, nj=1,
                         epilogue=_ep_sigmoid, out_dtype=F32)
    o = _nsa_attention(q, jnp.transpose(gates), *kvs)
    return _matmul_res(o, (w_out, layer), x, tm=tm, tn=512)


def kernel(x, p, norm_mix, norm_ffn, norm_ple, a_w_in, a_norm_v, a_w_s, a_b_s, a_w_out, kv_norm, kv_w, k_norm, cmp_pe_k, cmp_pe_v, cmp_wk1, cmp_wk2, cmp_wv1, cmp_wv2, b_w_in, b_q_norm, b_w_out, rel_bias, ffn_w_in, ffn_w_out, ple_w, ple_gate):
    batch, s, d = x.shape
    depth = norm_mix.shape[0]
    n_a = a_w_in.shape[0]
    a_w_out, b_w_in, b_w_out, kv_w, ple_gate = (
        w.astype(BF16) for w in (a_w_out, b_w_in, b_w_out, kv_w, ple_gate))
    outs = []
    for b in range(batch):
        xb = x.reshape(s, d) if batch == 1 else x[b]
        kvs = None
        for i in range(depth):
            if i < n_a:
                xb = _gmlp_layer(xb, i, norm_mix[i], a_w_in, a_norm_v[i], a_w_s, a_b_s[i], a_w_out)
            else:
                j = i - n_a
                xb = _nsa_layer(xb, j, norm_mix[i], b_w_in, b_q_norm[j], b_w_out, kvs)
            xb = _ffn_ple(xb, p, i, b, norm_ffn[i], ffn_w_in, ffn_w_out,
                          norm_ple[i], ple_w, ple_gate)
            if i == n_a - 1:
                k_cmp, v_cmp, v_cmp_t, ks, vs_far, vs_near, kw, vw_near = _shared_kv(
                    xb, kv_norm, kv_w, k_norm, cmp_pe_k, cmp_pe_v, cmp_wk1, cmp_wk2, cmp_wv1, cmp_wv2)
                ovl = _overlap_padded(s // CMP_STRIDE, s // SLC_BLOCK)
                kvs = (k_cmp, v_cmp, v_cmp_t, jnp.asarray(ovl), jnp.asarray(ovl.T).astype(BF16),
                       ks, vs_far, vs_near, kw, vw_near, _bias_tiles(rel_bias))
        outs.append(xb)
    return outs[0].reshape(1, s, d) if batch == 1 else jnp.stack(outs)
```

```python
import functools
import math

import numpy as np
import jax
import jax.numpy as jnp
from jax import lax
from jax.experimental import pallas as pl
from jax.experimental.pallas import tpu as pltpu

F32 = jnp.float32
BF16 = jnp.bfloat16

D_MODEL = 2048
PLE_DIM = 256
FFN_DIM = 5632
GMLP_CHUNK = 128
GMLP_GROUPS = 16
HEAD_DIM = 128
N_HEADS = 16
N_KV_GROUPS = 2
HEADS_PER_GROUP = 8
N_BRANCH = 3
CMP_BLOCK = 32
CMP_STRIDE = 16
SLC_BLOCK = 64
N_SELECT = 16
WINDOW = 512
Q_BLOCK = 128
N_BUCKETS = 32
MAX_DISTANCE = 128
EPS = 1e-6
NEG = -1e30
BIG = 1e30
LOG2E = math.log2(math.e)

LANE = 128
KV_TILE = 512
NEAR = WINDOW + Q_BLOCK
CMP_PAD = 128
SUM_ROWS = 16
VMEM_LIMIT = 56 * 1024 * 1024


def _cparams(sem):
    return pltpu.CompilerParams(dimension_semantics=sem, vmem_limit_bytes=VMEM_LIMIT)


def _dot(a, b):
    return jnp.dot(a, b, preferred_element_type=F32)


def _dot_nt(a, b):
    return lax.dot_general(a, b, (((1,), (1,)), ((), ())), preferred_element_type=F32)


def _rms_rows(x, g):
    ms = jnp.mean(x * x, axis=-1, keepdims=True)
    return x * lax.rsqrt(ms + EPS) * g


def _group_rms(acc, gain):
    outs = []
    for c in range(acc.shape[1] // LANE):
        a = acc[:, c * LANE:(c + 1) * LANE]
        outs.append(_rms_rows(a, gain[:, c * LANE:(c + 1) * LANE]))
    return outs[0] if len(outs) == 1 else jnp.concatenate(outs, axis=1)


def _norm_matmul_kernel(x_ref, g_ref, *refs, n_w, n_aux, epilogue, out_dtype):
    w_refs = refs[:n_w]
    aux_refs = refs[n_w:n_w + n_aux]
    o_ref = refs[n_w + n_aux]
    h_scr = refs[n_w + n_aux + 1]

    @pl.when(pl.program_id(1) == 0)
    def _():
        h_scr[...] = _rms_rows(x_ref[...], g_ref[...]).astype(BF16)

    h = h_scr[...]
    accs = [_dot(h, w_ref[...].astype(BF16)) for w_ref in w_refs]
    o_ref[...] = epilogue(accs, aux_refs + (x_ref,)).astype(out_dtype)


def _w_spec(w, k, tn, col_of):
    if isinstance(w, tuple):
        layer = w[1]
        return pl.BlockSpec((None, k, tn), lambda i, j: (layer, 0, col_of(j)))
    return pl.BlockSpec((k, tn), lambda i, j: (0, col_of(j)))


def _w_array(w):
    return w[0] if isinstance(w, tuple) else w


def _norm_matmul(x, gain, w, col_offsets, *, tm, tn, nj, epilogue, out_dtype,
                 aux=(), aux_specs=()):
    s, k = x.shape
    in_specs = [pl.BlockSpec((tm, k), lambda i, j: (i, 0)),
                pl.BlockSpec((1, k), lambda i, j: (0, 0))]
    for c0 in col_offsets:
        in_specs.append(_w_spec(w, k, tn, lambda j, c0=c0: c0 + j))
    in_specs += list(aux_specs)
    kern = functools.partial(_norm_matmul_kernel, n_w=len(col_offsets), n_aux=len(aux),
                             epilogue=epilogue, out_dtype=out_dtype)
    return pl.pallas_call(
        kern,
        grid=(s // tm, nj),
        in_specs=in_specs,
        out_specs=pl.BlockSpec((tm, tn), lambda i, j: (i, j)),
        out_shape=jax.ShapeDtypeStruct((s, nj * tn), out_dtype),
        scratch_shapes=[pltpu.VMEM((tm, k), BF16)],
        compiler_params=_cparams(("arbitrary", "arbitrary")),
    )(x, gain.reshape(1, k), *([_w_array(w)] * len(col_offsets)), *aux)


def _norm_matmul_pair_kernel(x_ref, g_ref, wa_ref, wb_ref, aux_ref, oa_ref, ob_ref, h_scr, *, nja, ep_a, ep_b):
    j = pl.program_id(1)

    @pl.when(j == 0)
    def _():
        h_scr[...] = _rms_rows(x_ref[...], g_ref[...]).astype(BF16)

    @pl.when(j < nja)
    def _():
        oa_ref[...] = ep_a(_dot(h_scr[...], wa_ref[...].astype(BF16)), aux_ref, j).astype(oa_ref.dtype)

    @pl.when(j >= nja)
    def _():
        ob_ref[...] = ep_b(_dot(h_scr[...], wb_ref[...].astype(BF16)), aux_ref, j - nja).astype(ob_ref.dtype)


def _norm_matmul_pair(x, gain, wa, wb, aux, aux_spec, *, tm, tna, nja, col_a, tnb, njb, col_b, ep_a, ep_b,
                      dtype_a, dtype_b, transpose_b=False):
    s, k = x.shape
    if transpose_b:
        b_spec = pl.BlockSpec((tnb, tm), lambda i, j: (jnp.maximum(j - nja, 0), i))
        b_shape = (njb * tnb, s)
    else:
        b_spec = pl.BlockSpec((tm, tnb), lambda i, j: (i, jnp.maximum(j - nja, 0)))
        b_shape = (s, njb * tnb)
    kern = functools.partial(_norm_matmul_pair_kernel, nja=nja, ep_a=ep_a, ep_b=ep_b)
    return pl.pallas_call(
        kern,
        grid=(s // tm, nja + njb),
        in_specs=[pl.BlockSpec((tm, k), lambda i, j: (i, 0)),
                  pl.BlockSpec((1, k), lambda i, j: (0, 0)),
                  _w_spec(wa, k, tna, lambda j: col_a + jnp.minimum(j, nja - 1)),
                  _w_spec(wb, k, tnb, lambda j: col_b + jnp.maximum(j - nja, 0)),
                  aux_spec],
        out_specs=[pl.BlockSpec((tm, tna), lambda i, j: (i, jnp.minimum(j, nja - 1))), b_spec],
        out_shape=[jax.ShapeDtypeStruct((s, nja * tna), dtype_a), jax.ShapeDtypeStruct(b_shape, dtype_b)],
        scratch_shapes=[pltpu.VMEM((tm, k), BF16)],
        compiler_params=_cparams(("arbitrary", "arbitrary")),
    )(x, gain.reshape(1, k), _w_array(wa), _w_array(wb), aux)


def _ep_swiglu(accs, aux):
    g, u = accs
    return g * jax.nn.sigmoid(g) * u


def _ep_ple(accs, aux):
    p_ref, wp_ref, x_ref = aux
    tn = accs[0].shape[1]
    col0 = pl.multiple_of(pl.program_id(1) * tn, tn)
    pp = _dot(p_ref[...].astype(BF16), wp_ref[...].astype(BF16))
    return x_ref[:, pl.ds(col0, tn)] + pp * jax.nn.sigmoid(accs[0])


def _ep_identity(acc, aux_ref, j):
    return acc


def _ep_gates_t(acc, aux_ref, j):
    return jax.nn.sigmoid(acc).T


def _ep_q(acc, gain_ref, j):
    return _group_rms(acc, gain_ref[...])


def _ep_kv(acc, gain_ref, j):
    normed = _group_rms(acc, gain_ref[0:1, :])
    return jnp.where((j % 2) == 0, normed, acc)


def _matmul_res_kernel(a_ref, w_ref, x_ref, o_ref):
    o_ref[...] = x_ref[...] + _dot(a_ref[...], w_ref[...].astype(BF16))


def _matmul_res(a, w, resid, *, tm, tn):
    s, k = a.shape
    n = _w_array(w).shape[-1]
    return pl.pallas_call(
        _matmul_res_kernel,
        grid=(s // tm, n // tn),
        in_specs=[pl.BlockSpec((tm, k), lambda i, j: (i, 0)),
                  _w_spec(w, k, tn, lambda j: j),
                  pl.BlockSpec((tm, tn), lambda i, j: (i, j))],
        out_specs=pl.BlockSpec((tm, tn), lambda i, j: (i, j)),
        out_shape=jax.ShapeDtypeStruct((s, n), F32),
        compiler_params=_cparams(("arbitrary", "arbitrary")),
    )(a, _w_array(w), resid)


GMLP_TN = 512


def _gmlp_kernel(x_ref, g_ref, win_ref, nv_ref, ws_ref, bs_ref, wo_ref, o_ref, h_scr, z_scr, y_scr, *, tm):
    width = GMLP_GROUPS * LANE
    n_in = 2 * width // GMLP_TN
    per_tile = GMLP_TN // LANE
    n_chunk = tm // GMLP_CHUNK
    j = pl.program_id(1)

    @pl.when(j == 0)
    def _():
        h_scr[...] = _rms_rows(x_ref[...], g_ref[...]).astype(BF16)

    @pl.when(j < n_in)
    def _():
        z_scr[j] = jax.nn.gelu(_dot(h_scr[...], win_ref[...].astype(BF16)))

    @pl.when(j == n_in)
    def _():
        v_tiles = range(n_in // 2, n_in)
        ms = sum(jnp.sum(z_scr[t] * z_scr[t], axis=-1, keepdims=True) for t in v_tiles) / width
        inv = lax.rsqrt(ms + EPS)
        row = lax.broadcasted_iota(jnp.int32, (GMLP_CHUNK, GMLP_CHUNK), 0)
        col = lax.broadcasted_iota(jnp.int32, (GMLP_CHUNK, GMLP_CHUNK), 1)
        causal = col <= row
        for g in range(GMLP_GROUPS):
            t, ls = g // per_tile, slice((g % per_tile) * LANE, (g % per_tile + 1) * LANE)
            cs = slice(g * LANE, (g + 1) * LANE)
            vn = (z_scr[n_in // 2 + t, :, ls] * inv * nv_ref[:, cs]).astype(BF16)
            ws = jnp.where(causal, ws_ref[g], 0.0).astype(BF16)
            vg = jnp.concatenate(
                [vn[c * GMLP_CHUNK:(c + 1) * GMLP_CHUNK, :] for c in range(n_chunk)], axis=1)
            sv = _dot(ws, vg) + bs_ref[:, g:g + 1]
            for c in range(n_chunk):
                rs = slice(c * GMLP_CHUNK, (c + 1) * GMLP_CHUNK)
                y_scr[rs, cs] = (z_scr[t, rs, ls] * sv[:, c * LANE:(c + 1) * LANE]).astype(BF16)

    @pl.when(j >= n_in)
    def _():
        col0 = pl.multiple_of((j - n_in) * GMLP_TN, GMLP_TN)
        o_ref[...] = x_ref[:, pl.ds(col0, GMLP_TN)] + _dot(y_scr[...], wo_ref[...].astype(BF16))


def _gmlp(x, gain, w_in, norm_v, w_s, b_s_t, w_out, layer, *, tm):
    s, d = x.shape
    width = GMLP_GROUPS * LANE
    n_in = 2 * width // GMLP_TN
    n_out = d // GMLP_TN
    return pl.pallas_call(
        functools.partial(_gmlp_kernel, tm=tm),
        grid=(s // tm, n_in + n_out),
        in_specs=[pl.BlockSpec((tm, d), lambda i, j: (i, 0), pipeline_mode=pl.Buffered(1)),
                  pl.BlockSpec((1, d), lambda i, j: (0, 0)),
                  pl.BlockSpec((None, d, GMLP_TN), lambda i, j: (layer, 0, jnp.minimum(j, n_in - 1))),
                  pl.BlockSpec((1, width), lambda i, j: (0, 0)),
                  pl.BlockSpec((None, GMLP_GROUPS, GMLP_CHUNK, GMLP_CHUNK), lambda i, j: (layer, 0, 0, 0)),
                  pl.BlockSpec((GMLP_CHUNK, GMLP_GROUPS), lambda i, j: (0, 0)),
                  pl.BlockSpec((None, width, GMLP_TN), lambda i, j: (layer, 0, jnp.maximum(j - n_in, 0)))],
        out_specs=pl.BlockSpec((tm, GMLP_TN), lambda i, j: (i, jnp.maximum(j - n_in, 0))),
        out_shape=jax.ShapeDtypeStruct((s, d), F32),
        scratch_shapes=[pltpu.VMEM((tm, d), BF16),
                        pltpu.VMEM((n_in, tm, GMLP_TN), F32),
                        pltpu.VMEM((tm, width), BF16)],
        compiler_params=_cparams(("arbitrary", "arbitrary")),
    )(x, gain.reshape(1, d), w_in, norm_v.reshape(1, width), w_s, b_s_t, w_out)


def _compress_kernel(h_ref, pe_ref, w1_ref, w2_ref, kn_ref, o_ref, *, n_half):
    half_w = CMP_STRIDE * HEAD_DIM
    hh = h_ref[...]
    pe = pe_ref[...]
    a = _dot((hh + pe[0:1, :]).astype(BF16), w1_ref[0:half_w, :].astype(BF16))
    b = _dot((hh + pe[1:2, :]).astype(BF16), w1_ref[half_w:2 * half_w, :].astype(BF16))
    pre = a + pltpu.roll(b, n_half - 1, 0)
    out = _dot(jax.nn.gelu(pre).astype(BF16), w2_ref[...].astype(BF16))
    is_key = pl.program_id(0) < N_KV_GROUPS
    out = jnp.where(is_key, _rms_rows(out, kn_ref[...]), out)
    row = lax.broadcasted_iota(jnp.int32, out.shape, 0)
    o_ref[...] = jnp.where(row < n_half - 1, out, 0.0)


def _compress(halves, pe, w1, w2, k_norm0):
    n_half = halves.shape[1]
    half_w = CMP_STRIDE * HEAD_DIM
    hid = w1.shape[2]
    return pl.pallas_call(
        functools.partial(_compress_kernel, n_half=n_half),
        grid=(2 * N_KV_GROUPS,),
        in_specs=[pl.BlockSpec((None, n_half, half_w), lambda n: (n, 0, 0)),
                  pl.BlockSpec((None, 2, half_w), lambda n: (n // N_KV_GROUPS, 0, 0)),
                  pl.BlockSpec((None, 2 * half_w, hid), lambda n: (n // N_KV_GROUPS, 0, 0)),
                  pl.BlockSpec((None, hid, HEAD_DIM), lambda n: (n // N_KV_GROUPS, 0, 0)),
                  pl.BlockSpec((1, HEAD_DIM), lambda n: (0, 0))],
        out_specs=pl.BlockSpec((None, n_half, HEAD_DIM), lambda n: (n, 0, 0)),
        out_shape=jax.ShapeDtypeStruct((2 * N_KV_GROUPS, n_half, HEAD_DIM), F32),
        compiler_params=_cparams(("arbitrary",)),
    )(halves, pe, w1, w2, k_norm0.reshape(1, HEAD_DIM))


def _t5_bucket_np(dist):
    n = np.maximum(dist, 0)
    max_exact = N_BUCKETS // 2
    nf = np.maximum(n, 1).astype(np.float32)
    large = max_exact + (np.log(nf / np.float32(max_exact)) / np.float32(math.log(MAX_DISTANCE / max_exact))
                         * np.float32(N_BUCKETS - max_exact)).astype(np.int32)
    large = np.minimum(large, N_BUCKETS - 1)
    return np.where(n < max_exact, n, large).astype(np.int32)


N_PATTERNS = 4


def _bucket_patterns():
    i = np.arange(Q_BLOCK)[:, None]
    c = np.arange(LANE)[None, :]
    d0 = i - c
    d1 = i - c + Q_BLOCK
    dc = i - CMP_STRIDE * (c - (LANE - 8)) - (CMP_BLOCK - 1)
    pats = [np.where(d >= 0, _t5_bucket_np(d), -1) for d in (d0, d1, dc)]
    pats.append(np.where(i < c, N_BUCKETS - 1, -1))
    return np.stack([p.T for p in pats]).astype(np.int32)


def _bias_tiles_kernel(tab_ref, pat_ref, o_ref):
    h = pl.program_id(0)
    pat = pat_ref[...]
    far = tab_ref[N_BUCKETS - 1, h]
    acc = jnp.full(pat.shape, NEG, F32)
    for b in range(N_BUCKETS):
        acc = jnp.where(pat == b, (tab_ref[b, h] - far) * LOG2E, acc)
    o_ref[...] = acc


def _bias_tiles(rel_bias):
    pats = jnp.asarray(_bucket_patterns())
    return pl.pallas_call(
        _bias_tiles_kernel,
        grid=(N_HEADS,),
        in_specs=[pl.BlockSpec(memory_space=pltpu.SMEM),
                  pl.BlockSpec((N_PATTERNS, Q_BLOCK, LANE), lambda h: (0, 0, 0))],
        out_specs=pl.BlockSpec((None, N_PATTERNS, Q_BLOCK, LANE), lambda h: (h, 0, 0, 0)),
        out_shape=jax.ShapeDtypeStruct((N_HEADS, N_PATTERNS, Q_BLOCK, LANE), F32),
        compiler_params=_cparams(("arbitrary",)),
    )(rel_bias, pats)


def _with_features(qs, feat):
    reps = qs.shape[0] // feat.shape[0]
    return jnp.concatenate([qs, jnp.concatenate([feat] * reps, axis=0)], axis=1)


def _softmax_cols(blocks):
    m = blocks[0].max(axis=0, keepdims=True)
    for b in blocks[1:]:
        m = jnp.maximum(m, b.max(axis=0, keepdims=True))
    es = [jnp.exp2(b - m) for b in blocks]
    den = es[0].sum(axis=0, keepdims=True)
    for e in es[1:]:
        den = den + e.sum(axis=0, keepdims=True)
    return es, m, den


def _nsa_kernel(q_ref, gt_ref, kc_ref, vc_ref, vct_ref, ovl_ref, ovlt_ref, ks_ref, vsf_ref, vsn_ref,
                kw_ref, vwn_ref, tb_ref, o_ref, m_scr, acc_scr, sa_scr, sb_scr, cmp_o_scr, cmp_pf_scr, cmp_pn_scr,
                *, n_slc):
    qb = pl.program_id(1)
    s = qb * Q_BLOCK
    hpg = HEADS_PER_GROUP

    q_all = q_ref[...]
    qs = jnp.concatenate([q_all[:, h * LANE:(h + 1) * LANE] for h in range(hpg)], axis=0)
    cols = [slice(h * Q_BLOCK, (h + 1) * Q_BLOCK) for h in range(hpg)]
    lane_f = lax.broadcasted_iota(jnp.int32, (Q_BLOCK, LANE), 1)
    pad_feat = jnp.where(lane_f == LANE - 1, 1.0, 0.0).astype(BF16)
    q_pad = _with_features(qs, pad_feat)

    n_sub = NEAR // LANE
    row0 = pl.multiple_of(s, Q_BLOCK)

    n_pair = hpg // 2
    pair_rows = [slice(pr * 2 * Q_BLOCK, (pr + 1) * 2 * Q_BLOCK) for pr in range(n_pair)]
    half = [slice(0, Q_BLOCK), slice(Q_BLOCK, 2 * Q_BLOCK)]

    def run_stages(stages):
        pending = stages[0][0]()
        for i, (_, consume) in enumerate(stages):
            cur = pending
            if i + 1 < len(stages):
                pending = stages[i + 1][0]()
            consume(cur)

    def near_stages(k_ref, q_aug, vn_ref, oldest_pat, res):
        k_aug = k_ref[pl.ds(row0, NEAR), :]
        v_blocks = vn_ref[pl.ds(qb, n_sub)]
        v_t = jnp.concatenate([v_blocks[u] for u in range(n_sub)], axis=1)

        def stage(pr):
            def issue():
                return _dot_nt(k_aug, q_aug[pair_rows[pr], :])

            def consume(logits):
                ps = []
                for hh in range(2):
                    h = 2 * pr + hh
                    blocks = [logits[u * LANE:(u + 1) * LANE, half[hh]] for u in range(n_sub)]
                    blocks[n_sub - 1] = blocks[n_sub - 1] + tb_ref[h, 0]
                    blocks[n_sub - 2] = blocks[n_sub - 2] + tb_ref[h, 1]
                    if oldest_pat is not None:
                        blocks[0] = blocks[0] + tb_ref[h, oldest_pat]
                    m = blocks[0].max(axis=0, keepdims=True)
                    for b in blocks[1:]:
                        m = jnp.maximum(m, b.max(axis=0, keepdims=True))
                    ps.append(jnp.concatenate([jnp.exp2((b - m).astype(BF16)) for b in blocks], axis=0))
                    res["m"].append(m)
                res["o"].append(_dot(v_t, jnp.concatenate(ps, axis=1)))
            return issue, consume
        return [stage(pr) for pr in range(n_pair)]

    near0 = pl.multiple_of(qb * 8 + 8, 8)
    kn = kc_ref[pl.ds(near0, LANE), :].astype(BF16)
    vn_t = vc_ref[pl.ds(near0, LANE), :].T.astype(BF16)
    ovl_n_t = ovl_ref[pl.ds(near0, LANE), :].T.astype(BF16)
    far_feat = jnp.where(lane_f > qb - 16, 1.0, 0.0).astype(BF16)
    q_cmp = _with_features(qs, far_feat)
    t_row = s + lax.broadcasted_iota(jnp.int32, (1, Q_BLOCK), 1)
    row_ok = t_row >= CMP_BLOCK - 1
    n_half = kc_ref.shape[0] - CMP_PAD

    def cmp_branch(n_keys):
        kc = kc_ref[CMP_PAD:CMP_PAD + n_keys, :].astype(BF16)
        vc_t = vct_ref[:, CMP_PAD:CMP_PAD + n_keys]
        res = {"o": [], "pf": jnp.zeros((n_keys, Q_BLOCK), F32), "pn": jnp.zeros((LANE, Q_BLOCK), F32)}

        def stage(pr):
            def issue():
                return _dot_nt(kc, q_cmp[pair_rows[pr], :]), _dot_nt(kn, q_pad[pair_rows[pr], :])

            def consume(logits):
                sf, sn = logits
                pfs, pns = [], []
                for hh in range(2):
                    (ef, en), _, den = _softmax_cols([sf[:, half[hh]], sn[:, half[hh]] + tb_ref[2 * pr + hh, 2]])
                    inv = jnp.where(row_ok, 1.0 / den, 0.0)
                    pf = ef * inv
                    pn = en * inv
                    res["pf"] = res["pf"] + pf
                    res["pn"] = res["pn"] + pn
                    pfs.append(pf.astype(BF16))
                    pns.append(pn.astype(BF16))
                res["o"].append(_dot(vc_t, jnp.concatenate(pfs, axis=1)) + _dot(vn_t, jnp.concatenate(pns, axis=1)))
            return issue, consume

        run_stages([stage(pr) for pr in range(n_pair)])
        cmp_o_scr[...] = jnp.concatenate(res["o"], axis=1)
        cmp_pf_scr[0:n_keys, :] = res["pf"]
        if n_keys < n_half:
            cmp_pf_scr[n_keys:, :] = jnp.zeros((n_half - n_keys, Q_BLOCK), F32)
        cmp_pn_scr[...] = res["pn"]

    sizes = sorted({min(n_half, c) for c in (LANE, 2 * LANE)} | {n_half})
    lo = 0
    for idx, n_keys in enumerate(sizes):
        last = idx + 1 == len(sizes)
        hi = (n_keys + LANE - 8) // 8 + 1
        pl.when((qb >= lo) if last else ((qb >= lo) & (qb < hi)))(functools.partial(cmp_branch, n_keys))
        lo = hi

    blk = lax.broadcasted_iota(jnp.int32, (n_slc, Q_BLOCK), 0)
    blk_f = blk.astype(F32)
    cur = (s + lax.broadcasted_iota(jnp.int32, (n_slc, Q_BLOCK), 1)) >> 6
    forced = (blk == 0) | (blk == cur) | (blk == cur - 1)
    valid = blk <= cur
    topk = {}

    def importance_stage():
        def consume(_):
            psum_f, psum_n = cmp_pf_scr[...], cmp_pn_scr[...]
            ovl_t = ovlt_ref[:, CMP_PAD:]
            pf_hi = psum_f.astype(BF16)
            pf_lo = (psum_f - pf_hi.astype(F32)).astype(BF16)
            pn_hi = psum_n.astype(BF16)
            pn_lo = (psum_n - pn_hi.astype(F32)).astype(BF16)
            imp_t = _dot(ovl_t, pf_hi) + _dot(ovl_t, pf_lo) + _dot(ovl_n_t, pn_hi) + _dot(ovl_n_t, pn_lo)
            topk["score"] = jnp.where(valid & ~forced, imp_t, -BIG)
            topk["sel"] = jnp.where(forced, 1.0, 0.0)
        return (lambda: None), consume

    def topk_stage(rounds):
        def consume(_):
            score, sel_t = topk["score"], topk["sel"]
            for _r in range(rounds):
                top = jnp.max(score, axis=0, keepdims=True)
                first = jnp.min(jnp.where(score == top, blk_f, float(n_slc)), axis=0, keepdims=True)
                pick = blk_f == first
                sel_t = jnp.where(pick, 1.0, sel_t)
                score = jnp.where(pick, -2.0 * BIG, score)
            topk["score"], topk["sel"] = score, sel_t
        return (lambda: None), consume

    win_res = {"o": [], "m": []}
    win_stages = near_stages(kw_ref, q_pad, vwn_ref, 3, win_res)
    free_picks = max(min(N_SELECT, n_slc) - 3, 0)
    rounds = [free_picks // n_pair + (1 if pr < free_picks % n_pair else 0) for pr in range(n_pair)]
    stages = [importance_stage()]
    for pr in range(n_pair):
        stages += [win_stages[pr], topk_stage(rounds[pr])]
    run_stages(stages)
    o_win = jnp.concatenate(win_res["o"], axis=1)
    o_cmp = cmp_o_scr[...]
    sel_t = topk["sel"]
    drop_t = jnp.where(valid, 1.0 - sel_t, 1.0)
    near_blk = (s - WINDOW) >> 6
    drop_far_t = jnp.where(blk >= near_blk, 1.0, drop_t)
    q_near = _with_features(qs, drop_t.T.astype(BF16))
    q_far = _with_features(qs, drop_far_t.T.astype(BF16))

    def far_logits(t):
        r0 = pl.multiple_of(WINDOW + KV_TILE * t, KV_TILE)
        return _dot_nt(ks_ref[pl.ds(r0, KV_TILE), :], q_far)

    sa_scr[...] = far_logits(0)
    slc_res = {"o": [], "m": []}
    run_stages(near_stages(ks_ref, q_near, vsn_ref, None, slc_res))
    m_scr[...] = jnp.concatenate(slc_res["m"], axis=1)
    acc_scr[...] = jnp.concatenate(slc_res["o"], axis=1)

    gates = gt_ref[...]
    g_c, g_s, g_w = [jnp.concatenate([gates[3 * h + br:3 * h + br + 1, :] for h in range(hpg)], axis=1)
                     for br in range(N_BRANCH)]
    out_cw = g_c * o_cmp + (g_w / o_win[HEAD_DIM:HEAD_DIM + 1, :]) * o_win[:HEAD_DIM, :]

    n_far = (jnp.maximum(s - WINDOW, 0) + KV_TILE - 1) // KV_TILE

    n_tiles = vsf_ref.shape[0]

    def far_probs(st_ref):
        m_old = m_scr[...]
        sts = [st_ref[:, cols[h]] for h in range(hpg)]
        m_new = jnp.concatenate(
            [jnp.maximum(m_old[:, cols[h]], sts[h].max(axis=0, keepdims=True)) for h in range(hpg)], axis=1)
        p_t = jnp.concatenate(
            [jnp.exp2((sts[h] - m_new[:, cols[h]]).astype(BF16)) for h in range(hpg)], axis=1)
        m_scr[...] = m_new
        return p_t, jnp.exp2(m_old - m_new)

    def far_accumulate(t, p_t, alpha):
        acc_scr[...] = alpha * acc_scr[...] + _dot(vsf_ref[t], p_t)

    def far_body(i, carry):
        t0 = 2 * i
        sb_scr[...] = far_logits(t0 + 1)
        p_a, alpha_a = far_probs(sa_scr)
        sa_scr[...] = far_logits(jnp.minimum(t0 + 2, n_tiles - 1))
        far_accumulate(t0, p_a, alpha_a)
        p_b, alpha_b = far_probs(sb_scr)
        far_accumulate(t0 + 1, p_b, alpha_b)
        return carry

    lax.fori_loop(0, (n_far + 1) // 2, far_body, 0)

    out_t = out_cw + (g_s / acc_scr[HEAD_DIM:HEAD_DIM + 1, :]) * acc_scr[:HEAD_DIM, :]
    for h in range(hpg):
        o_ref[:, cols[h]] = out_t[:, cols[h]].T.astype(o_ref.dtype)


def _nsa_attention(q, gates_t, k_cmp, v_cmp, v_cmp_t, ovl, ovl_t, ks, vs_far, vs_near, kw, vw_near, bias_tiles):
    s = q.shape[0]
    n_qb = s // Q_BLOCK
    n_slc = s // SLC_BLOCK
    gw = HEADS_PER_GROUP * HEAD_DIM
    lanes = HEADS_PER_GROUP * Q_BLOCK

    def group_spec(a):
        zeros = (0,) * (a.ndim - 1)
        return pl.BlockSpec((None,) + a.shape[1:], lambda g, i: (g,) + zeros)

    def whole_spec(a):
        zeros = (0,) * a.ndim
        return pl.BlockSpec(a.shape, lambda g, i: zeros)

    return pl.pallas_call(
        functools.partial(_nsa_kernel, n_slc=n_slc),
        grid=(N_KV_GROUPS, n_qb),
        in_specs=[pl.BlockSpec((Q_BLOCK, gw), lambda g, i: (i, g)),
                  pl.BlockSpec((LANE, Q_BLOCK), lambda g, i: (g, i)),
                  group_spec(k_cmp), group_spec(v_cmp), group_spec(v_cmp_t),
                  whole_spec(ovl), whole_spec(ovl_t),
                  group_spec(ks), group_spec(vs_far), group_spec(vs_near),
                  group_spec(kw), group_spec(vw_near),
                  pl.BlockSpec((HEADS_PER_GROUP, N_PATTERNS, LANE, Q_BLOCK), lambda g, i: (g, 0, 0, 0))],
        out_specs=pl.BlockSpec((Q_BLOCK, gw), lambda g, i: (i, g)),
        out_shape=jax.ShapeDtypeStruct((s, N_KV_GROUPS * gw), BF16),
        scratch_shapes=[pltpu.VMEM((1, lanes), F32),
                        pltpu.VMEM((HEAD_DIM + SUM_ROWS, lanes), F32),
                        pltpu.VMEM((KV_TILE, lanes), F32),
                        pltpu.VMEM((KV_TILE, lanes), F32),
                        pltpu.VMEM((HEAD_DIM, lanes), F32),
                        pltpu.VMEM((k_cmp.shape[1] - CMP_PAD, Q_BLOCK), F32),
                        pltpu.VMEM((LANE, Q_BLOCK), F32)],
        compiler_params=_cparams(("arbitrary", "arbitrary")),
    )(q, gates_t, k_cmp, v_cmp, v_cmp_t, ovl, ovl_t, ks, vs_far, vs_near, kw, vw_near, bias_tiles)


def _cmp_mask_columns(n_half):
    assert n_half // 8 < LANE - 1
    out = np.zeros((CMP_PAD + n_half, LANE), np.float32)
    out[np.arange(CMP_PAD), LANE - 1] = NEG
    k = np.arange(n_half)
    out[CMP_PAD + k, k // 8] = NEG
    return out


def _slc_mask_columns(s):
    n_slc = s // SLC_BLOCK
    assert n_slc - 1 > (WINDOW + Q_BLOCK) // SLC_BLOCK
    out = np.zeros((WINDOW + s, n_slc), np.float32)
    out[np.arange(WINDOW), n_slc - 1] = NEG
    pos = np.arange(s)
    out[WINDOW + pos, pos // SLC_BLOCK] = NEG
    return out


def _pad_mask_columns(s):
    out = np.zeros((WINDOW + s, LANE), np.float32)
    out[np.arange(WINDOW), LANE - 1] = NEG
    return out


def _overlap_padded(n_half, n_slc):
    n_cmp = n_half - 1
    c0 = np.arange(n_cmp) * CMP_STRIDE
    s0 = np.arange(n_slc) * SLC_BLOCK
    lo = np.maximum(c0[:, None], s0[None, :])
    hi = np.minimum(c0[:, None] + CMP_BLOCK, s0[None, :] + SLC_BLOCK)
    ovl = np.maximum(hi - lo, 0).astype(np.float32) / CMP_BLOCK
    out = np.zeros((CMP_PAD + n_half, n_slc), np.float32)
    out[CMP_PAD:CMP_PAD + n_cmp] = ovl
    return out


def _row_tile(s):
    return min(1024, s)


def _ffn_ple(x, p, layer, b, norm_ffn, w_in, w_out, norm_ple, ple_w, ple_gate):
    s = x.shape[0]
    tm = _row_tile(s)
    tn = 512
    nj = FFN_DIM // tn
    act = _norm_matmul(x, norm_ffn, (w_in, layer), (0, nj), tm=tm, tn=tn, nj=nj,
                       epilogue=_ep_swiglu, out_dtype=BF16)
    x = _matmul_res(act, (w_out, layer), x, tm=tm, tn=256)
    aux = (p, ple_w)
    aux_specs = (pl.BlockSpec((None, None, tm, PLE_DIM), lambda i, j: (layer, b, i, 0)),
                 pl.BlockSpec((None, PLE_DIM, tn), lambda i, j: (layer, 0, j)))
    return _norm_matmul(x, norm_ple, (ple_gate, layer), (0,), tm=tm, tn=tn, nj=D_MODEL // tn,
                        epilogue=_ep_ple, out_dtype=F32, aux=aux, aux_specs=aux_specs)


def _gmlp_layer(x, layer, norm_mix, w_in, norm_v, w_s, b_s, w_out):
    return _gmlp(x, norm_mix, w_in, norm_v, w_s, jnp.transpose(b_s), w_out, layer, tm=_row_tile(x.shape[0]))


def _shared_kv(x, kv_norm, kv_w, k_norm, cmp_pe_k, cmp_pe_v, cmp_wk1, cmp_wk2, cmp_wv1, cmp_wv2):
    s = x.shape[0]
    tm = _row_tile(s)
    gw = N_KV_GROUPS * HEAD_DIM
    gains = jnp.stack([jnp.tile(k_norm[1], N_KV_GROUPS), jnp.ones((gw,), F32),
                       jnp.tile(k_norm[2], N_KV_GROUPS), jnp.ones((gw,), F32)])
    gains = jnp.broadcast_to(gains[:, None, :], (4, 8, gw))
    kvc, kvr = _norm_matmul_pair(
        x, kv_norm, kv_w, kv_w, gains, pl.BlockSpec((None, 8, gw), lambda i, j: (jnp.maximum(j - 1, 0), 0, 0)),
        tm=tm, tna=2 * gw, nja=1, col_a=0, tnb=gw, njb=4, col_b=2,
        ep_a=_ep_identity, ep_b=_ep_kv, dtype_a=F32, dtype_b=BF16)
    n_half = s // CMP_STRIDE
    halves = kvc.reshape(n_half, CMP_STRIDE, 2 * N_KV_GROUPS, HEAD_DIM)
    halves = jnp.transpose(halves, (2, 0, 1, 3)).reshape(2 * N_KV_GROUPS, n_half, CMP_STRIDE * HEAD_DIM)
    pe = jnp.stack([cmp_pe_k, cmp_pe_v]).reshape(2, 2, CMP_STRIDE * HEAD_DIM)
    kv_cmp = _compress(halves, pe, jnp.stack([cmp_wk1, cmp_wv1]), jnp.stack([cmp_wk2, cmp_wv2]), k_norm[0])
    kv_cmp = jnp.pad(kv_cmp, ((0, 0), (CMP_PAD, 0), (0, 0)))
    cmp_cols = jnp.broadcast_to(jnp.asarray(_cmp_mask_columns(n_half)), (N_KV_GROUPS, CMP_PAD + n_half, LANE))
    k_cmp = jnp.concatenate([kv_cmp[:N_KV_GROUPS], cmp_cols], axis=2)
    v_cmp = kv_cmp[N_KV_GROUPS:]
    v_cmp_t = jnp.transpose(v_cmp, (0, 2, 1)).astype(BF16)
    s_pad = WINDOW + s
    kvr = jnp.pad(kvr, ((WINDOW, 0), (0, 0))).reshape(s_pad, 4, N_KV_GROUPS, HEAD_DIM)
    kvr = jnp.transpose(kvr, (1, 2, 0, 3))
    slc_cols = jnp.asarray(_slc_mask_columns(s)).astype(BF16)
    pad_cols = jnp.asarray(_pad_mask_columns(s)).astype(BF16)
    ks = jnp.concatenate([kvr[0], jnp.broadcast_to(slc_cols, (N_KV_GROUPS,) + slc_cols.shape)], axis=2)
    kw = jnp.concatenate([kvr[2], jnp.broadcast_to(pad_cols, (N_KV_GROUPS,) + pad_cols.shape)], axis=2)

    def key_tiles_t(v, tile):
        v_t = jnp.transpose(v.reshape(N_KV_GROUPS, -1, tile, HEAD_DIM), (0, 1, 3, 2))
        return jnp.concatenate([v_t, jnp.ones(v_t.shape[:2] + (SUM_ROWS, tile), v_t.dtype)], axis=2)

    vs_far = key_tiles_t(kvr[1][:, WINDOW:], KV_TILE)
    return k_cmp, v_cmp, v_cmp_t, ks, vs_far, key_tiles_t(kvr[1], LANE), kw, key_tiles_t(kvr[3], LANE)


def _nsa_layer(x, layer, norm_mix, w_in, q_norm, w_out, kvs):
    s = x.shape[0]
    tm = _row_tile(s)
    nq = N_HEADS * HEAD_DIM
    scale = HEAD_DIM ** -0.5 * LOG2E
    q_gain = jnp.tile(q_norm * scale, 4).reshape(1, 4 * HEAD_DIM)
    w_gate = w_in[layer, :, nq:].reshape(D_MODEL, N_KV_GROUPS, HEADS_PER_GROUP * N_BRANCH)
    w_gate = jnp.pad(w_gate, ((0, 0), (0, 0), (0, LANE - HEADS_PER_GROUP * N_BRANCH)))
    w_gate = w_gate.reshape(D_MODEL, N_KV_GROUPS * LANE)
    q, gates_t = _norm_matmul_pair(
        x, norm_mix, (w_in, layer), w_gate, q_gain, pl.BlockSpec((1, 4 * HEAD_DIM), lambda i, j: (0, 0)),
        tm=tm, tna=4 * HEAD_DIM, nja=nq // (4 * HEAD_DIM), col_a=0, tnb=N_KV_GROUPS * LANE, njb=1, col_b=0,
        ep_a=_ep_q, ep_b=_ep_gates_t, dtype_a=BF16, dtype_b=F32, transpose_b=True)
    o = _nsa_attention(q, gates_t, *kvs)
    return _matmul_res(o, (w_out, layer), x, tm=tm, tn=512)


def kernel(x, p, norm_mix, norm_ffn, norm_ple, a_w_in, a_norm_v, a_w_s, a_b_s, a_w_out, kv_norm, kv_w, k_norm, cmp_pe_k, cmp_pe_v, cmp_wk1, cmp_wk2, cmp_wv1, cmp_wv2, b_w_in, b_q_norm, b_w_out, rel_bias, ffn_w_in, ffn_w_out, ple_w, ple_gate):
    batch, s, d = x.shape
    depth = norm_mix.shape[0]
    n_a = a_w_in.shape[0]
    a_w_out, b_w_in, b_w_out, kv_w, ple_gate = (
        w.astype(BF16) for w in (a_w_out, b_w_in, b_w_out, kv_w, ple_gate))
    outs = []
    for b in range(batch):
        xb = x.reshape(s, d) if batch == 1 else x[b]
        kvs = None
        for i in range(depth):
            if i < n_a:
                xb = _gmlp_layer(xb, i, norm_mix[i], a_w_in, a_norm_v[i], a_w_s, a_b_s[i], a_w_out)
            else:
                j = i - n_a
                xb = _nsa_layer(xb, j, norm_mix[i], b_w_in, b_q_norm[j], b_w_out, kvs)
            xb = _ffn_ple(xb, p, i, b, norm_ffn[i], ffn_w_in, ffn_w_out,
                          norm_ple[i], ple_w, ple_gate)
            if i == n_a - 1:
                k_cmp, v_cmp, v_cmp_t, ks, vs_far, vs_near, kw, vw_near = _shared_kv(
                    xb, kv_norm, kv_w, k_norm, cmp_pe_k, cmp_pe_v, cmp_wk1, cmp_wk2, cmp_wv1, cmp_wv2)
                ovl = _overlap_padded(s // CMP_STRIDE, s // SLC_BLOCK)
                kvs = (k_cmp, v_cmp, v_cmp_t, jnp.asarray(ovl), jnp.asarray(ovl.T).astype(BF16),
                       ks, vs_far, vs_near, kw, vw_near, _bias_tiles(rel_bias))
        outs.append(xb)
    return outs[0].reshape(1, s, d) if batch == 1 else jnp.stack(outs)
```

```python
import functools
import math

import numpy as np
import jax
import jax.numpy as jnp
from jax import lax
from jax.experimental import pallas as pl
from jax.experimental.pallas import tpu as pltpu

F32 = jnp.float32
BF16 = jnp.bfloat16

D_MODEL = 2048
PLE_DIM = 256
FFN_DIM = 5632
GMLP_CHUNK = 128
GMLP_GROUPS = 16
HEAD_DIM = 128
N_HEADS = 16
N_KV_GROUPS = 2
HEADS_PER_GROUP = 8
N_BRANCH = 3
CMP_BLOCK = 32
CMP_STRIDE = 16
SLC_BLOCK = 64
N_SELECT = 16
WINDOW = 512
Q_BLOCK = 128
N_BUCKETS = 32
MAX_DISTANCE = 128
EPS = 1e-6
NEG = -1e30
BIG = 1e30
LOG2E = math.log2(math.e)

LANE = 128
KV_TILE = 512
NEAR = WINDOW + Q_BLOCK
CMP_PAD = 128
SUM_ROWS = 16
VMEM_LIMIT = 56 * 1024 * 1024


def _cparams(sem):
    return pltpu.CompilerParams(dimension_semantics=sem, vmem_limit_bytes=VMEM_LIMIT)


def _dot(a, b):
    return jnp.dot(a, b, preferred_element_type=F32)


def _dot_nt(a, b):
    return lax.dot_general(a, b, (((1,), (1,)), ((), ())), preferred_element_type=F32)


def _rms_rows(x, g):
    ms = jnp.mean(x * x, axis=-1, keepdims=True)
    return x * lax.rsqrt(ms + EPS) * g


def _group_rms(acc, gain):
    outs = []
    for c in range(acc.shape[1] // LANE):
        a = acc[:, c * LANE:(c + 1) * LANE]
        outs.append(_rms_rows(a, gain[:, c * LANE:(c + 1) * LANE]))
    return outs[0] if len(outs) == 1 else jnp.concatenate(outs, axis=1)


def _norm_matmul_kernel(x_ref, g_ref, *refs, n_w, n_aux, epilogue, out_dtype):
    w_refs = refs[:n_w]
    aux_refs = refs[n_w:n_w + n_aux]
    o_ref = refs[n_w + n_aux]
    h_scr = refs[n_w + n_aux + 1]

    @pl.when(pl.program_id(1) == 0)
    def _():
        h_scr[...] = _rms_rows(x_ref[...], g_ref[...]).astype(BF16)

    h = h_scr[...]
    accs = [_dot(h, w_ref[...].astype(BF16)) for w_ref in w_refs]
    o_ref[...] = epilogue(accs, aux_refs + (x_ref,)).astype(out_dtype)


def _w_spec(w, k, tn, col_of):
    if isinstance(w, tuple):
        layer = w[1]
        return pl.BlockSpec((None, k, tn), lambda i, j: (layer, 0, col_of(j)))
    return pl.BlockSpec((k, tn), lambda i, j: (0, col_of(j)))


def _w_array(w):
    return w[0] if isinstance(w, tuple) else w


def _norm_matmul(x, gain, w, col_offsets, *, tm, tn, nj, epilogue, out_dtype,
                 aux=(), aux_specs=()):
    s, k = x.shape
    in_specs = [pl.BlockSpec((tm, k), lambda i, j: (i, 0)),
                pl.BlockSpec((1, k), lambda i, j: (0, 0))]
    for c0 in col_offsets:
        in_specs.append(_w_spec(w, k, tn, lambda j, c0=c0: c0 + j))
    in_specs += list(aux_specs)
    kern = functools.partial(_norm_matmul_kernel, n_w=len(col_offsets), n_aux=len(aux),
                             epilogue=epilogue, out_dtype=out_dtype)
    return pl.pallas_call(
        kern,
        grid=(s // tm, nj),
        in_specs=in_specs,
        out_specs=pl.BlockSpec((tm, tn), lambda i, j: (i, j)),
        out_shape=jax.ShapeDtypeStruct((s, nj * tn), out_dtype),
        scratch_shapes=[pltpu.VMEM((tm, k), BF16)],
        compiler_params=_cparams(("arbitrary", "arbitrary")),
    )(x, gain.reshape(1, k), *([_w_array(w)] * len(col_offsets)), *aux)


def _norm_matmul_pair_kernel(x_ref, g_ref, wa_ref, wb_ref, aux_ref, oa_ref, ob_ref, h_scr, *, nja, ep_a, ep_b):
    j = pl.program_id(1)

    @pl.when(j == 0)
    def _():
        h_scr[...] = _rms_rows(x_ref[...], g_ref[...]).astype(BF16)

    @pl.when(j < nja)
    def _():
        oa_ref[...] = ep_a(_dot(h_scr[...], wa_ref[...].astype(BF16)), aux_ref, j).astype(oa_ref.dtype)

    @pl.when(j >= nja)
    def _():
        ob_ref[...] = ep_b(_dot(h_scr[...], wb_ref[...].astype(BF16)), aux_ref, j - nja).astype(ob_ref.dtype)


def _norm_matmul_pair(x, gain, wa, wb, aux, aux_spec, *, tm, tna, nja, col_a, tnb, njb, col_b, ep_a, ep_b,
                      dtype_a, dtype_b, transpose_b=False):
    s, k = x.shape
    if transpose_b:
        b_spec = pl.BlockSpec((tnb, tm), lambda i, j: (jnp.maximum(j - nja, 0), i))
        b_shape = (njb * tnb, s)
    else:
        b_spec = pl.BlockSpec((tm, tnb), lambda i, j: (i, jnp.maximum(j - nja, 0)))
        b_shape = (s, njb * tnb)
    kern = functools.partial(_norm_matmul_pair_kernel, nja=nja, ep_a=ep_a, ep_b=ep_b)
    return pl.pallas_call(
        kern,
        grid=(s // tm, nja + njb),
        in_specs=[pl.BlockSpec((tm, k), lambda i, j: (i, 0)),
                  pl.BlockSpec((1, k), lambda i, j: (0, 0)),
                  _w_spec(wa, k, tna, lambda j: col_a + jnp.minimum(j, nja - 1)),
                  _w_spec(wb, k, tnb, lambda j: col_b + jnp.maximum(j - nja, 0)),
                  aux_spec],
        out_specs=[pl.BlockSpec((tm, tna), lambda i, j: (i, jnp.minimum(j, nja - 1))), b_spec],
        out_shape=[jax.ShapeDtypeStruct((s, nja * tna), dtype_a), jax.ShapeDtypeStruct(b_shape, dtype_b)],
        scratch_shapes=[pltpu.VMEM((tm, k), BF16)],
        compiler_params=_cparams(("arbitrary", "arbitrary")),
    )(x, gain.reshape(1, k), _w_array(wa), _w_array(wb), aux)


def _ep_swiglu(accs, aux):
    g, u = accs
    return g * jax.nn.sigmoid(g) * u


def _ep_ple(accs, aux):
    p_ref, wp_ref, x_ref = aux
    tn = accs[0].shape[1]
    col0 = pl.multiple_of(pl.program_id(1) * tn, tn)
    pp = _dot(p_ref[...].astype(BF16), wp_ref[...].astype(BF16))
    return x_ref[:, pl.ds(col0, tn)] + pp * jax.nn.sigmoid(accs[0])


def _ep_identity(acc, aux_ref, j):
    return acc


def _ep_gates_t(acc, aux_ref, j):
    return jax.nn.sigmoid(acc).T


def _ep_q(acc, gain_ref, j):
    return _group_rms(acc, gain_ref[...])


def _ep_kv(acc, gain_ref, j):
    normed = _group_rms(acc, gain_ref[0:1, :])
    return jnp.where((j % 2) == 0, normed, acc)


def _matmul_res_kernel(a_ref, w_ref, x_ref, o_ref):
    o_ref[...] = x_ref[...] + _dot(a_ref[...], w_ref[...].astype(BF16))


def _matmul_res(a, w, resid, *, tm, tn):
    s, k = a.shape
    n = _w_array(w).shape[-1]
    return pl.pallas_call(
        _matmul_res_kernel,
        grid=(s // tm, n // tn),
        in_specs=[pl.BlockSpec((tm, k), lambda i, j: (i, 0)),
                  _w_spec(w, k, tn, lambda j: j),
                  pl.BlockSpec((tm, tn), lambda i, j: (i, j))],
        out_specs=pl.BlockSpec((tm, tn), lambda i, j: (i, j)),
        out_shape=jax.ShapeDtypeStruct((s, n), F32),
        compiler_params=_cparams(("arbitrary", "arbitrary")),
    )(a, _w_array(w), resid)


GMLP_TN = 512
PROJ_TN = 1024


def _gmlp_kernel(x_ref, g_ref, win_ref, nv_ref, ws_ref, bs_ref, wo_ref, o_ref, h_scr, z_scr, y_scr, *, tm):
    width = GMLP_GROUPS * LANE
    n_in = 2 * width // GMLP_TN
    per_tile = GMLP_TN // LANE
    n_chunk = tm // GMLP_CHUNK
    j = pl.program_id(1)

    @pl.when(j == 0)
    def _():
        h_scr[...] = _rms_rows(x_ref[...], g_ref[...]).astype(BF16)

    @pl.when(j < n_in)
    def _():
        z_scr[j] = jax.nn.gelu(_dot(h_scr[...], win_ref[...].astype(BF16)))

    @pl.when(j == n_in)
    def _():
        v_tiles = range(n_in // 2, n_in)
        ms = sum(jnp.sum(z_scr[t] * z_scr[t], axis=-1, keepdims=True) for t in v_tiles) / width
        inv = lax.rsqrt(ms + EPS)
        row = lax.broadcasted_iota(jnp.int32, (GMLP_CHUNK, GMLP_CHUNK), 0)
        col = lax.broadcasted_iota(jnp.int32, (GMLP_CHUNK, GMLP_CHUNK), 1)
        causal = col <= row
        for g in range(GMLP_GROUPS):
            t, ls = g // per_tile, slice((g % per_tile) * LANE, (g % per_tile + 1) * LANE)
            cs = slice(g * LANE, (g + 1) * LANE)
            vn = (z_scr[n_in // 2 + t, :, ls] * inv * nv_ref[:, cs]).astype(BF16)
            ws = jnp.where(causal, ws_ref[g], 0.0).astype(BF16)
            vg = jnp.concatenate(
                [vn[c * GMLP_CHUNK:(c + 1) * GMLP_CHUNK, :] for c in range(n_chunk)], axis=1)
            sv = _dot(ws, vg) + bs_ref[:, g:g + 1]
            for c in range(n_chunk):
                rs = slice(c * GMLP_CHUNK, (c + 1) * GMLP_CHUNK)
                y_scr[rs, cs] = (z_scr[t, rs, ls] * sv[:, c * LANE:(c + 1) * LANE]).astype(BF16)

    @pl.when(j >= n_in)
    def _():
        col0 = pl.multiple_of((j - n_in) * GMLP_TN, GMLP_TN)
        o_ref[...] = x_ref[:, pl.ds(col0, GMLP_TN)] + _dot(y_scr[...], wo_ref[...].astype(BF16))


def _gmlp(x, gain, w_in, norm_v, w_s, b_s_t, w_out, layer, *, tm):
    s, d = x.shape
    width = GMLP_GROUPS * LANE
    n_in = 2 * width // GMLP_TN
    n_out = d // GMLP_TN
    return pl.pallas_call(
        functools.partial(_gmlp_kernel, tm=tm),
        grid=(s // tm, n_in + n_out),
        in_specs=[pl.BlockSpec((tm, d), lambda i, j: (i, 0), pipeline_mode=pl.Buffered(1)),
                  pl.BlockSpec((1, d), lambda i, j: (0, 0)),
                  pl.BlockSpec((None, d, GMLP_TN), lambda i, j: (layer, 0, jnp.minimum(j, n_in - 1))),
                  pl.BlockSpec((1, width), lambda i, j: (0, 0)),
                  pl.BlockSpec((None, GMLP_GROUPS, GMLP_CHUNK, GMLP_CHUNK), lambda i, j: (layer, 0, 0, 0)),
                  pl.BlockSpec((GMLP_CHUNK, GMLP_GROUPS), lambda i, j: (0, 0)),
                  pl.BlockSpec((None, width, GMLP_TN), lambda i, j: (layer, 0, jnp.maximum(j - n_in, 0)))],
        out_specs=pl.BlockSpec((tm, GMLP_TN), lambda i, j: (i, jnp.maximum(j - n_in, 0))),
        out_shape=jax.ShapeDtypeStruct((s, d), F32),
        scratch_shapes=[pltpu.VMEM((tm, d), BF16),
                        pltpu.VMEM((n_in, tm, GMLP_TN), F32),
                        pltpu.VMEM((tm, width), BF16)],
        compiler_params=_cparams(("arbitrary", "arbitrary")),
    )(x, gain.reshape(1, d), w_in, norm_v.reshape(1, width), w_s, b_s_t, w_out)


def _compress_kernel(kv_ref, pe_ref, w1_ref, w2_ref, kn_ref, o_ref, *, n_half):
    half_w = CMP_STRIDE * HEAD_DIM
    hh = jnp.concatenate([kv_ref[pl.ds(r, n_half, stride=CMP_STRIDE), :] for r in range(CMP_STRIDE)],
                         axis=1)
    pe = pe_ref[...]
    a = _dot((hh + pe[0:1, :]).astype(BF16), w1_ref[0:half_w, :].astype(BF16))
    b = _dot((hh + pe[1:2, :]).astype(BF16), w1_ref[half_w:2 * half_w, :].astype(BF16))
    pre = a + pltpu.roll(b, n_half - 1, 0)
    out = _dot(jax.nn.gelu(pre).astype(BF16), w2_ref[...].astype(BF16))
    is_key = pl.program_id(0) < N_KV_GROUPS
    out = jnp.where(is_key, _rms_rows(out, kn_ref[...]), out)
    row = lax.broadcasted_iota(jnp.int32, out.shape, 0)
    o_ref[...] = jnp.where(row < n_half - 1, out, 0.0)


def _compress(kvc, pe, w1, w2, k_norm0):
    s = kvc.shape[0]
    n_half = s // CMP_STRIDE
    half_w = CMP_STRIDE * HEAD_DIM
    hid = w1.shape[2]
    return pl.pallas_call(
        functools.partial(_compress_kernel, n_half=n_half),
        grid=(2 * N_KV_GROUPS,),
        in_specs=[pl.BlockSpec((s, HEAD_DIM), lambda n: (0, n)),
                  pl.BlockSpec((None, 2, half_w), lambda n: (n // N_KV_GROUPS, 0, 0)),
                  pl.BlockSpec((None, 2 * half_w, hid), lambda n: (n // N_KV_GROUPS, 0, 0)),
                  pl.BlockSpec((None, hid, HEAD_DIM), lambda n: (n // N_KV_GROUPS, 0, 0)),
                  pl.BlockSpec((1, HEAD_DIM), lambda n: (0, 0))],
        out_specs=pl.BlockSpec((None, n_half, HEAD_DIM), lambda n: (n, 0, 0)),
        out_shape=jax.ShapeDtypeStruct((2 * N_KV_GROUPS, n_half, HEAD_DIM), F32),
        compiler_params=_cparams(("arbitrary",)),
    )(kvc, pe, w1, w2, k_norm0.reshape(1, HEAD_DIM))


def _t5_bucket_np(dist):
    n = np.maximum(dist, 0)
    max_exact = N_BUCKETS // 2
    nf = np.maximum(n, 1).astype(np.float32)
    large = max_exact + (np.log(nf / np.float32(max_exact)) / np.float32(math.log(MAX_DISTANCE / max_exact))
                         * np.float32(N_BUCKETS - max_exact)).astype(np.int32)
    large = np.minimum(large, N_BUCKETS - 1)
    return np.where(n < max_exact, n, large).astype(np.int32)


N_PATTERNS = 4


def _bucket_patterns():
    i = np.arange(Q_BLOCK)[:, None]
    c = np.arange(LANE)[None, :]
    d0 = i - c
    d1 = i - c + Q_BLOCK
    dc = i - CMP_STRIDE * (c - (LANE - 8)) - (CMP_BLOCK - 1)
    pats = [np.where(d >= 0, _t5_bucket_np(d), -1) for d in (d0, d1, dc)]
    pats.append(np.where(i < c, N_BUCKETS - 1, -1))
    return np.stack([p.T for p in pats]).astype(np.int32)


def _bias_tiles_kernel(tab_ref, pat_ref, o_ref):
    h = pl.program_id(0)
    pat = pat_ref[...]
    far = tab_ref[N_BUCKETS - 1, h]
    acc = jnp.full(pat.shape, NEG, F32)
    for b in range(N_BUCKETS):
        acc = jnp.where(pat == b, (tab_ref[b, h] - far) * LOG2E, acc)
    o_ref[...] = acc


def _bias_tiles(rel_bias):
    pats = jnp.asarray(_bucket_patterns())
    return pl.pallas_call(
        _bias_tiles_kernel,
        grid=(N_HEADS,),
        in_specs=[pl.BlockSpec(memory_space=pltpu.SMEM),
                  pl.BlockSpec((N_PATTERNS, Q_BLOCK, LANE), lambda h: (0, 0, 0))],
        out_specs=pl.BlockSpec((None, N_PATTERNS, Q_BLOCK, LANE), lambda h: (h, 0, 0, 0)),
        out_shape=jax.ShapeDtypeStruct((N_HEADS, N_PATTERNS, Q_BLOCK, LANE), F32),
        compiler_params=_cparams(("arbitrary",)),
    )(rel_bias, pats)


def _with_features(qs, feat):
    reps = qs.shape[0] // feat.shape[0]
    return jnp.concatenate([qs, jnp.concatenate([feat] * reps, axis=0)], axis=1)


def _softmax_cols(blocks):
    m = blocks[0].max(axis=0, keepdims=True)
    for b in blocks[1:]:
        m = jnp.maximum(m, b.max(axis=0, keepdims=True))
    es = [jnp.exp2(b - m) for b in blocks]
    den = es[0].sum(axis=0, keepdims=True)
    for e in es[1:]:
        den = den + e.sum(axis=0, keepdims=True)
    return es, m, den


def _nsa_kernel(q_ref, gt_ref, kc_ref, vc_ref, vct_ref, ovl_ref, ovlt_ref, ks_ref, vsf_ref, vsn_ref,
                kw_ref, vwn_ref, tb_ref, o_ref, m_scr, acc_scr, sa_scr, sb_scr, cmp_o_scr, cmp_pf_scr, cmp_pn_scr,
                *, n_slc):
    qb = pl.program_id(1)
    s = qb * Q_BLOCK
    hpg = HEADS_PER_GROUP

    q_all = q_ref[...]
    qs = jnp.concatenate([q_all[:, h * LANE:(h + 1) * LANE] for h in range(hpg)], axis=0)
    cols = [slice(h * Q_BLOCK, (h + 1) * Q_BLOCK) for h in range(hpg)]
    lane_f = lax.broadcasted_iota(jnp.int32, (Q_BLOCK, LANE), 1)
    pad_feat = jnp.where(lane_f == LANE - 1, 1.0, 0.0).astype(BF16)
    q_pad = _with_features(qs, pad_feat)

    n_sub = NEAR // LANE
    row0 = pl.multiple_of(s, Q_BLOCK)

    n_pair = hpg // 2
    pair_rows = [slice(pr * 2 * Q_BLOCK, (pr + 1) * 2 * Q_BLOCK) for pr in range(n_pair)]
    half = [slice(0, Q_BLOCK), slice(Q_BLOCK, 2 * Q_BLOCK)]

    def run_stages(stages):
        pending = stages[0][0]()
        for i, (_, consume) in enumerate(stages):
            cur = pending
            if i + 1 < len(stages):
                pending = stages[i + 1][0]()
            consume(cur)

    def near_stages(k_ref, q_aug, vn_ref, oldest_pat, res):
        k_aug = k_ref[pl.ds(row0, NEAR), :]
        v_blocks = vn_ref[pl.ds(qb, n_sub)]
        v_t = jnp.concatenate([v_blocks[u] for u in range(n_sub)], axis=1)

        def stage(pr):
            def issue():
                return _dot_nt(k_aug, q_aug[pair_rows[pr], :])

            def consume(logits):
                ps = []
                for hh in range(2):
                    h = 2 * pr + hh
                    blocks = [logits[u * LANE:(u + 1) * LANE, half[hh]] for u in range(n_sub)]
                    blocks[n_sub - 1] = blocks[n_sub - 1] + tb_ref[h, 0]
                    blocks[n_sub - 2] = blocks[n_sub - 2] + tb_ref[h, 1]
                    if oldest_pat is not None:
                        blocks[0] = blocks[0] + tb_ref[h, oldest_pat]
                    m = blocks[0].max(axis=0, keepdims=True)
                    for b in blocks[1:]:
                        m = jnp.maximum(m, b.max(axis=0, keepdims=True))
                    ps.append(jnp.concatenate([jnp.exp2((b - m).astype(BF16)) for b in blocks], axis=0))
                    res["m"].append(m)
                res["o"].append(_dot(v_t, jnp.concatenate(ps, axis=1)))
            return issue, consume
        return [stage(pr) for pr in range(n_pair)]

    near0 = pl.multiple_of(qb * 8 + 8, 8)
    kn = kc_ref[pl.ds(near0, LANE), :].astype(BF16)
    vn_t = vc_ref[pl.ds(near0, LANE), :].T.astype(BF16)
    ovl_n_t = ovl_ref[pl.ds(near0, LANE), :].T.astype(BF16)
    far_feat = jnp.where(lane_f > qb - 16, 1.0, 0.0).astype(BF16)
    q_cmp = _with_features(qs, far_feat)
    t_row = s + lax.broadcasted_iota(jnp.int32, (1, Q_BLOCK), 1)
    row_ok = t_row >= CMP_BLOCK - 1
    n_half = kc_ref.shape[0] - CMP_PAD

    def cmp_branch(n_keys):
        kc = kc_ref[CMP_PAD:CMP_PAD + n_keys, :].astype(BF16)
        vc_t = vct_ref[:, CMP_PAD:CMP_PAD + n_keys]
        res = {"o": [], "pf": jnp.zeros((n_keys, Q_BLOCK), F32), "pn": jnp.zeros((LANE, Q_BLOCK), F32)}

        def stage(pr):
            def issue():
                return _dot_nt(kc, q_cmp[pair_rows[pr], :]), _dot_nt(kn, q_pad[pair_rows[pr], :])

            def consume(logits):
                sf, sn = logits
                pfs, pns = [], []
                for hh in range(2):
                    (ef, en), _, den = _softmax_cols([sf[:, half[hh]], sn[:, half[hh]] + tb_ref[2 * pr + hh, 2]])
                    inv = jnp.where(row_ok, 1.0 / den, 0.0)
                    pf = ef * inv
                    pn = en * inv
                    res["pf"] = res["pf"] + pf
                    res["pn"] = res["pn"] + pn
                    pfs.append(pf.astype(BF16))
                    pns.append(pn.astype(BF16))
                res["o"].append(_dot(vc_t, jnp.concatenate(pfs, axis=1)) + _dot(vn_t, jnp.concatenate(pns, axis=1)))
            return issue, consume

        run_stages([stage(pr) for pr in range(n_pair)])
        cmp_o_scr[...] = jnp.concatenate(res["o"], axis=1)
        cmp_pf_scr[0:n_keys, :] = res["pf"]
        if n_keys < n_half:
            cmp_pf_scr[n_keys:, :] = jnp.zeros((n_half - n_keys, Q_BLOCK), F32)
        cmp_pn_scr[...] = res["pn"]

    sizes = sorted({min(n_half, c) for c in (LANE, 2 * LANE)} | {n_half})
    lo = 0
    for idx, n_keys in enumerate(sizes):
        last = idx + 1 == len(sizes)
        hi = (n_keys + LANE - 8) // 8 + 1
        pl.when((qb >= lo) if last else ((qb >= lo) & (qb < hi)))(functools.partial(cmp_branch, n_keys))
        lo = hi

    blk = lax.broadcasted_iota(jnp.int32, (n_slc, Q_BLOCK), 0)
    blk_f = blk.astype(F32)
    cur = (s + lax.broadcasted_iota(jnp.int32, (n_slc, Q_BLOCK), 1)) >> 6
    forced = (blk == 0) | (blk == cur) | (blk == cur - 1)
    valid = blk <= cur
    topk = {}

    def importance_stage():
        def consume(_):
            psum_f, psum_n = cmp_pf_scr[...], cmp_pn_scr[...]
            ovl_t = ovlt_ref[:, CMP_PAD:]
            pf_hi = psum_f.astype(BF16)
            pf_lo = (psum_f - pf_hi.astype(F32)).astype(BF16)
            pn_hi = psum_n.astype(BF16)
            pn_lo = (psum_n - pn_hi.astype(F32)).astype(BF16)
            imp_t = _dot(ovl_t, pf_hi) + _dot(ovl_t, pf_lo) + _dot(ovl_n_t, pn_hi) + _dot(ovl_n_t, pn_lo)
            topk["score"] = jnp.where(valid & ~forced, imp_t, -BIG)
            topk["sel"] = jnp.where(forced, 1.0, 0.0)
        return (lambda: None), consume

    def topk_stage(rounds):
        def consume(_):
            score, sel_t = topk["score"], topk["sel"]
            for _r in range(rounds):
                top = jnp.max(score, axis=0, keepdims=True)
                first = jnp.min(jnp.where(score == top, blk_f, float(n_slc)), axis=0, keepdims=True)
                pick = blk_f == first
                sel_t = jnp.where(pick, 1.0, sel_t)
                score = jnp.where(pick, -2.0 * BIG, score)
            topk["score"], topk["sel"] = score, sel_t
        return (lambda: None), consume

    win_res = {"o": [], "m": []}
    win_stages = near_stages(kw_ref, q_pad, vwn_ref, 3, win_res)
    free_picks = max(min(N_SELECT, n_slc) - 3, 0)
    rounds = [free_picks // n_pair + (1 if pr < free_picks % n_pair else 0) for pr in range(n_pair)]
    stages = [importance_stage()]
    for pr in range(n_pair):
        stages += [win_stages[pr], topk_stage(rounds[pr])]
    run_stages(stages)
    o_win = jnp.concatenate(win_res["o"], axis=1)
    o_cmp = cmp_o_scr[...]
    sel_t = topk["sel"]
    drop_t = jnp.where(valid, 1.0 - sel_t, 1.0)
    near_blk = (s - WINDOW) >> 6
    drop_far_t = jnp.where(blk >= near_blk, 1.0, drop_t)
    q_near = _with_features(qs, drop_t.T.astype(BF16))
    q_far = _with_features(qs, drop_far_t.T.astype(BF16))

    def far_logits(t):
        r0 = pl.multiple_of(WINDOW + KV_TILE * t, KV_TILE)
        return _dot_nt(ks_ref[pl.ds(r0, KV_TILE), :], q_far)

    sa_scr[...] = far_logits(0)
    slc_res = {"o": [], "m": []}
    run_stages(near_stages(ks_ref, q_near, vsn_ref, None, slc_res))
    m_scr[...] = jnp.concatenate(slc_res["m"], axis=1)
    acc_scr[...] = jnp.concatenate(slc_res["o"], axis=1)

    gates = gt_ref[...]
    g_c, g_s, g_w = [jnp.concatenate([gates[3 * h + br:3 * h + br + 1, :] for h in range(hpg)], axis=1)
                     for br in range(N_BRANCH)]
    out_cw = g_c * o_cmp + (g_w / o_win[HEAD_DIM:HEAD_DIM + 1, :]) * o_win[:HEAD_DIM, :]

    n_far = (jnp.maximum(s - WINDOW, 0) + KV_TILE - 1) // KV_TILE

    n_tiles = vsf_ref.shape[0]

    def far_probs(st_ref):
        m_old = m_scr[...]
        sts = [st_ref[:, cols[h]] for h in range(hpg)]
        m_new = jnp.concatenate(
            [jnp.maximum(m_old[:, cols[h]], sts[h].max(axis=0, keepdims=True)) for h in range(hpg)], axis=1)
        p_t = jnp.concatenate(
            [jnp.exp2((sts[h] - m_new[:, cols[h]]).astype(BF16)) for h in range(hpg)], axis=1)
        m_scr[...] = m_new
        return p_t, jnp.exp2(m_old - m_new)

    def far_accumulate(t, p_t, alpha):
        acc_scr[...] = alpha * acc_scr[...] + _dot(vsf_ref[t], p_t)

    def far_body(i, carry):
        t0 = 2 * i
        sb_scr[...] = far_logits(t0 + 1)
        p_a, alpha_a = far_probs(sa_scr)
        sa_scr[...] = far_logits(jnp.minimum(t0 + 2, n_tiles - 1))
        far_accumulate(t0, p_a, alpha_a)
        p_b, alpha_b = far_probs(sb_scr)
        far_accumulate(t0 + 1, p_b, alpha_b)
        return carry

    lax.fori_loop(0, (n_far + 1) // 2, far_body, 0)

    out_t = out_cw + (g_s / acc_scr[HEAD_DIM:HEAD_DIM + 1, :]) * acc_scr[:HEAD_DIM, :]
    for h in range(hpg):
        o_ref[:, cols[h]] = out_t[:, cols[h]].T.astype(o_ref.dtype)


def _nsa_attention(q, gates_t, k_cmp, v_cmp, v_cmp_t, ovl, ovl_t, ks, vs_far, vs_near, kw, vw_near, bias_tiles):
    s = q.shape[0]
    n_qb = s // Q_BLOCK
    n_slc = s // SLC_BLOCK
    gw = HEADS_PER_GROUP * HEAD_DIM
    lanes = HEADS_PER_GROUP * Q_BLOCK

    def group_spec(a):
        zeros = (0,) * (a.ndim - 1)
        return pl.BlockSpec((None,) + a.shape[1:], lambda g, i: (g,) + zeros)

    def whole_spec(a):
        zeros = (0,) * a.ndim
        return pl.BlockSpec(a.shape, lambda g, i: zeros)

    return pl.pallas_call(
        functools.partial(_nsa_kernel, n_slc=n_slc),
        grid=(N_KV_GROUPS, n_qb),
        in_specs=[pl.BlockSpec((Q_BLOCK, gw), lambda g, i: (i, g)),
                  pl.BlockSpec((LANE, Q_BLOCK), lambda g, i: (g, i)),
                  group_spec(k_cmp), group_spec(v_cmp), group_spec(v_cmp_t),
                  whole_spec(ovl), whole_spec(ovl_t),
                  group_spec(ks), group_spec(vs_far), group_spec(vs_near),
                  group_spec(kw), group_spec(vw_near),
                  pl.BlockSpec((HEADS_PER_GROUP, N_PATTERNS, LANE, Q_BLOCK), lambda g, i: (g, 0, 0, 0))],
        out_specs=pl.BlockSpec((Q_BLOCK, gw), lambda g, i: (i, g)),
        out_shape=jax.ShapeDtypeStruct((s, N_KV_GROUPS * gw), BF16),
        scratch_shapes=[pltpu.VMEM((1, lanes), F32),
                        pltpu.VMEM((HEAD_DIM + SUM_ROWS, lanes), F32),
                        pltpu.VMEM((KV_TILE, lanes), F32),
                        pltpu.VMEM((KV_TILE, lanes), F32),
                        pltpu.VMEM((HEAD_DIM, lanes), F32),
                        pltpu.VMEM((k_cmp.shape[1] - CMP_PAD, Q_BLOCK), F32),
                        pltpu.VMEM((LANE, Q_BLOCK), F32)],
        compiler_params=_cparams(("arbitrary", "arbitrary")),
    )(q, gates_t, k_cmp, v_cmp, v_cmp_t, ovl, ovl_t, ks, vs_far, vs_near, kw, vw_near, bias_tiles)


def _cmp_mask_columns(n_half):
    assert n_half // 8 < LANE - 1
    out = np.zeros((CMP_PAD + n_half, LANE), np.float32)
    out[np.arange(CMP_PAD), LANE - 1] = NEG
    k = np.arange(n_half)
    out[CMP_PAD + k, k // 8] = NEG
    return out


def _slc_mask_columns(s):
    n_slc = s // SLC_BLOCK
    assert n_slc - 1 > (WINDOW + Q_BLOCK) // SLC_BLOCK
    out = np.zeros((WINDOW + s, n_slc), np.float32)
    out[np.arange(WINDOW), n_slc - 1] = NEG
    pos = np.arange(s)
    out[WINDOW + pos, pos // SLC_BLOCK] = NEG
    return out


def _pad_mask_columns(s):
    out = np.zeros((WINDOW + s, LANE), np.float32)
    out[np.arange(WINDOW), LANE - 1] = NEG
    return out


def _overlap_padded(n_half, n_slc):
    n_cmp = n_half - 1
    c0 = np.arange(n_cmp) * CMP_STRIDE
    s0 = np.arange(n_slc) * SLC_BLOCK
    lo = np.maximum(c0[:, None], s0[None, :])
    hi = np.minimum(c0[:, None] + CMP_BLOCK, s0[None, :] + SLC_BLOCK)
    ovl = np.maximum(hi - lo, 0).astype(np.float32) / CMP_BLOCK
    out = np.zeros((CMP_PAD + n_half, n_slc), np.float32)
    out[CMP_PAD:CMP_PAD + n_cmp] = ovl
    return out


def _row_tile(s):
    return min(1024, s)


def _ffn_ple(x, p, layer, b, norm_ffn, w_in, w_out, norm_ple, ple_w, ple_gate):
    s = x.shape[0]
    tm = _row_tile(s)
    tn = 512
    nj = FFN_DIM // tn
    act = _norm_matmul(x, norm_ffn, (w_in, layer), (0, nj), tm=tm, tn=tn, nj=nj,
                       epilogue=_ep_swiglu, out_dtype=BF16)
    x = _matmul_res(act, (w_out, layer), x, tm=tm, tn=256)
    aux = (p, ple_w)
    tn_ple = PROJ_TN
    aux_specs = (pl.BlockSpec((None, None, tm, PLE_DIM), lambda i, j: (layer, b, i, 0)),
                 pl.BlockSpec((None, PLE_DIM, tn_ple), lambda i, j: (layer, 0, j)))
    return _norm_matmul(x, norm_ple, (ple_gate, layer), (0,), tm=tm, tn=tn_ple, nj=D_MODEL // tn_ple,
                        epilogue=_ep_ple, out_dtype=F32, aux=aux, aux_specs=aux_specs)


def _gmlp_layer(x, layer, norm_mix, w_in, norm_v, w_s, b_s, w_out):
    return _gmlp(x, norm_mix, w_in, norm_v, w_s, jnp.transpose(b_s), w_out, layer, tm=_row_tile(x.shape[0]))


def _shared_kv(x, kv_norm, kv_w, k_norm, cmp_pe_k, cmp_pe_v, cmp_wk1, cmp_wk2, cmp_wv1, cmp_wv2):
    s = x.shape[0]
    tm = _row_tile(s)
    gw = N_KV_GROUPS * HEAD_DIM
    gains = jnp.stack([jnp.tile(k_norm[1], N_KV_GROUPS), jnp.ones((gw,), F32),
                       jnp.tile(k_norm[2], N_KV_GROUPS), jnp.ones((gw,), F32)])
    gains = jnp.broadcast_to(gains[:, None, :], (4, 8, gw))
    kvc, kvr = _norm_matmul_pair(
        x, kv_norm, kv_w, kv_w, gains, pl.BlockSpec((None, 8, gw), lambda i, j: (jnp.maximum(j - 1, 0), 0, 0)),
        tm=tm, tna=2 * gw, nja=1, col_a=0, tnb=gw, njb=4, col_b=2,
        ep_a=_ep_identity, ep_b=_ep_kv, dtype_a=F32, dtype_b=BF16)
    n_half = s // CMP_STRIDE
    pe = jnp.stack([cmp_pe_k, cmp_pe_v]).reshape(2, 2, CMP_STRIDE * HEAD_DIM)
    kv_cmp = _compress(kvc, pe, jnp.stack([cmp_wk1, cmp_wv1]), jnp.stack([cmp_wk2, cmp_wv2]), k_norm[0])
    kv_cmp = jnp.pad(kv_cmp, ((0, 0), (CMP_PAD, 0), (0, 0)))
    cmp_cols = jnp.broadcast_to(jnp.asarray(_cmp_mask_columns(n_half)), (N_KV_GROUPS, CMP_PAD + n_half, LANE))
    k_cmp = jnp.concatenate([kv_cmp[:N_KV_GROUPS], cmp_cols], axis=2)
    v_cmp = kv_cmp[N_KV_GROUPS:]
    v_cmp_t = jnp.transpose(v_cmp, (0, 2, 1)).astype(BF16)
    s_pad = WINDOW + s
    kvr = jnp.pad(kvr, ((WINDOW, 0), (0, 0))).reshape(s_pad, 4, N_KV_GROUPS, HEAD_DIM)
    kvr = jnp.transpose(kvr, (1, 2, 0, 3))
    slc_cols = jnp.asarray(_slc_mask_columns(s)).astype(BF16)
    pad_cols = jnp.asarray(_pad_mask_columns(s)).astype(BF16)
    ks = jnp.concatenate([kvr[0], jnp.broadcast_to(slc_cols, (N_KV_GROUPS,) + slc_cols.shape)], axis=2)
    kw = jnp.concatenate([kvr[2], jnp.broadcast_to(pad_cols, (N_KV_GROUPS,) + pad_cols.shape)], axis=2)

    def key_tiles_t(v, tile):
        v_t = jnp.transpose(v.reshape(N_KV_GROUPS, -1, tile, HEAD_DIM), (0, 1, 3, 2))
        return jnp.concatenate([v_t, jnp.ones(v_t.shape[:2] + (SUM_ROWS, tile), v_t.dtype)], axis=2)

    vs_far = key_tiles_t(kvr[1][:, WINDOW:], KV_TILE)
    return k_cmp, v_cmp, v_cmp_t, ks, vs_far, key_tiles_t(kvr[1], LANE), kw, key_tiles_t(kvr[3], LANE)


def _nsa_layer(x, layer, norm_mix, w_in, q_norm, w_out, kvs):
    s = x.shape[0]
    tm = _row_tile(s)
    nq = N_HEADS * HEAD_DIM
    scale = HEAD_DIM ** -0.5 * LOG2E
    q_gain = jnp.tile(q_norm * scale, PROJ_TN // HEAD_DIM).reshape(1, PROJ_TN)
    w_gate = w_in[layer, :, nq:].reshape(D_MODEL, N_KV_GROUPS, HEADS_PER_GROUP * N_BRANCH)
    w_gate = jnp.pad(w_gate, ((0, 0), (0, 0), (0, LANE - HEADS_PER_GROUP * N_BRANCH)))
    w_gate = w_gate.reshape(D_MODEL, N_KV_GROUPS * LANE)
    q, gates_t = _norm_matmul_pair(
        x, norm_mix, (w_in, layer), w_gate, q_gain, pl.BlockSpec((1, PROJ_TN), lambda i, j: (0, 0)),
        tm=tm, tna=PROJ_TN, nja=nq // PROJ_TN, col_a=0, tnb=N_KV_GROUPS * LANE, njb=1, col_b=0,
        ep_a=_ep_q, ep_b=_ep_gates_t, dtype_a=BF16, dtype_b=F32, transpose_b=True)
    o = _nsa_attention(q, gates_t, *kvs)
    return _matmul_res(o, (w_out, layer), x, tm=tm, tn=PROJ_TN)


def kernel(x, p, norm_mix, norm_ffn, norm_ple, a_w_in, a_norm_v, a_w_s, a_b_s, a_w_out, kv_norm, kv_w, k_norm, cmp_pe_k, cmp_pe_v, cmp_wk1, cmp_wk2, cmp_wv1, cmp_wv2, b_w_in, b_q_norm, b_w_out, rel_bias, ffn_w_in, ffn_w_out, ple_w, ple_gate):
    batch, s, d = x.shape
    depth = norm_mix.shape[0]
    n_a = a_w_in.shape[0]
    a_w_out, b_w_in, b_w_out, kv_w, ple_gate = (
        w.astype(BF16) for w in (a_w_out, b_w_in, b_w_out, kv_w, ple_gate))
    outs = []
    for b in range(batch):
        xb = x.reshape(s, d) if batch == 1 else x[b]
        kvs = None
        for i in range(depth):
            if i < n_a:
                xb = _gmlp_layer(xb, i, norm_mix[i], a_w_in, a_norm_v[i], a_w_s, a_b_s[i], a_w_out)
            else:
                j = i - n_a
                xb = _nsa_layer(xb, j, norm_mix[i], b_w_in, b_q_norm[j], b_w_out, kvs)
            xb = _ffn_ple(xb, p, i, b, norm_ffn[i], ffn_w_in, ffn_w_out,
                          norm_ple[i], ple_w, ple_gate)
            if i == n_a - 1:
                k_cmp, v_cmp, v_cmp_t, ks, vs_far, vs_near, kw, vw_near = _shared_kv(
                    xb, kv_norm, kv_w, k_norm, cmp_pe_k, cmp_pe_v, cmp_wk1, cmp_wk2, cmp_wv1, cmp_wv2)
                ovl = _overlap_padded(s // CMP_STRIDE, s // SLC_BLOCK)
                kvs = (k_cmp, v_cmp, v_cmp_t, jnp.asarray(ovl), jnp.asarray(ovl.T).astype(BF16),
                       ks, vs_far, vs_near, kw, vw_near, _bias_tiles(rel_bias))
        outs.append(xb)
    return outs[0].reshape(1, s, d) if batch == 1 else jnp.stack(outs)
```

```python
import functools
import math

import numpy as np
import jax
import jax.numpy as jnp
from jax import lax
from jax.experimental import pallas as pl
from jax.experimental.pallas import tpu as pltpu

F32 = jnp.float32
BF16 = jnp.bfloat16

D_MODEL = 2048
PLE_DIM = 256
FFN_DIM = 5632
GMLP_CHUNK = 128
GMLP_GROUPS = 16
HEAD_DIM = 128
N_HEADS = 16
N_KV_GROUPS = 2
HEADS_PER_GROUP = 8
N_BRANCH = 3
CMP_BLOCK = 32
CMP_STRIDE = 16
SLC_BLOCK = 64
N_SELECT = 16
WINDOW = 512
Q_BLOCK = 128
N_BUCKETS = 32
MAX_DISTANCE = 128
EPS = 1e-6
NEG = -1e30
BIG = 1e30
LOG2E = math.log2(math.e)

LANE = 128
KV_TILE = 512
NEAR = WINDOW + Q_BLOCK
CMP_PAD = 128
SUM_ROWS = 16
VMEM_LIMIT = 56 * 1024 * 1024


def _cparams(sem):
    return pltpu.CompilerParams(dimension_semantics=sem, vmem_limit_bytes=VMEM_LIMIT)


def _dot(a, b):
    return jnp.dot(a, b, preferred_element_type=F32)


def _dot_nt(a, b):
    return lax.dot_general(a, b, (((1,), (1,)), ((), ())), preferred_element_type=F32)


def _rms_rows(x, g):
    ms = jnp.mean(x * x, axis=-1, keepdims=True)
    return x * lax.rsqrt(ms + EPS) * g


def _group_rms(acc, gain):
    outs = []
    for c in range(acc.shape[1] // LANE):
        a = acc[:, c * LANE:(c + 1) * LANE]
        outs.append(_rms_rows(a, gain[:, c * LANE:(c + 1) * LANE]))
    return outs[0] if len(outs) == 1 else jnp.concatenate(outs, axis=1)


def _norm_matmul_kernel(x_ref, g_ref, *refs, n_w, n_aux, epilogue, out_dtype):
    w_refs = refs[:n_w]
    aux_refs = refs[n_w:n_w + n_aux]
    o_ref = refs[n_w + n_aux]
    h_scr = refs[n_w + n_aux + 1]

    @pl.when(pl.program_id(1) == 0)
    def _():
        h_scr[...] = _rms_rows(x_ref[...], g_ref[...]).astype(BF16)

    h = h_scr[...]
    accs = [_dot(h, w_ref[...].astype(BF16)) for w_ref in w_refs]
    o_ref[...] = epilogue(accs, aux_refs + (x_ref,)).astype(out_dtype)


def _w_spec(w, k, tn, col_of):
    if isinstance(w, tuple):
        layer = w[1]
        return pl.BlockSpec((None, k, tn), lambda i, j: (layer, 0, col_of(j)))
    return pl.BlockSpec((k, tn), lambda i, j: (0, col_of(j)))


def _w_array(w):
    return w[0] if isinstance(w, tuple) else w


def _norm_matmul(x, gain, w, col_offsets, *, tm, tn, nj, epilogue, out_dtype,
                 aux=(), aux_specs=()):
    s, k = x.shape
    in_specs = [pl.BlockSpec((tm, k), lambda i, j: (i, 0)),
                pl.BlockSpec((1, k), lambda i, j: (0, 0))]
    for c0 in col_offsets:
        in_specs.append(_w_spec(w, k, tn, lambda j, c0=c0: c0 + j))
    in_specs += list(aux_specs)
    kern = functools.partial(_norm_matmul_kernel, n_w=len(col_offsets), n_aux=len(aux),
                             epilogue=epilogue, out_dtype=out_dtype)
    return pl.pallas_call(
        kern,
        grid=(s // tm, nj),
        in_specs=in_specs,
        out_specs=pl.BlockSpec((tm, tn), lambda i, j: (i, j)),
        out_shape=jax.ShapeDtypeStruct((s, nj * tn), out_dtype),
        scratch_shapes=[pltpu.VMEM((tm, k), BF16)],
        compiler_params=_cparams(("arbitrary", "arbitrary")),
    )(x, gain.reshape(1, k), *([_w_array(w)] * len(col_offsets)), *aux)


def _norm_matmul_pair_kernel(x_ref, g_ref, wa_ref, wb_ref, aux_ref, oa_ref, ob_ref, h_scr, *, nja, ep_a, ep_b):
    j = pl.program_id(1)

    @pl.when(j == 0)
    def _():
        h_scr[...] = _rms_rows(x_ref[...], g_ref[...]).astype(BF16)

    @pl.when(j < nja)
    def _():
        oa_ref[...] = ep_a(_dot(h_scr[...], wa_ref[...].astype(BF16)), aux_ref, j).astype(oa_ref.dtype)

    @pl.when(j >= nja)
    def _():
        ob_ref[...] = ep_b(_dot(h_scr[...], wb_ref[...].astype(BF16)), aux_ref, j - nja).astype(ob_ref.dtype)


def _norm_matmul_pair(x, gain, wa, wb, aux, aux_spec, *, tm, tna, nja, col_a, tnb, njb, col_b, ep_a, ep_b,
                      dtype_a, dtype_b, transpose_b=False):
    s, k = x.shape
    if transpose_b:
        b_spec = pl.BlockSpec((tnb, tm), lambda i, j: (jnp.maximum(j - nja, 0), i))
        b_shape = (njb * tnb, s)
    else:
        b_spec = pl.BlockSpec((tm, tnb), lambda i, j: (i, jnp.maximum(j - nja, 0)))
        b_shape = (s, njb * tnb)
    kern = functools.partial(_norm_matmul_pair_kernel, nja=nja, ep_a=ep_a, ep_b=ep_b)
    return pl.pallas_call(
        kern,
        grid=(s // tm, nja + njb),
        in_specs=[pl.BlockSpec((tm, k), lambda i, j: (i, 0)),
                  pl.BlockSpec((1, k), lambda i, j: (0, 0)),
                  _w_spec(wa, k, tna, lambda j: col_a + jnp.minimum(j, nja - 1)),
                  _w_spec(wb, k, tnb, lambda j: col_b + jnp.maximum(j - nja, 0)),
                  aux_spec],
        out_specs=[pl.BlockSpec((tm, tna), lambda i, j: (i, jnp.minimum(j, nja - 1))), b_spec],
        out_shape=[jax.ShapeDtypeStruct((s, nja * tna), dtype_a), jax.ShapeDtypeStruct(b_shape, dtype_b)],
        scratch_shapes=[pltpu.VMEM((tm, k), BF16)],
        compiler_params=_cparams(("arbitrary", "arbitrary")),
    )(x, gain.reshape(1, k), _w_array(wa), _w_array(wb), aux)


def _ep_swiglu(accs, aux):
    g, u = accs
    return g * jax.nn.sigmoid(g) * u


def _ep_ple(accs, aux):
    p_ref, wp_ref, x_ref = aux
    tn = accs[0].shape[1]
    col0 = pl.multiple_of(pl.program_id(1) * tn, tn)
    pp = _dot(p_ref[...].astype(BF16), wp_ref[...].astype(BF16))
    return x_ref[:, pl.ds(col0, tn)] + pp * jax.nn.sigmoid(accs[0])


def _ep_identity(acc, aux_ref, j):
    return acc


def _ep_gates_t(acc, aux_ref, j):
    return jax.nn.sigmoid(acc).T


def _ep_q(acc, gain_ref, j):
    return _group_rms(acc, gain_ref[...])


def _ep_kv(acc, gain_ref, j):
    normed = _group_rms(acc, gain_ref[0:1, :])
    return jnp.where((j % 2) == 0, normed, acc)


def _matmul_res_kernel(a_ref, w_ref, x_ref, o_ref):
    o_ref[...] = x_ref[...] + _dot(a_ref[...], w_ref[...].astype(BF16))


def _matmul_res(a, w, resid, *, tm, tn):
    s, k = a.shape
    n = _w_array(w).shape[-1]
    return pl.pallas_call(
        _matmul_res_kernel,
        grid=(s // tm, n // tn),
        in_specs=[pl.BlockSpec((tm, k), lambda i, j: (i, 0)),
                  _w_spec(w, k, tn, lambda j: j),
                  pl.BlockSpec((tm, tn), lambda i, j: (i, j))],
        out_specs=pl.BlockSpec((tm, tn), lambda i, j: (i, j)),
        out_shape=jax.ShapeDtypeStruct((s, n), F32),
        compiler_params=_cparams(("arbitrary", "arbitrary")),
    )(a, _w_array(w), resid)


GMLP_TN = 512
PROJ_TN = 1024


def _gmlp_kernel(x_ref, g_ref, win_ref, nv_ref, ws_ref, bs_ref, wo_ref, o_ref, h_scr, z_scr, y_scr, *, tm):
    width = GMLP_GROUPS * LANE
    n_in = 2 * width // GMLP_TN
    per_tile = GMLP_TN // LANE
    n_chunk = tm // GMLP_CHUNK
    j = pl.program_id(1)

    @pl.when(j == 0)
    def _():
        h_scr[...] = _rms_rows(x_ref[...], g_ref[...]).astype(BF16)

    @pl.when(j < n_in)
    def _():
        z_scr[j] = jax.nn.gelu(_dot(h_scr[...], win_ref[...].astype(BF16)))

    @pl.when(j == n_in)
    def _():
        v_tiles = range(n_in // 2, n_in)
        ms = sum(jnp.sum(z_scr[t] * z_scr[t], axis=-1, keepdims=True) for t in v_tiles) / width
        inv = lax.rsqrt(ms + EPS)
        row = lax.broadcasted_iota(jnp.int32, (GMLP_CHUNK, GMLP_CHUNK), 0)
        col = lax.broadcasted_iota(jnp.int32, (GMLP_CHUNK, GMLP_CHUNK), 1)
        causal = col <= row
        for g in range(GMLP_GROUPS):
            t, ls = g // per_tile, slice((g % per_tile) * LANE, (g % per_tile + 1) * LANE)
            cs = slice(g * LANE, (g + 1) * LANE)
            vn = (z_scr[n_in // 2 + t, :, ls] * inv * nv_ref[:, cs]).astype(BF16)
            ws = jnp.where(causal, ws_ref[g], 0.0).astype(BF16)
            vg = jnp.concatenate(
                [vn[c * GMLP_CHUNK:(c + 1) * GMLP_CHUNK, :] for c in range(n_chunk)], axis=1)
            sv = _dot(ws, vg) + bs_ref[:, g:g + 1]
            for c in range(n_chunk):
                rs = slice(c * GMLP_CHUNK, (c + 1) * GMLP_CHUNK)
                y_scr[rs, cs] = (z_scr[t, rs, ls] * sv[:, c * LANE:(c + 1) * LANE]).astype(BF16)

    @pl.when(j >= n_in)
    def _():
        col0 = pl.multiple_of((j - n_in) * GMLP_TN, GMLP_TN)
        o_ref[...] = x_ref[:, pl.ds(col0, GMLP_TN)] + _dot(y_scr[...], wo_ref[...].astype(BF16))


def _gmlp(x, gain, w_in, norm_v, w_s, b_s_t, w_out, layer, *, tm):
    s, d = x.shape
    width = GMLP_GROUPS * LANE
    n_in = 2 * width // GMLP_TN
    n_out = d // GMLP_TN
    return pl.pallas_call(
        functools.partial(_gmlp_kernel, tm=tm),
        grid=(s // tm, n_in + n_out),
        in_specs=[pl.BlockSpec((tm, d), lambda i, j: (i, 0), pipeline_mode=pl.Buffered(1)),
                  pl.BlockSpec((1, d), lambda i, j: (0, 0)),
                  pl.BlockSpec((None, d, GMLP_TN), lambda i, j: (layer, 0, jnp.minimum(j, n_in - 1))),
                  pl.BlockSpec((1, width), lambda i, j: (0, 0)),
                  pl.BlockSpec((None, GMLP_GROUPS, GMLP_CHUNK, GMLP_CHUNK), lambda i, j: (layer, 0, 0, 0)),
                  pl.BlockSpec((GMLP_CHUNK, GMLP_GROUPS), lambda i, j: (0, 0)),
                  pl.BlockSpec((None, width, GMLP_TN), lambda i, j: (layer, 0, jnp.maximum(j - n_in, 0)))],
        out_specs=pl.BlockSpec((tm, GMLP_TN), lambda i, j: (i, jnp.maximum(j - n_in, 0))),
        out_shape=jax.ShapeDtypeStruct((s, d), F32),
        scratch_shapes=[pltpu.VMEM((tm, d), BF16),
                        pltpu.VMEM((n_in, tm, GMLP_TN), F32),
                        pltpu.VMEM((tm, width), BF16)],
        compiler_params=_cparams(("arbitrary", "arbitrary")),
    )(x, gain.reshape(1, d), w_in, norm_v.reshape(1, width), w_s, b_s_t, w_out)


def _compress_kernel(kv_ref, pe_ref, w1_ref, w2_ref, kn_ref, o_ref, *, n_half):
    half_w = CMP_STRIDE * HEAD_DIM
    hh = jnp.concatenate([kv_ref[pl.ds(r, n_half, stride=CMP_STRIDE), :] for r in range(CMP_STRIDE)],
                         axis=1)
    pe = pe_ref[...]
    a = _dot((hh + pe[0:1, :]).astype(BF16), w1_ref[0:half_w, :].astype(BF16))
    b = _dot((hh + pe[1:2, :]).astype(BF16), w1_ref[half_w:2 * half_w, :].astype(BF16))
    pre = a + pltpu.roll(b, n_half - 1, 0)
    out = _dot(jax.nn.gelu(pre).astype(BF16), w2_ref[...].astype(BF16))
    is_key = pl.program_id(0) < N_KV_GROUPS
    out = jnp.where(is_key, _rms_rows(out, kn_ref[...]), out)
    row = lax.broadcasted_iota(jnp.int32, out.shape, 0)
    o_ref[...] = jnp.where(row < n_half - 1, out, 0.0)


def _compress(kvc, pe, w1, w2, k_norm0):
    s = kvc.shape[0]
    n_half = s // CMP_STRIDE
    half_w = CMP_STRIDE * HEAD_DIM
    hid = w1.shape[2]
    return pl.pallas_call(
        functools.partial(_compress_kernel, n_half=n_half),
        grid=(2 * N_KV_GROUPS,),
        in_specs=[pl.BlockSpec((s, HEAD_DIM), lambda n: (0, n)),
                  pl.BlockSpec((None, 2, half_w), lambda n: (n // N_KV_GROUPS, 0, 0)),
                  pl.BlockSpec((None, 2 * half_w, hid), lambda n: (n // N_KV_GROUPS, 0, 0)),
                  pl.BlockSpec((None, hid, HEAD_DIM), lambda n: (n // N_KV_GROUPS, 0, 0)),
                  pl.BlockSpec((1, HEAD_DIM), lambda n: (0, 0))],
        out_specs=pl.BlockSpec((None, n_half, HEAD_DIM), lambda n: (n, 0, 0)),
        out_shape=jax.ShapeDtypeStruct((2 * N_KV_GROUPS, n_half, HEAD_DIM), F32),
        compiler_params=_cparams(("arbitrary",)),
    )(kvc, pe, w1, w2, k_norm0.reshape(1, HEAD_DIM))


def _t5_bucket_np(dist):
    n = np.maximum(dist, 0)
    max_exact = N_BUCKETS // 2
    nf = np.maximum(n, 1).astype(np.float32)
    large = max_exact + (np.log(nf / np.float32(max_exact)) / np.float32(math.log(MAX_DISTANCE / max_exact))
                         * np.float32(N_BUCKETS - max_exact)).astype(np.int32)
    large = np.minimum(large, N_BUCKETS - 1)
    return np.where(n < max_exact, n, large).astype(np.int32)


N_PATTERNS = 4


def _bucket_patterns():
    i = np.arange(Q_BLOCK)[:, None]
    c = np.arange(LANE)[None, :]
    d0 = i - c
    d1 = i - c + Q_BLOCK
    dc = i - CMP_STRIDE * (c - (LANE - 8)) - (CMP_BLOCK - 1)
    pats = [np.where(d >= 0, _t5_bucket_np(d), -1) for d in (d0, d1, dc)]
    pats.append(np.where(i < c, N_BUCKETS - 1, -1))
    return np.stack([p.T for p in pats]).astype(np.int32)


def _bias_tiles_kernel(tab_ref, pat_ref, o_ref):
    h = pl.program_id(0)
    pat = pat_ref[...]
    far = tab_ref[N_BUCKETS - 1, h]
    acc = jnp.full(pat.shape, NEG, F32)
    for b in range(N_BUCKETS):
        acc = jnp.where(pat == b, (tab_ref[b, h] - far) * LOG2E, acc)
    o_ref[...] = acc


def _bias_tiles(rel_bias):
    pats = jnp.asarray(_bucket_patterns())
    return pl.pallas_call(
        _bias_tiles_kernel,
        grid=(N_HEADS,),
        in_specs=[pl.BlockSpec(memory_space=pltpu.SMEM),
                  pl.BlockSpec((N_PATTERNS, Q_BLOCK, LANE), lambda h: (0, 0, 0))],
        out_specs=pl.BlockSpec((None, N_PATTERNS, Q_BLOCK, LANE), lambda h: (h, 0, 0, 0)),
        out_shape=jax.ShapeDtypeStruct((N_HEADS, N_PATTERNS, Q_BLOCK, LANE), F32),
        compiler_params=_cparams(("arbitrary",)),
    )(rel_bias, pats)


def _with_features(qs, feat):
    reps = qs.shape[0] // feat.shape[0]
    return jnp.concatenate([qs, jnp.concatenate([feat] * reps, axis=0)], axis=1)


def _softmax_cols(blocks):
    m = blocks[0].max(axis=0, keepdims=True)
    for b in blocks[1:]:
        m = jnp.maximum(m, b.max(axis=0, keepdims=True))
    es = [jnp.exp2(b - m) for b in blocks]
    den = es[0].sum(axis=0, keepdims=True)
    for e in es[1:]:
        den = den + e.sum(axis=0, keepdims=True)
    return es, m, den


def _nsa_kernel(q_ref, gt_ref, kc_ref, vc_ref, vct_ref, ovl_ref, ovlt_ref, ks_ref, vsf_ref, vsn_ref,
                kw_ref, vwn_ref, tb_ref, o_ref, m_scr, acc_scr, sa_scr, sb_scr, cmp_o_scr, cmp_pf_scr, cmp_pn_scr,
                *, n_slc):
    qb = pl.program_id(1)
    s = qb * Q_BLOCK
    hpg = HEADS_PER_GROUP

    q_all = q_ref[...]
    qs = jnp.concatenate([q_all[:, h * LANE:(h + 1) * LANE] for h in range(hpg)], axis=0)
    cols = [slice(h * Q_BLOCK, (h + 1) * Q_BLOCK) for h in range(hpg)]
    lane_f = lax.broadcasted_iota(jnp.int32, (Q_BLOCK, LANE), 1)
    pad_feat = jnp.where(lane_f == LANE - 1, 1.0, 0.0).astype(BF16)
    q_pad = _with_features(qs, pad_feat)

    n_sub = NEAR // LANE
    row0 = pl.multiple_of(s, Q_BLOCK)

    n_pair = hpg // 2
    pair_rows = [slice(pr * 2 * Q_BLOCK, (pr + 1) * 2 * Q_BLOCK) for pr in range(n_pair)]
    half = [slice(0, Q_BLOCK), slice(Q_BLOCK, 2 * Q_BLOCK)]

    def run_stages(stages):
        pending = stages[0][0]()
        for i, (_, consume) in enumerate(stages):
            cur = pending
            if i + 1 < len(stages):
                pending = stages[i + 1][0]()
            consume(cur)

    def near_stages(k_ref, q_aug, vn_ref, oldest_pat, res):
        k_aug = k_ref[pl.ds(row0, NEAR), :]
        v_blocks = vn_ref[pl.ds(qb, n_sub)]
        v_t = jnp.concatenate([v_blocks[u] for u in range(n_sub)], axis=1)

        def stage(pr):
            def issue():
                return _dot_nt(k_aug, q_aug[pair_rows[pr], :])

            def consume(logits):
                ps = []
                for hh in range(2):
                    h = 2 * pr + hh
                    blocks = [logits[u * LANE:(u + 1) * LANE, half[hh]] for u in range(n_sub)]
                    blocks[n_sub - 1] = blocks[n_sub - 1] + tb_ref[h, 0]
                    blocks[n_sub - 2] = blocks[n_sub - 2] + tb_ref[h, 1]
                    if oldest_pat is not None:
                        blocks[0] = blocks[0] + tb_ref[h, oldest_pat]
                    m = blocks[0].max(axis=0, keepdims=True)
                    for b in blocks[1:]:
                        m = jnp.maximum(m, b.max(axis=0, keepdims=True))
                    ps.append(jnp.concatenate([jnp.exp2((b - m).astype(BF16)) for b in blocks], axis=0))
                    res["m"].append(m)
                res["o"].append(_dot(v_t, jnp.concatenate(ps, axis=1)))
            return issue, consume
        return [stage(pr) for pr in range(n_pair)]

    near0 = pl.multiple_of(qb * 8 + 8, 8)
    kn = kc_ref[pl.ds(near0, LANE), :].astype(BF16)
    vn_t = vc_ref[pl.ds(near0, LANE), :].T.astype(BF16)
    ovl_n_t = ovl_ref[pl.ds(near0, LANE), :].T.astype(BF16)
    far_feat = jnp.where(lane_f > qb - 16, 1.0, 0.0).astype(BF16)
    q_cmp = _with_features(qs, far_feat)
    t_row = s + lax.broadcasted_iota(jnp.int32, (1, Q_BLOCK), 1)
    row_ok = t_row >= CMP_BLOCK - 1
    n_half = kc_ref.shape[0] - CMP_PAD

    def cmp_branch(n_keys):
        kc = kc_ref[CMP_PAD:CMP_PAD + n_keys, :].astype(BF16)
        vc_t = vct_ref[:, CMP_PAD:CMP_PAD + n_keys]
        res = {"o": [], "pf": jnp.zeros((n_keys, Q_BLOCK), F32), "pn": jnp.zeros((LANE, Q_BLOCK), F32)}

        def stage(pr):
            def issue():
                return _dot_nt(kc, q_cmp[pair_rows[pr], :]), _dot_nt(kn, q_pad[pair_rows[pr], :])

            def consume(logits):
                sf, sn = logits
                pfs, pns = [], []
                for hh in range(2):
                    (ef, en), _, den = _softmax_cols([sf[:, half[hh]], sn[:, half[hh]] + tb_ref[2 * pr + hh, 2]])
                    inv = jnp.where(row_ok, 1.0 / den, 0.0)
                    pf = ef * inv
                    pn = en * inv
                    res["pf"] = res["pf"] + pf
                    res["pn"] = res["pn"] + pn
                    pfs.append(pf.astype(BF16))
                    pns.append(pn.astype(BF16))
                res["o"].append(_dot(vc_t, jnp.concatenate(pfs, axis=1)) + _dot(vn_t, jnp.concatenate(pns, axis=1)))
            return issue, consume

        run_stages([stage(pr) for pr in range(n_pair)])
        cmp_o_scr[...] = jnp.concatenate(res["o"], axis=1)
        cmp_pf_scr[0:n_keys, :] = res["pf"]
        if n_keys < n_half:
            cmp_pf_scr[n_keys:, :] = jnp.zeros((n_half - n_keys, Q_BLOCK), F32)
        cmp_pn_scr[...] = res["pn"]

    sizes = sorted({min(n_half, c) for c in (LANE, 2 * LANE)} | {n_half})
    lo = 0
    for idx, n_keys in enumerate(sizes):
        last = idx + 1 == len(sizes)
        hi = (n_keys + LANE - 8) // 8 + 1
        pl.when((qb >= lo) if last else ((qb >= lo) & (qb < hi)))(functools.partial(cmp_branch, n_keys))
        lo = hi

    blk = lax.broadcasted_iota(jnp.int32, (n_slc, Q_BLOCK), 0)
    blk_f = blk.astype(F32)
    cur = (s + lax.broadcasted_iota(jnp.int32, (n_slc, Q_BLOCK), 1)) >> 6
    forced = (blk == 0) | (blk == cur) | (blk == cur - 1)
    valid = blk <= cur
    topk = {}

    def importance_stage():
        def consume(_):
            psum_f, psum_n = cmp_pf_scr[...], cmp_pn_scr[...]
            ovl_t = ovlt_ref[:, CMP_PAD:]
            pf_hi = psum_f.astype(BF16)
            pf_lo = (psum_f - pf_hi.astype(F32)).astype(BF16)
            pn_hi = psum_n.astype(BF16)
            pn_lo = (psum_n - pn_hi.astype(F32)).astype(BF16)
            imp_t = _dot(ovl_t, pf_hi) + _dot(ovl_t, pf_lo) + _dot(ovl_n_t, pn_hi) + _dot(ovl_n_t, pn_lo)
            topk["score"] = jnp.where(valid & ~forced, imp_t, -BIG)
            topk["sel"] = jnp.where(forced, 1.0, 0.0)
        return (lambda: None), consume

    def topk_stage(rounds):
        def consume(_):
            score, sel_t = topk["score"], topk["sel"]
            for _r in range(rounds):
                top = jnp.max(score, axis=0, keepdims=True)
                first = jnp.min(jnp.where(score == top, blk_f, float(n_slc)), axis=0, keepdims=True)
                pick = blk_f == first
                sel_t = jnp.where(pick, 1.0, sel_t)
                score = jnp.where(pick, -2.0 * BIG, score)
            topk["score"], topk["sel"] = score, sel_t
        return (lambda: None), consume

    win_res = {"o": [], "m": []}
    win_stages = near_stages(kw_ref, q_pad, vwn_ref, 3, win_res)
    free_picks = max(min(N_SELECT, n_slc) - 3, 0)
    rounds = [free_picks // n_pair + (1 if pr < free_picks % n_pair else 0) for pr in range(n_pair)]
    stages = [importance_stage()]
    for pr in range(n_pair):
        stages += [win_stages[pr], topk_stage(rounds[pr])]
    run_stages(stages)
    o_win = jnp.concatenate(win_res["o"], axis=1)
    o_cmp = cmp_o_scr[...]
    sel_t = topk["sel"]
    drop_t = jnp.where(valid, 1.0 - sel_t, 1.0)
    near_blk = (s - WINDOW) >> 6
    drop_far_t = jnp.where(blk >= near_blk, 1.0, drop_t)
    q_near = _with_features(qs, drop_t.T.astype(BF16))
    q_far = _with_features(qs, drop_far_t.T.astype(BF16))

    def far_logits(t):
        r0 = pl.multiple_of(WINDOW + KV_TILE * t, KV_TILE)
        return _dot_nt(ks_ref[pl.ds(r0, KV_TILE), :], q_far)

    sa_scr[...] = far_logits(0)
    slc_res = {"o": [], "m": []}
    run_stages(near_stages(ks_ref, q_near, vsn_ref, None, slc_res))
    m_scr[...] = jnp.concatenate(slc_res["m"], axis=1)
    acc_scr[...] = jnp.concatenate(slc_res["o"], axis=1)

    gates = gt_ref[...]
    g_c, g_s, g_w = [jnp.concatenate([gates[3 * h + br:3 * h + br + 1, :] for h in range(hpg)], axis=1)
                     for br in range(N_BRANCH)]
    out_cw = g_c * o_cmp + (g_w / o_win[HEAD_DIM:HEAD_DIM + 1, :]) * o_win[:HEAD_DIM, :]

    n_far = (jnp.maximum(s - WINDOW, 0) + KV_TILE - 1) // KV_TILE

    n_tiles = vsf_ref.shape[0]

    def far_probs(st_ref):
        m_old = m_scr[...]
        sts = [st_ref[:, cols[h]] for h in range(hpg)]
        m_new = jnp.concatenate(
            [jnp.maximum(m_old[:, cols[h]], sts[h].max(axis=0, keepdims=True)) for h in range(hpg)], axis=1)
        p_t = jnp.concatenate(
            [jnp.exp2((sts[h] - m_new[:, cols[h]]).astype(BF16)) for h in range(hpg)], axis=1)
        m_scr[...] = m_new
        return p_t, jnp.exp2(m_old - m_new)

    def far_accumulate(t, p_t, alpha):
        acc_scr[...] = alpha * acc_scr[...] + _dot(vsf_ref[t], p_t)

    def far_body(i, carry):
        t0 = 2 * i
        sb_scr[...] = far_logits(t0 + 1)
        p_a, alpha_a = far_probs(sa_scr)
        sa_scr[...] = far_logits(jnp.minimum(t0 + 2, n_tiles - 1))
        far_accumulate(t0, p_a, alpha_a)
        p_b, alpha_b = far_probs(sb_scr)
        far_accumulate(t0 + 1, p_b, alpha_b)
        return carry

    lax.fori_loop(0, n_far // 2, far_body, 0)

    @pl.when(n_far % 2 == 1)
    def _():
        p_last, alpha_last = far_probs(sa_scr)
        far_accumulate(n_far - 1, p_last, alpha_last)

    out_t = out_cw + (g_s / acc_scr[HEAD_DIM:HEAD_DIM + 1, :]) * acc_scr[:HEAD_DIM, :]
    for h in range(hpg):
        o_ref[:, cols[h]] = out_t[:, cols[h]].T.astype(o_ref.dtype)


def _nsa_attention(q, gates_t, k_cmp, v_cmp, v_cmp_t, ovl, ovl_t, ks, vs_far, vs_near, kw, vw_near, bias_tiles):
    s = q.shape[0]
    n_qb = s // Q_BLOCK
    n_slc = s // SLC_BLOCK
    gw = HEADS_PER_GROUP * HEAD_DIM
    lanes = HEADS_PER_GROUP * Q_BLOCK

    def group_spec(a):
        zeros = (0,) * (a.ndim - 1)
        return pl.BlockSpec((None,) + a.shape[1:], lambda g, i: (g,) + zeros)

    def whole_spec(a):
        zeros = (0,) * a.ndim
        return pl.BlockSpec(a.shape, lambda g, i: zeros)

    return pl.pallas_call(
        functools.partial(_nsa_kernel, n_slc=n_slc),
        grid=(N_KV_GROUPS, n_qb),
        in_specs=[pl.BlockSpec((Q_BLOCK, gw), lambda g, i: (i, g)),
                  pl.BlockSpec((LANE, Q_BLOCK), lambda g, i: (g, i)),
                  group_spec(k_cmp), group_spec(v_cmp), group_spec(v_cmp_t),
                  whole_spec(ovl), whole_spec(ovl_t),
                  group_spec(ks), group_spec(vs_far), group_spec(vs_near),
                  group_spec(kw), group_spec(vw_near),
                  pl.BlockSpec((HEADS_PER_GROUP, N_PATTERNS, LANE, Q_BLOCK), lambda g, i: (g, 0, 0, 0))],
        out_specs=pl.BlockSpec((Q_BLOCK, gw), lambda g, i: (i, g)),
        out_shape=jax.ShapeDtypeStruct((s, N_KV_GROUPS * gw), BF16),
        scratch_shapes=[pltpu.VMEM((1, lanes), F32),
                        pltpu.VMEM((HEAD_DIM + SUM_ROWS, lanes), F32),
                        pltpu.VMEM((KV_TILE, lanes), F32),
                        pltpu.VMEM((KV_TILE, lanes), F32),
                        pltpu.VMEM((HEAD_DIM, lanes), F32),
                        pltpu.VMEM((k_cmp.shape[1] - CMP_PAD, Q_BLOCK), F32),
                        pltpu.VMEM((LANE, Q_BLOCK), F32)],
        compiler_params=_cparams(("arbitrary", "arbitrary")),
    )(q, gates_t, k_cmp, v_cmp, v_cmp_t, ovl, ovl_t, ks, vs_far, vs_near, kw, vw_near, bias_tiles)


def _cmp_mask_columns(n_half):
    assert n_half // 8 < LANE - 1
    out = np.zeros((CMP_PAD + n_half, LANE), np.float32)
    out[np.arange(CMP_PAD), LANE - 1] = NEG
    k = np.arange(n_half)
    out[CMP_PAD + k, k // 8] = NEG
    return out


def _slc_mask_columns(s):
    n_slc = s // SLC_BLOCK
    assert n_slc - 1 > (WINDOW + Q_BLOCK) // SLC_BLOCK
    out = np.zeros((WINDOW + s, n_slc), np.float32)
    out[np.arange(WINDOW), n_slc - 1] = NEG
    pos = np.arange(s)
    out[WINDOW + pos, pos // SLC_BLOCK] = NEG
    return out


def _pad_mask_columns(s):
    out = np.zeros((WINDOW + s, LANE), np.float32)
    out[np.arange(WINDOW), LANE - 1] = NEG
    return out


def _overlap_padded(n_half, n_slc):
    n_cmp = n_half - 1
    c0 = np.arange(n_cmp) * CMP_STRIDE
    s0 = np.arange(n_slc) * SLC_BLOCK
    lo = np.maximum(c0[:, None], s0[None, :])
    hi = np.minimum(c0[:, None] + CMP_BLOCK, s0[None, :] + SLC_BLOCK)
    ovl = np.maximum(hi - lo, 0).astype(np.float32) / CMP_BLOCK
    out = np.zeros((CMP_PAD + n_half, n_slc), np.float32)
    out[CMP_PAD:CMP_PAD + n_cmp] = ovl
    return out


def _row_tile(s):
    return min(1024, s)


def _ffn_ple(x, p, layer, b, norm_ffn, w_in, w_out, norm_ple, ple_w, ple_gate):
    s = x.shape[0]
    tm = _row_tile(s)
    tn = 512
    nj = FFN_DIM // tn
    act = _norm_matmul(x, norm_ffn, (w_in, layer), (0, nj), tm=tm, tn=tn, nj=nj,
                       epilogue=_ep_swiglu, out_dtype=BF16)
    x = _matmul_res(act, (w_out, layer), x, tm=tm, tn=512)
    aux = (p, ple_w)
    tn_ple = PROJ_TN
    aux_specs = (pl.BlockSpec((None, None, tm, PLE_DIM), lambda i, j: (layer, b, i, 0)),
                 pl.BlockSpec((None, PLE_DIM, tn_ple), lambda i, j: (layer, 0, j)))
    return _norm_matmul(x, norm_ple, (ple_gate, layer), (0,), tm=tm, tn=tn_ple, nj=D_MODEL // tn_ple,
                        epilogue=_ep_ple, out_dtype=F32, aux=aux, aux_specs=aux_specs)


def _gmlp_layer(x, layer, norm_mix, w_in, norm_v, w_s, b_s, w_out):
    return _gmlp(x, norm_mix, w_in, norm_v, w_s, jnp.transpose(b_s), w_out, layer, tm=_row_tile(x.shape[0]))


def _shared_kv(x, kv_norm, kv_w, k_norm, cmp_pe_k, cmp_pe_v, cmp_wk1, cmp_wk2, cmp_wv1, cmp_wv2):
    s = x.shape[0]
    tm = _row_tile(s)
    gw = N_KV_GROUPS * HEAD_DIM
    gains = jnp.stack([jnp.tile(k_norm[1], N_KV_GROUPS), jnp.ones((gw,), F32),
                       jnp.tile(k_norm[2], N_KV_GROUPS), jnp.ones((gw,), F32)])
    gains = jnp.broadcast_to(gains[:, None, :], (4, 8, gw))
    kvc, kvr = _norm_matmul_pair(
        x, kv_norm, kv_w, kv_w, gains, pl.BlockSpec((None, 8, gw), lambda i, j: (jnp.maximum(j - 1, 0), 0, 0)),
        tm=tm, tna=2 * gw, nja=1, col_a=0, tnb=gw, njb=4, col_b=2,
        ep_a=_ep_identity, ep_b=_ep_kv, dtype_a=F32, dtype_b=BF16)
    n_half = s // CMP_STRIDE
    pe = jnp.stack([cmp_pe_k, cmp_pe_v]).reshape(2, 2, CMP_STRIDE * HEAD_DIM)
    kv_cmp = _compress(kvc, pe, jnp.stack([cmp_wk1, cmp_wv1]), jnp.stack([cmp_wk2, cmp_wv2]), k_norm[0])
    kv_cmp = jnp.pad(kv_cmp, ((0, 0), (CMP_PAD, 0), (0, 0)))
    cmp_cols = jnp.broadcast_to(jnp.asarray(_cmp_mask_columns(n_half)), (N_KV_GROUPS, CMP_PAD + n_half, LANE))
    k_cmp = jnp.concatenate([kv_cmp[:N_KV_GROUPS], cmp_cols], axis=2)
    v_cmp = kv_cmp[N_KV_GROUPS:]
    v_cmp_t = jnp.transpose(v_cmp, (0, 2, 1)).astype(BF16)
    s_pad = WINDOW + s
    kvr = jnp.pad(kvr, ((WINDOW, 0), (0, 0))).reshape(s_pad, 4, N_KV_GROUPS, HEAD_DIM)
    kvr = jnp.transpose(kvr, (1, 2, 0, 3))
    slc_cols = jnp.asarray(_slc_mask_columns(s)).astype(BF16)
    pad_cols = jnp.asarray(_pad_mask_columns(s)).astype(BF16)
    ks = jnp.concatenate([kvr[0], jnp.broadcast_to(slc_cols, (N_KV_GROUPS,) + slc_cols.shape)], axis=2)
    kw = jnp.concatenate([kvr[2], jnp.broadcast_to(pad_cols, (N_KV_GROUPS,) + pad_cols.shape)], axis=2)

    def key_tiles_t(v, tile):
        v_t = jnp.transpose(v.reshape(N_KV_GROUPS, -1, tile, HEAD_DIM), (0, 1, 3, 2))
        return jnp.concatenate([v_t, jnp.ones(v_t.shape[:2] + (SUM_ROWS, tile), v_t.dtype)], axis=2)

    vs_far = key_tiles_t(kvr[1][:, WINDOW:], KV_TILE)
    return k_cmp, v_cmp, v_cmp_t, ks, vs_far, key_tiles_t(kvr[1], LANE), kw, key_tiles_t(kvr[3], LANE)


def _nsa_layer(x, layer, norm_mix, w_in, q_norm, w_out, kvs):
    s = x.shape[0]
    tm = _row_tile(s)
    nq = N_HEADS * HEAD_DIM
    scale = HEAD_DIM ** -0.5 * LOG2E
    q_gain = jnp.tile(q_norm * scale, PROJ_TN // HEAD_DIM).reshape(1, PROJ_TN)
    w_gate = w_in[layer, :, nq:].reshape(D_MODEL, N_KV_GROUPS, HEADS_PER_GROUP * N_BRANCH)
    w_gate = jnp.pad(w_gate, ((0, 0), (0, 0), (0, LANE - HEADS_PER_GROUP * N_BRANCH)))
    w_gate = w_gate.reshape(D_MODEL, N_KV_GROUPS * LANE)
    q, gates_t = _norm_matmul_pair(
        x, norm_mix, (w_in, layer), w_gate, q_gain, pl.BlockSpec((1, PROJ_TN), lambda i, j: (0, 0)),
        tm=tm, tna=PROJ_TN, nja=nq // PROJ_TN, col_a=0, tnb=N_KV_GROUPS * LANE, njb=1, col_b=0,
        ep_a=_ep_q, ep_b=_ep_gates_t, dtype_a=BF16, dtype_b=F32, transpose_b=True)
    o = _nsa_attention(q, gates_t, *kvs)
    return _matmul_res(o, (w_out, layer), x, tm=tm, tn=PROJ_TN)


def kernel(x, p, norm_mix, norm_ffn, norm_ple, a_w_in, a_norm_v, a_w_s, a_b_s, a_w_out, kv_norm, kv_w, k_norm, cmp_pe_k, cmp_pe_v, cmp_wk1, cmp_wk2, cmp_wv1, cmp_wv2, b_w_in, b_q_norm, b_w_out, rel_bias, ffn_w_in, ffn_w_out, ple_w, ple_gate):
    batch, s, d = x.shape
    depth = norm_mix.shape[0]
    n_a = a_w_in.shape[0]
    a_w_out, b_w_in, b_w_out, kv_w, ple_gate, ffn_w_out = (
        w.astype(BF16) for w in (a_w_out, b_w_in, b_w_out, kv_w, ple_gate, ffn_w_out))
    outs = []
    for b in range(batch):
        xb = x.reshape(s, d) if batch == 1 else x[b]
        kvs = None
        for i in range(depth):
            if i < n_a:
                xb = _gmlp_layer(xb, i, norm_mix[i], a_w_in, a_norm_v[i], a_w_s, a_b_s[i], a_w_out)
            else:
                j = i - n_a
                xb = _nsa_layer(xb, j, norm_mix[i], b_w_in, b_q_norm[j], b_w_out, kvs)
            xb = _ffn_ple(xb, p, i, b, norm_ffn[i], ffn_w_in, ffn_w_out,
                          norm_ple[i], ple_w, ple_gate)
            if i == n_a - 1:
                k_cmp, v_cmp, v_cmp_t, ks, vs_far, vs_near, kw, vw_near = _shared_kv(
                    xb, kv_norm, kv_w, k_norm, cmp_pe_k, cmp_pe_v, cmp_wk1, cmp_wk2, cmp_wv1, cmp_wv2)
                ovl = _overlap_padded(s // CMP_STRIDE, s // SLC_BLOCK)
                kvs = (k_cmp, v_cmp, v_cmp_t, jnp.asarray(ovl), jnp.asarray(ovl.T).astype(BF16),
                       ks, vs_far, vs_near, kw, vw_near, _bias_tiles(rel_bias))
        outs.append(xb)
    return outs[0].reshape(1, s, d) if batch == 1 else jnp.stack(outs)
```

```python
import functools
import math

import numpy as np
import jax
import jax.numpy as jnp
from jax import lax
from jax.experimental import pallas as pl
from jax.experimental.pallas import tpu as pltpu

F32 = jnp.float32
BF16 = jnp.bfloat16

D_MODEL = 2048
PLE_DIM = 256
FFN_DIM = 5632
GMLP_CHUNK = 128
GMLP_GROUPS = 16
HEAD_DIM = 128
N_HEADS = 16
N_KV_GROUPS = 2
HEADS_PER_GROUP = 8
N_BRANCH = 3
CMP_BLOCK = 32
CMP_STRIDE = 16
SLC_BLOCK = 64
N_SELECT = 16
WINDOW = 512
Q_BLOCK = 128
N_BUCKETS = 32
MAX_DISTANCE = 128
EPS = 1e-6
NEG = -1e30
BIG = 1e30
LOG2E = math.log2(math.e)

LANE = 128
KV_TILE = 512
NEAR = WINDOW + Q_BLOCK
CMP_PAD = 128
SUM_ROWS = 16
VMEM_LIMIT = 56 * 1024 * 1024


def _cparams(sem):
    return pltpu.CompilerParams(dimension_semantics=sem, vmem_limit_bytes=VMEM_LIMIT)


def _dot(a, b):
    return jnp.dot(a, b, preferred_element_type=F32)


def _dot_nt(a, b):
    return lax.dot_general(a, b, (((1,), (1,)), ((), ())), preferred_element_type=F32)


def _rms_rows(x, g):
    ms = jnp.mean(x * x, axis=-1, keepdims=True)
    return x * lax.rsqrt(ms + EPS) * g


def _group_rms(acc, gain):
    outs = []
    for c in range(acc.shape[1] // LANE):
        a = acc[:, c * LANE:(c + 1) * LANE]
        outs.append(_rms_rows(a, gain[:, c * LANE:(c + 1) * LANE]))
    return outs[0] if len(outs) == 1 else jnp.concatenate(outs, axis=1)


def _norm_matmul_kernel(x_ref, g_ref, *refs, n_w, n_aux, epilogue, out_dtype):
    w_refs = refs[:n_w]
    aux_refs = refs[n_w:n_w + n_aux]
    o_ref = refs[n_w + n_aux]
    h_scr = refs[n_w + n_aux + 1]

    def column_tile(h):
        accs = [_dot(h, w_ref[...].astype(BF16)) for w_ref in w_refs]
        o_ref[...] = epilogue(accs, aux_refs + (x_ref,)).astype(out_dtype)

    @pl.when(pl.program_id(1) == 0)
    def _():
        h = _rms_rows(x_ref[...], g_ref[...]).astype(BF16)
        h_scr[...] = h
        column_tile(h)

    @pl.when(pl.program_id(1) > 0)
    def _():
        column_tile(h_scr[...])


def _w_spec(w, k, tn, col_of):
    if isinstance(w, tuple):
        layer = w[1]
        return pl.BlockSpec((None, k, tn), lambda i, j: (layer, 0, col_of(j)))
    return pl.BlockSpec((k, tn), lambda i, j: (0, col_of(j)))


def _w_array(w):
    return w[0] if isinstance(w, tuple) else w


def _norm_matmul(x, gain, w, col_offsets, *, tm, tn, nj, epilogue, out_dtype,
                 aux=(), aux_specs=()):
    s, k = x.shape
    in_specs = [pl.BlockSpec((tm, k), lambda i, j: (i, 0)),
                pl.BlockSpec((1, k), lambda i, j: (0, 0))]
    for c0 in col_offsets:
        in_specs.append(_w_spec(w, k, tn, lambda j, c0=c0: c0 + j))
    in_specs += list(aux_specs)
    kern = functools.partial(_norm_matmul_kernel, n_w=len(col_offsets), n_aux=len(aux),
                             epilogue=epilogue, out_dtype=out_dtype)
    return pl.pallas_call(
        kern,
        grid=(s // tm, nj),
        in_specs=in_specs,
        out_specs=pl.BlockSpec((tm, tn), lambda i, j: (i, j)),
        out_shape=jax.ShapeDtypeStruct((s, nj * tn), out_dtype),
        scratch_shapes=[pltpu.VMEM((tm, k), BF16)],
        compiler_params=_cparams(("arbitrary", "arbitrary")),
    )(x, gain.reshape(1, k), *([_w_array(w)] * len(col_offsets)), *aux)


def _norm_matmul_pair_kernel(x_ref, g_ref, wa_ref, wb_ref, aux_ref, oa_ref, ob_ref, h_scr, *, nja, ep_a, ep_b):
    j = pl.program_id(1)

    def tile_a(h):
        oa_ref[...] = ep_a(_dot(h, wa_ref[...].astype(BF16)), aux_ref, j).astype(oa_ref.dtype)

    @pl.when(j == 0)
    def _():
        h = _rms_rows(x_ref[...], g_ref[...]).astype(BF16)
        h_scr[...] = h
        tile_a(h)

    if nja > 1:
        @pl.when((j > 0) & (j < nja))
        def _():
            tile_a(h_scr[...])

    @pl.when(j >= nja)
    def _():
        ob_ref[...] = ep_b(_dot(h_scr[...], wb_ref[...].astype(BF16)), aux_ref, j - nja).astype(ob_ref.dtype)


def _norm_matmul_pair(x, gain, wa, wb, aux, aux_spec, *, tm, tna, nja, col_a, tnb, njb, col_b, ep_a, ep_b,
                      dtype_a, dtype_b, transpose_b=False):
    s, k = x.shape
    if transpose_b:
        b_spec = pl.BlockSpec((tnb, tm), lambda i, j: (jnp.maximum(j - nja, 0), i))
        b_shape = (njb * tnb, s)
    else:
        b_spec = pl.BlockSpec((tm, tnb), lambda i, j: (i, jnp.maximum(j - nja, 0)))
        b_shape = (s, njb * tnb)
    kern = functools.partial(_norm_matmul_pair_kernel, nja=nja, ep_a=ep_a, ep_b=ep_b)
    return pl.pallas_call(
        kern,
        grid=(s // tm, nja + njb),
        in_specs=[pl.BlockSpec((tm, k), lambda i, j: (i, 0)),
                  pl.BlockSpec((1, k), lambda i, j: (0, 0)),
                  _w_spec(wa, k, tna, lambda j: col_a + jnp.minimum(j, nja - 1)),
                  _w_spec(wb, k, tnb, lambda j: col_b + jnp.maximum(j - nja, 0)),
                  aux_spec],
        out_specs=[pl.BlockSpec((tm, tna), lambda i, j: (i, jnp.minimum(j, nja - 1))), b_spec],
        out_shape=[jax.ShapeDtypeStruct((s, nja * tna), dtype_a), jax.ShapeDtypeStruct(b_shape, dtype_b)],
        scratch_shapes=[pltpu.VMEM((tm, k), BF16)],
        compiler_params=_cparams(("arbitrary", "arbitrary")),
    )(x, gain.reshape(1, k), _w_array(wa), _w_array(wb), aux)


def _ep_swiglu(accs, aux):
    g, u = accs
    return g * jax.nn.sigmoid(g) * u


def _ep_ple(accs, aux):
    p_ref, wp_ref, x_ref = aux
    tn = accs[0].shape[1]
    col0 = pl.multiple_of(pl.program_id(1) * tn, tn)
    pp = _dot(p_ref[...].astype(BF16), wp_ref[...].astype(BF16))
    return x_ref[:, pl.ds(col0, tn)] + pp * jax.nn.sigmoid(accs[0])


def _ep_identity(acc, aux_ref, j):
    return acc


def _ep_gates_t(acc, aux_ref, j):
    return jax.nn.sigmoid(acc).T


def _ep_q(acc, gain_ref, j):
    return _group_rms(acc, gain_ref[...])


def _ep_kv(acc, gain_ref, j):
    normed = _group_rms(acc, gain_ref[0:1, :])
    return jnp.where((j % 2) == 0, normed, acc)


def _matmul_res_kernel(a_ref, w_ref, x_ref, o_ref):
    o_ref[...] = x_ref[...] + _dot(a_ref[...], w_ref[...].astype(BF16))


def _matmul_res(a, w, resid, *, tm, tn):
    s, k = a.shape
    n = _w_array(w).shape[-1]
    return pl.pallas_call(
        _matmul_res_kernel,
        grid=(s // tm, n // tn),
        in_specs=[pl.BlockSpec((tm, k), lambda i, j: (i, 0)),
                  _w_spec(w, k, tn, lambda j: j),
                  pl.BlockSpec((tm, tn), lambda i, j: (i, j))],
        out_specs=pl.BlockSpec((tm, tn), lambda i, j: (i, j)),
        out_shape=jax.ShapeDtypeStruct((s, n), F32),
        compiler_params=_cparams(("arbitrary", "arbitrary")),
    )(a, _w_array(w), resid)


GMLP_TN = 512
PROJ_TN = 1024


def _gmlp_kernel(x_ref, g_ref, win_ref, nv_ref, ws_ref, bs_ref, wo_ref, o_ref, h_scr, z_scr, y_scr, *, tm):
    width = GMLP_GROUPS * LANE
    n_in = 2 * width // GMLP_TN
    per_tile = GMLP_TN // LANE
    n_chunk = tm // GMLP_CHUNK
    j = pl.program_id(1)

    def z_tile(h):
        z_scr[j] = jax.nn.gelu(_dot(h, win_ref[...].astype(BF16)))

    @pl.when(j == 0)
    def _():
        h = _rms_rows(x_ref[...], g_ref[...]).astype(BF16)
        h_scr[...] = h
        z_tile(h)

    @pl.when((j > 0) & (j < n_in))
    def _():
        z_tile(h_scr[...])

    @pl.when(j == n_in)
    def _():
        v_tiles = range(n_in // 2, n_in)
        ms = sum(jnp.sum(z_scr[t] * z_scr[t], axis=-1, keepdims=True) for t in v_tiles) / width
        inv = lax.rsqrt(ms + EPS)
        row = lax.broadcasted_iota(jnp.int32, (GMLP_CHUNK, GMLP_CHUNK), 0)
        col = lax.broadcasted_iota(jnp.int32, (GMLP_CHUNK, GMLP_CHUNK), 1)
        causal = col <= row
        for g in range(GMLP_GROUPS):
            t, ls = g // per_tile, slice((g % per_tile) * LANE, (g % per_tile + 1) * LANE)
            cs = slice(g * LANE, (g + 1) * LANE)
            vn = (z_scr[n_in // 2 + t, :, ls] * inv * nv_ref[:, cs]).astype(BF16)
            ws = jnp.where(causal, ws_ref[g], 0.0).astype(BF16)
            vg = jnp.concatenate(
                [vn[c * GMLP_CHUNK:(c + 1) * GMLP_CHUNK, :] for c in range(n_chunk)], axis=1)
            sv = _dot(ws, vg) + bs_ref[:, g:g + 1]
            for c in range(n_chunk):
                rs = slice(c * GMLP_CHUNK, (c + 1) * GMLP_CHUNK)
                y_scr[rs, cs] = (z_scr[t, rs, ls] * sv[:, c * LANE:(c + 1) * LANE]).astype(BF16)

    @pl.when(j >= n_in)
    def _():
        col0 = pl.multiple_of((j - n_in) * GMLP_TN, GMLP_TN)
        o_ref[...] = x_ref[:, pl.ds(col0, GMLP_TN)] + _dot(y_scr[...], wo_ref[...].astype(BF16))


def _gmlp(x, gain, w_in, norm_v, w_s, b_s_t, w_out, layer, *, tm):
    s, d = x.shape
    width = GMLP_GROUPS * LANE
    n_in = 2 * width // GMLP_TN
    n_out = d // GMLP_TN
    return pl.pallas_call(
        functools.partial(_gmlp_kernel, tm=tm),
        grid=(s // tm, n_in + n_out),
        in_specs=[pl.BlockSpec((tm, d), lambda i, j: (i, 0), pipeline_mode=pl.Buffered(1)),
                  pl.BlockSpec((1, d), lambda i, j: (0, 0)),
                  pl.BlockSpec((None, d, GMLP_TN), lambda i, j: (layer, 0, jnp.minimum(j, n_in - 1))),
                  pl.BlockSpec((1, width), lambda i, j: (0, 0)),
                  pl.BlockSpec((None, GMLP_GROUPS, GMLP_CHUNK, GMLP_CHUNK), lambda i, j: (layer, 0, 0, 0)),
                  pl.BlockSpec((GMLP_CHUNK, GMLP_GROUPS), lambda i, j: (0, 0)),
                  pl.BlockSpec((None, width, GMLP_TN), lambda i, j: (layer, 0, jnp.maximum(j - n_in, 0)))],
        out_specs=pl.BlockSpec((tm, GMLP_TN), lambda i, j: (i, jnp.maximum(j - n_in, 0))),
        out_shape=jax.ShapeDtypeStruct((s, d), F32),
        scratch_shapes=[pltpu.VMEM((tm, d), BF16),
                        pltpu.VMEM((n_in, tm, GMLP_TN), F32),
                        pltpu.VMEM((tm, width), BF16)],
        compiler_params=_cparams(("arbitrary", "arbitrary")),
    )(x, gain.reshape(1, d), w_in, norm_v.reshape(1, width), w_s, b_s_t, w_out)


def _compress_kernel(kv_ref, pe_ref, w1_ref, w2_ref, kn_ref, o_ref, *, n_half):
    half_w = CMP_STRIDE * HEAD_DIM
    hh = jnp.concatenate([kv_ref[pl.ds(r, n_half, stride=CMP_STRIDE), :] for r in range(CMP_STRIDE)],
                         axis=1)
    pe = pe_ref[...]
    a = _dot((hh + pe[0:1, :]).astype(BF16), w1_ref[0:half_w, :].astype(BF16))
    b = _dot((hh + pe[1:2, :]).astype(BF16), w1_ref[half_w:2 * half_w, :].astype(BF16))
    pre = a + pltpu.roll(b, n_half - 1, 0)
    out = _dot(jax.nn.gelu(pre).astype(BF16), w2_ref[...].astype(BF16))
    is_key = pl.program_id(0) < N_KV_GROUPS
    out = jnp.where(is_key, _rms_rows(out, kn_ref[...]), out)
    row = lax.broadcasted_iota(jnp.int32, out.shape, 0)
    o_ref[...] = jnp.where(row < n_half - 1, out, 0.0)


def _compress(kvc, pe, w1, w2, k_norm0):
    s = kvc.shape[0]
    n_half = s // CMP_STRIDE
    half_w = CMP_STRIDE * HEAD_DIM
    hid = w1.shape[2]
    return pl.pallas_call(
        functools.partial(_compress_kernel, n_half=n_half),
        grid=(2 * N_KV_GROUPS,),
        in_specs=[pl.BlockSpec((s, HEAD_DIM), lambda n: (0, n)),
                  pl.BlockSpec((None, 2, half_w), lambda n: (n // N_KV_GROUPS, 0, 0)),
                  pl.BlockSpec((None, 2 * half_w, hid), lambda n: (n // N_KV_GROUPS, 0, 0)),
                  pl.BlockSpec((None, hid, HEAD_DIM), lambda n: (n // N_KV_GROUPS, 0, 0)),
                  pl.BlockSpec((1, HEAD_DIM), lambda n: (0, 0))],
        out_specs=pl.BlockSpec((None, n_half, HEAD_DIM), lambda n: (n, 0, 0)),
        out_shape=jax.ShapeDtypeStruct((2 * N_KV_GROUPS, n_half, HEAD_DIM), F32),
        compiler_params=_cparams(("arbitrary",)),
    )(kvc, pe, w1, w2, k_norm0.reshape(1, HEAD_DIM))


def _t5_bucket_np(dist):
    n = np.maximum(dist, 0)
    max_exact = N_BUCKETS // 2
    nf = np.maximum(n, 1).astype(np.float32)
    large = max_exact + (np.log(nf / np.float32(max_exact)) / np.float32(math.log(MAX_DISTANCE / max_exact))
                         * np.float32(N_BUCKETS - max_exact)).astype(np.int32)
    large = np.minimum(large, N_BUCKETS - 1)
    return np.where(n < max_exact, n, large).astype(np.int32)


N_PATTERNS = 4


def _bucket_patterns():
    i = np.arange(Q_BLOCK)[:, None]
    c = np.arange(LANE)[None, :]
    d0 = i - c
    d1 = i - c + Q_BLOCK
    dc = i - CMP_STRIDE * (c - (LANE - 8)) - (CMP_BLOCK - 1)
    pats = [np.where(d >= 0, _t5_bucket_np(d), -1) for d in (d0, d1, dc)]
    pats.append(np.where(i < c, N_BUCKETS - 1, -1))
    return np.stack([p.T for p in pats]).astype(np.int32)


def _bias_tiles_kernel(tab_ref, pat_ref, o_ref):
    h = pl.program_id(0)
    pat = pat_ref[...]
    far = tab_ref[N_BUCKETS - 1, h]
    acc = jnp.full(pat.shape, NEG, F32)
    for b in range(N_BUCKETS):
        acc = jnp.where(pat == b, (tab_ref[b, h] - far) * LOG2E, acc)
    o_ref[...] = acc


def _bias_tiles(rel_bias):
    pats = jnp.asarray(_bucket_patterns())
    return pl.pallas_call(
        _bias_tiles_kernel,
        grid=(N_HEADS,),
        in_specs=[pl.BlockSpec(memory_space=pltpu.SMEM),
                  pl.BlockSpec((N_PATTERNS, Q_BLOCK, LANE), lambda h: (0, 0, 0))],
        out_specs=pl.BlockSpec((None, N_PATTERNS, Q_BLOCK, LANE), lambda h: (h, 0, 0, 0)),
        out_shape=jax.ShapeDtypeStruct((N_HEADS, N_PATTERNS, Q_BLOCK, LANE), F32),
        compiler_params=_cparams(("arbitrary",)),
    )(rel_bias, pats)


def _with_features(qs, feat):
    reps = qs.shape[0] // feat.shape[0]
    return jnp.concatenate([qs, jnp.concatenate([feat] * reps, axis=0)], axis=1)


def _softmax_cols(blocks):
    m = blocks[0].max(axis=0, keepdims=True)
    for b in blocks[1:]:
        m = jnp.maximum(m, b.max(axis=0, keepdims=True))
    es = [jnp.exp2(b - m) for b in blocks]
    den = es[0].sum(axis=0, keepdims=True)
    for e in es[1:]:
        den = den + e.sum(axis=0, keepdims=True)
    return es, m, den


def _nsa_kernel(q_ref, gt_ref, kc_ref, vc_ref, vct_ref, ovl_ref, ovlt_ref, ks_ref, vsf_ref, vsn_ref,
                kw_ref, vwn_ref, tb_ref, o_ref, m_scr, acc_scr, sa_scr, sb_scr, cmp_o_scr, cmp_pf_scr, cmp_pn_scr,
                *, n_slc):
    qb = pl.program_id(1)
    s = qb * Q_BLOCK
    hpg = HEADS_PER_GROUP

    q_all = q_ref[...]
    qs = jnp.concatenate([q_all[:, h * LANE:(h + 1) * LANE] for h in range(hpg)], axis=0)
    cols = [slice(h * Q_BLOCK, (h + 1) * Q_BLOCK) for h in range(hpg)]
    lane_f = lax.broadcasted_iota(jnp.int32, (Q_BLOCK, LANE), 1)
    pad_feat = jnp.where(lane_f == LANE - 1, 1.0, 0.0).astype(BF16)
    q_pad = _with_features(qs, pad_feat)

    n_sub = NEAR // LANE
    row0 = pl.multiple_of(s, Q_BLOCK)

    n_pair = hpg // 2
    pair_rows = [slice(pr * 2 * Q_BLOCK, (pr + 1) * 2 * Q_BLOCK) for pr in range(n_pair)]
    half = [slice(0, Q_BLOCK), slice(Q_BLOCK, 2 * Q_BLOCK)]

    def run_stages(stages):
        pending = stages[0][0]()
        for i, (_, consume) in enumerate(stages):
            cur = pending
            if i + 1 < len(stages):
                pending = stages[i + 1][0]()
            consume(cur)

    def near_stages(k_ref, q_aug, vn_ref, oldest_pat, res):
        k_aug = k_ref[pl.ds(row0, NEAR), :]
        v_blocks = vn_ref[pl.ds(qb, n_sub)]
        v_t = jnp.concatenate([v_blocks[u] for u in range(n_sub)], axis=1)

        def stage(pr):
            def issue():
                return _dot_nt(k_aug, q_aug[pair_rows[pr], :])

            def consume(logits):
                ps = []
                for hh in range(2):
                    h = 2 * pr + hh
                    blocks = [logits[u * LANE:(u + 1) * LANE, half[hh]] for u in range(n_sub)]
                    blocks[n_sub - 1] = blocks[n_sub - 1] + tb_ref[h, 0]
                    blocks[n_sub - 2] = blocks[n_sub - 2] + tb_ref[h, 1]
                    if oldest_pat is not None:
                        blocks[0] = blocks[0] + tb_ref[h, oldest_pat]
                    m = blocks[0].max(axis=0, keepdims=True)
                    for b in blocks[1:]:
                        m = jnp.maximum(m, b.max(axis=0, keepdims=True))
                    ps.append(jnp.concatenate([jnp.exp2((b - m).astype(BF16)) for b in blocks], axis=0))
                    res["m"].append(m)
                res["o"].append(_dot(v_t, jnp.concatenate(ps, axis=1)))
            return issue, consume
        return [stage(pr) for pr in range(n_pair)]

    near0 = pl.multiple_of(qb * 8 + 8, 8)
    kn = kc_ref[pl.ds(near0, LANE), :].astype(BF16)
    vn_t = vc_ref[pl.ds(near0, LANE), :].T.astype(BF16)
    ovl_n_t = ovl_ref[pl.ds(near0, LANE), :].T.astype(BF16)
    far_feat = jnp.where(lane_f > qb - 16, 1.0, 0.0).astype(BF16)
    q_cmp = _with_features(qs, far_feat)
    t_row = s + lax.broadcasted_iota(jnp.int32, (1, Q_BLOCK), 1)
    row_ok = t_row >= CMP_BLOCK - 1
    n_half = kc_ref.shape[0] - CMP_PAD

    def cmp_branch(n_keys):
        kc = kc_ref[CMP_PAD:CMP_PAD + n_keys, :].astype(BF16)
        vc_t = vct_ref[:, CMP_PAD:CMP_PAD + n_keys]
        res = {"o": [], "pf": jnp.zeros((n_keys, Q_BLOCK), F32), "pn": jnp.zeros((LANE, Q_BLOCK), F32)}
        heads = hpg * LANE // min(n_keys, hpg * LANE // 2)

        def stage(first):
            q_rows = slice(first * Q_BLOCK, (first + heads) * Q_BLOCK)

            def issue():
                return _dot_nt(kc, q_cmp[q_rows, :]), _dot_nt(kn, q_pad[q_rows, :])

            def consume(logits):
                sf, sn = logits
                pfs, pns = [], []
                for hh in range(heads):
                    (ef, en), _, den = _softmax_cols([sf[:, cols[hh]], sn[:, cols[hh]] + tb_ref[first + hh, 2]])
                    inv = jnp.where(row_ok, 1.0 / den, 0.0)
                    pf = ef * inv
                    pn = en * inv
                    res["pf"] = res["pf"] + pf
                    res["pn"] = res["pn"] + pn
                    pfs.append(pf.astype(BF16))
                    pns.append(pn.astype(BF16))
                res["o"].append(_dot(vc_t, jnp.concatenate(pfs, axis=1)) + _dot(vn_t, jnp.concatenate(pns, axis=1)))
            return issue, consume

        run_stages([stage(first) for first in range(0, hpg, heads)])
        cmp_o_scr[...] = jnp.concatenate(res["o"], axis=1)
        cmp_pf_scr[0:n_keys, :] = res["pf"]
        if n_keys < n_half:
            cmp_pf_scr[n_keys:, :] = jnp.zeros((n_half - n_keys, Q_BLOCK), F32)
        cmp_pn_scr[...] = res["pn"]

    sizes = sorted({min(n_half, c) for c in (LANE, 2 * LANE)} | {n_half})
    lo = 0
    for idx, n_keys in enumerate(sizes):
        last = idx + 1 == len(sizes)
        hi = (n_keys + LANE - 8) // 8 + 1
        pl.when((qb >= lo) if last else ((qb >= lo) & (qb < hi)))(functools.partial(cmp_branch, n_keys))
        lo = hi

    blk = lax.broadcasted_iota(jnp.int32, (n_slc, Q_BLOCK), 0)
    blk_f = blk.astype(F32)
    cur = (s + lax.broadcasted_iota(jnp.int32, (n_slc, Q_BLOCK), 1)) >> 6
    forced = (blk == 0) | (blk == cur) | (blk == cur - 1)
    valid = blk <= cur
    topk = {}

    def importance_stage():
        def consume(_):
            psum_f, psum_n = cmp_pf_scr[...], cmp_pn_scr[...]
            ovl_t = ovlt_ref[:, CMP_PAD:]
            pf_hi = psum_f.astype(BF16)
            pf_lo = (psum_f - pf_hi.astype(F32)).astype(BF16)
            pn_hi = psum_n.astype(BF16)
            pn_lo = (psum_n - pn_hi.astype(F32)).astype(BF16)
            imp_t = _dot(ovl_t, pf_hi) + _dot(ovl_t, pf_lo) + _dot(ovl_n_t, pn_hi) + _dot(ovl_n_t, pn_lo)
            topk["score"] = jnp.where(valid & ~forced, imp_t, -BIG)
            topk["sel"] = jnp.where(forced, 1.0, 0.0)
        return (lambda: None), consume

    def topk_stage(rounds):
        def consume(_):
            score, sel_t = topk["score"], topk["sel"]
            for _r in range(rounds):
                top = jnp.max(score, axis=0, keepdims=True)
                first = jnp.min(jnp.where(score == top, blk_f, float(n_slc)), axis=0, keepdims=True)
                pick = blk_f == first
                sel_t = jnp.where(pick, 1.0, sel_t)
                score = jnp.where(pick, -2.0 * BIG, score)
            topk["score"], topk["sel"] = score, sel_t
        return (lambda: None), consume

    win_res = {"o": [], "m": []}
    win_stages = near_stages(kw_ref, q_pad, vwn_ref, 3, win_res)
    free_picks = max(min(N_SELECT, n_slc) - 3, 0)
    rounds = [free_picks // n_pair + (1 if pr < free_picks % n_pair else 0) for pr in range(n_pair)]
    stages = [importance_stage()]
    for pr in range(n_pair):
        stages += [win_stages[pr], topk_stage(rounds[pr])]
    run_stages(stages)
    o_win = jnp.concatenate(win_res["o"], axis=1)
    o_cmp = cmp_o_scr[...]
    sel_t = topk["sel"]
    drop_t = jnp.where(valid, 1.0 - sel_t, 1.0)
    near_blk = (s - WINDOW) >> 6
    drop_far_t = jnp.where(blk >= near_blk, 1.0, drop_t)
    q_near = _with_features(qs, drop_t.T.astype(BF16))
    q_far = _with_features(qs, drop_far_t.T.astype(BF16))

    def far_logits(t):
        r0 = pl.multiple_of(WINDOW + KV_TILE * t, KV_TILE)
        return _dot_nt(ks_ref[pl.ds(r0, KV_TILE), :], q_far)

    sa_scr[...] = far_logits(0)
    slc_res = {"o": [], "m": []}
    run_stages(near_stages(ks_ref, q_near, vsn_ref, None, slc_res))
    m_scr[...] = jnp.concatenate(slc_res["m"], axis=1)
    acc_scr[...] = jnp.concatenate(slc_res["o"], axis=1)

    gates = gt_ref[...]
    g_c, g_s, g_w = [jnp.concatenate([gates[3 * h + br:3 * h + br + 1, :] for h in range(hpg)], axis=1)
                     for br in range(N_BRANCH)]
    out_cw = g_c * o_cmp + (g_w / o_win[HEAD_DIM:HEAD_DIM + 1, :]) * o_win[:HEAD_DIM, :]

    n_far = (jnp.maximum(s - WINDOW, 0) + KV_TILE - 1) // KV_TILE

    n_tiles = vsf_ref.shape[0]

    def far_probs(st_ref):
        m_old = m_scr[...]
        sts = [st_ref[:, cols[h]] for h in range(hpg)]
        m_new = jnp.concatenate(
            [jnp.maximum(m_old[:, cols[h]], sts[h].max(axis=0, keepdims=True)) for h in range(hpg)], axis=1)
        p_t = jnp.concatenate(
            [jnp.exp2((sts[h] - m_new[:, cols[h]]).astype(BF16)) for h in range(hpg)], axis=1)
        m_scr[...] = m_new
        return p_t, jnp.exp2(m_old - m_new)

    def far_accumulate(t, p_t, alpha):
        acc_scr[...] = alpha * acc_scr[...] + _dot(vsf_ref[t], p_t)

    def far_body(i, carry):
        t0 = 2 * i
        sb_scr[...] = far_logits(t0 + 1)
        p_a, alpha_a = far_probs(sa_scr)
        sa_scr[...] = far_logits(jnp.minimum(t0 + 2, n_tiles - 1))
        far_accumulate(t0, p_a, alpha_a)
        p_b, alpha_b = far_probs(sb_scr)
        far_accumulate(t0 + 1, p_b, alpha_b)
        return carry

    lax.fori_loop(0, n_far // 2, far_body, 0)

    @pl.when(n_far % 2 == 1)
    def _():
        p_last, alpha_last = far_probs(sa_scr)
        far_accumulate(n_far - 1, p_last, alpha_last)

    out_t = out_cw + (g_s / acc_scr[HEAD_DIM:HEAD_DIM + 1, :]) * acc_scr[:HEAD_DIM, :]
    for h in range(hpg):
        o_ref[:, cols[h]] = out_t[:, cols[h]].T.astype(o_ref.dtype)


def _nsa_attention(q, gates_t, k_cmp, v_cmp, v_cmp_t, ovl, ovl_t, ks, vs_far, vs_near, kw, vw_near, bias_tiles):
    s = q.shape[0]
    n_qb = s // Q_BLOCK
    n_slc = s // SLC_BLOCK
    gw = HEADS_PER_GROUP * HEAD_DIM
    lanes = HEADS_PER_GROUP * Q_BLOCK

    def group_spec(a):
        zeros = (0,) * (a.ndim - 1)
        return pl.BlockSpec((None,) + a.shape[1:], lambda g, i: (g,) + zeros)

    def whole_spec(a):
        zeros = (0,) * a.ndim
        return pl.BlockSpec(a.shape, lambda g, i: zeros)

    return pl.pallas_call(
        functools.partial(_nsa_kernel, n_slc=n_slc),
        grid=(N_KV_GROUPS, n_qb),
        in_specs=[pl.BlockSpec((Q_BLOCK, gw), lambda g, i: (i, g)),
                  pl.BlockSpec((LANE, Q_BLOCK), lambda g, i: (g, i)),
                  group_spec(k_cmp), group_spec(v_cmp), group_spec(v_cmp_t),
                  whole_spec(ovl), whole_spec(ovl_t),
                  group_spec(ks), group_spec(vs_far), group_spec(vs_near),
                  group_spec(kw), group_spec(vw_near),
                  pl.BlockSpec((HEADS_PER_GROUP, N_PATTERNS, LANE, Q_BLOCK), lambda g, i: (g, 0, 0, 0))],
        out_specs=pl.BlockSpec((Q_BLOCK, gw), lambda g, i: (i, g)),
        out_shape=jax.ShapeDtypeStruct((s, N_KV_GROUPS * gw), BF16),
        scratch_shapes=[pltpu.VMEM((1, lanes), F32),
                        pltpu.VMEM((HEAD_DIM + SUM_ROWS, lanes), F32),
                        pltpu.VMEM((KV_TILE, lanes), F32),
                        pltpu.VMEM((KV_TILE, lanes), F32),
                        pltpu.VMEM((HEAD_DIM, lanes), F32),
                        pltpu.VMEM((k_cmp.shape[1] - CMP_PAD, Q_BLOCK), F32),
                        pltpu.VMEM((LANE, Q_BLOCK), F32)],
        compiler_params=_cparams(("arbitrary", "arbitrary")),
    )(q, gates_t, k_cmp, v_cmp, v_cmp_t, ovl, ovl_t, ks, vs_far, vs_near, kw, vw_near, bias_tiles)


def _cmp_mask_columns(n_half):
    assert n_half // 8 < LANE - 1
    out = np.zeros((CMP_PAD + n_half, LANE), np.float32)
    out[np.arange(CMP_PAD), LANE - 1] = NEG
    k = np.arange(n_half)
    out[CMP_PAD + k, k // 8] = NEG
    return out


def _slc_mask_columns(s):
    n_slc = s // SLC_BLOCK
    assert n_slc - 1 > (WINDOW + Q_BLOCK) // SLC_BLOCK
    out = np.zeros((WINDOW + s, n_slc), np.float32)
    out[np.arange(WINDOW), n_slc - 1] = NEG
    pos = np.arange(s)
    out[WINDOW + pos, pos // SLC_BLOCK] = NEG
    return out


def _pad_mask_columns(s):
    out = np.zeros((WINDOW + s, LANE), np.float32)
    out[np.arange(WINDOW), LANE - 1] = NEG
    return out


def _overlap_padded(n_half, n_slc):
    n_cmp = n_half - 1
    c0 = np.arange(n_cmp) * CMP_STRIDE
    s0 = np.arange(n_slc) * SLC_BLOCK
    lo = np.maximum(c0[:, None], s0[None, :])
    hi = np.minimum(c0[:, None] + CMP_BLOCK, s0[None, :] + SLC_BLOCK)
    ovl = np.maximum(hi - lo, 0).astype(np.float32) / CMP_BLOCK
    out = np.zeros((CMP_PAD + n_half, n_slc), np.float32)
    out[CMP_PAD:CMP_PAD + n_cmp] = ovl
    return out


def _row_tile(s):
    return min(1024, s)


def _ffn_ple(x, p, layer, b, norm_ffn, w_in, w_out, norm_ple, ple_w, ple_gate):
    s = x.shape[0]
    tm = _row_tile(s)
    tn = 512
    nj = FFN_DIM // tn
    act = _norm_matmul(x, norm_ffn, (w_in, layer), (0, nj), tm=tm, tn=tn, nj=nj,
                       epilogue=_ep_swiglu, out_dtype=BF16)
    x = _matmul_res(act, (w_out, layer), x, tm=tm, tn=512)
    aux = (p, ple_w)
    tn_ple = PROJ_TN
    aux_specs = (pl.BlockSpec((None, None, tm, PLE_DIM), lambda i, j: (layer, b, i, 0)),
                 pl.BlockSpec((None, PLE_DIM, tn_ple), lambda i, j: (layer, 0, j)))
    return _norm_matmul(x, norm_ple, (ple_gate, layer), (0,), tm=tm, tn=tn_ple, nj=D_MODEL // tn_ple,
                        epilogue=_ep_ple, out_dtype=F32, aux=aux, aux_specs=aux_specs)


def _gmlp_layer(x, layer, norm_mix, w_in, norm_v, w_s, b_s, w_out):
    return _gmlp(x, norm_mix, w_in, norm_v, w_s, jnp.transpose(b_s), w_out, layer, tm=_row_tile(x.shape[0]))


def _shared_kv(x, kv_norm, kv_w, k_norm, cmp_pe_k, cmp_pe_v, cmp_wk1, cmp_wk2, cmp_wv1, cmp_wv2):
    s = x.shape[0]
    tm = _row_tile(s)
    gw = N_KV_GROUPS * HEAD_DIM
    gains = jnp.stack([jnp.tile(k_norm[1], N_KV_GROUPS), jnp.ones((gw,), F32),
                       jnp.tile(k_norm[2], N_KV_GROUPS), jnp.ones((gw,), F32)])
    gains = jnp.broadcast_to(gains[:, None, :], (4, 8, gw))
    kvc, kvr = _norm_matmul_pair(
        x, kv_norm, kv_w, kv_w, gains, pl.BlockSpec((None, 8, gw), lambda i, j: (jnp.maximum(j - 1, 0), 0, 0)),
        tm=tm, tna=2 * gw, nja=1, col_a=0, tnb=gw, njb=4, col_b=2,
        ep_a=_ep_identity, ep_b=_ep_kv, dtype_a=F32, dtype_b=BF16)
    n_half = s // CMP_STRIDE
    pe = jnp.stack([cmp_pe_k, cmp_pe_v]).reshape(2, 2, CMP_STRIDE * HEAD_DIM)
    kv_cmp = _compress(kvc, pe, jnp.stack([cmp_wk1, cmp_wv1]), jnp.stack([cmp_wk2, cmp_wv2]), k_norm[0])
    kv_cmp = jnp.pad(kv_cmp, ((0, 0), (CMP_PAD, 0), (0, 0)))
    cmp_cols = jnp.broadcast_to(jnp.asarray(_cmp_mask_columns(n_half)), (N_KV_GROUPS, CMP_PAD + n_half, LANE))
    k_cmp = jnp.concatenate([kv_cmp[:N_KV_GROUPS], cmp_cols], axis=2)
    v_cmp = kv_cmp[N_KV_GROUPS:]
    v_cmp_t = jnp.transpose(v_cmp, (0, 2, 1)).astype(BF16)
    s_pad = WINDOW + s
    kvr = jnp.pad(kvr, ((WINDOW, 0), (0, 0))).reshape(s_pad, 4, N_KV_GROUPS, HEAD_DIM)
    kvr = jnp.transpose(kvr, (1, 2, 0, 3))
    slc_cols = jnp.asarray(_slc_mask_columns(s)).astype(BF16)
    pad_cols = jnp.asarray(_pad_mask_columns(s)).astype(BF16)
    ks = jnp.concatenate([kvr[0], jnp.broadcast_to(slc_cols, (N_KV_GROUPS,) + slc_cols.shape)], axis=2)
    kw = jnp.concatenate([kvr[2], jnp.broadcast_to(pad_cols, (N_KV_GROUPS,) + pad_cols.shape)], axis=2)

    def key_tiles_t(v, tile):
        v_t = jnp.transpose(v.reshape(N_KV_GROUPS, -1, tile, HEAD_DIM), (0, 1, 3, 2))
        return jnp.concatenate([v_t, jnp.ones(v_t.shape[:2] + (SUM_ROWS, tile), v_t.dtype)], axis=2)

    vs_far = key_tiles_t(kvr[1][:, WINDOW:], KV_TILE)
    return k_cmp, v_cmp, v_cmp_t, ks, vs_far, key_tiles_t(kvr[1], LANE), kw, key_tiles_t(kvr[3], LANE)


def _nsa_layer(x, layer, norm_mix, w_in, q_norm, w_out, kvs):
    s = x.shape[0]
    tm = _row_tile(s)
    nq = N_HEADS * HEAD_DIM
    scale = HEAD_DIM ** -0.5 * LOG2E
    q_gain = jnp.tile(q_norm * scale, PROJ_TN // HEAD_DIM).reshape(1, PROJ_TN)
    w_gate = w_in[layer, :, nq:].reshape(D_MODEL, N_KV_GROUPS, HEADS_PER_GROUP * N_BRANCH)
    w_gate = jnp.pad(w_gate, ((0, 0), (0, 0), (0, LANE - HEADS_PER_GROUP * N_BRANCH)))
    w_gate = w_gate.reshape(D_MODEL, N_KV_GROUPS * LANE)
    q, gates_t = _norm_matmul_pair(
        x, norm_mix, (w_in, layer), w_gate, q_gain, pl.BlockSpec((1, PROJ_TN), lambda i, j: (0, 0)),
        tm=tm, tna=PROJ_TN, nja=nq // PROJ_TN, col_a=0, tnb=N_KV_GROUPS * LANE, njb=1, col_b=0,
        ep_a=_ep_q, ep_b=_ep_gates_t, dtype_a=BF16, dtype_b=F32, transpose_b=True)
    o = _nsa_attention(q, gates_t, *kvs)
    return _matmul_res(o, (w_out, layer), x, tm=tm, tn=PROJ_TN)


def kernel(x, p, norm_mix, norm_ffn, norm_ple, a_w_in, a_norm_v, a_w_s, a_b_s, a_w_out, kv_norm, kv_w, k_norm, cmp_pe_k, cmp_pe_v, cmp_wk1, cmp_wk2, cmp_wv1, cmp_wv2, b_w_in, b_q_norm, b_w_out, rel_bias, ffn_w_in, ffn_w_out, ple_w, ple_gate):
    batch, s, d = x.shape
    depth = norm_mix.shape[0]
    n_a = a_w_in.shape[0]
    a_w_out, b_w_in, b_w_out, kv_w, ple_gate, ffn_w_out = (
        w.astype(BF16) for w in (a_w_out, b_w_in, b_w_out, kv_w, ple_gate, ffn_w_out))
    outs = []
    for b in range(batch):
        xb = x.reshape(s, d) if batch == 1 else x[b]
        kvs = None
        for i in range(depth):
            if i < n_a:
                xb = _gmlp_layer(xb, i, norm_mix[i], a_w_in, a_norm_v[i], a_w_s, a_b_s[i], a_w_out)
            else:
                j = i - n_a
                xb = _nsa_layer(xb, j, norm_mix[i], b_w_in, b_q_norm[j], b_w_out, kvs)
            xb = _ffn_ple(xb, p, i, b, norm_ffn[i], ffn_w_in, ffn_w_out,
                          norm_ple[i], ple_w, ple_gate)
            if i == n_a - 1:
                k_cmp, v_cmp, v_cmp_t, ks, vs_far, vs_near, kw, vw_near = _shared_kv(
                    xb, kv_norm, kv_w, k_norm, cmp_pe_k, cmp_pe_v, cmp_wk1, cmp_wk2, cmp_wv1, cmp_wv2)
                ovl = _overlap_padded(s // CMP_STRIDE, s // SLC_BLOCK)
                kvs = (k_cmp, v_cmp, v_cmp_t, jnp.asarray(ovl), jnp.asarray(ovl.T).astype(BF16),
                       ks, vs_far, vs_near, kw, vw_near, _bias_tiles(rel_bias))
        outs.append(xb)
    return outs[0].reshape(1, s, d) if batch == 1 else jnp.stack(outs)
```

```python
import functools
import math

import numpy as np
import jax
import jax.numpy as jnp
from jax import lax
from jax.experimental import pallas as pl
from jax.experimental.pallas import tpu as pltpu

F32 = jnp.float32
BF16 = jnp.bfloat16

D_MODEL = 2048
PLE_DIM = 256
FFN_DIM = 5632
GMLP_CHUNK = 128
GMLP_GROUPS = 16
HEAD_DIM = 128
N_HEADS = 16
N_KV_GROUPS = 2
HEADS_PER_GROUP = 8
N_BRANCH = 3
CMP_BLOCK = 32
CMP_STRIDE = 16
SLC_BLOCK = 64
N_SELECT = 16
WINDOW = 512
Q_BLOCK = 128
N_BUCKETS = 32
MAX_DISTANCE = 128
EPS = 1e-6
NEG = -1e30
BIG = 1e30
LOG2E = math.log2(math.e)

LANE = 128
KV_TILE = 512
NEAR = WINDOW + Q_BLOCK
CMP_PAD = 128
SUM_ROWS = 16
VMEM_LIMIT = 56 * 1024 * 1024


def _cparams(sem):
    return pltpu.CompilerParams(dimension_semantics=sem, vmem_limit_bytes=VMEM_LIMIT)


def _dot(a, b):
    return jnp.dot(a, b, preferred_element_type=F32)


def _dot_nt(a, b):
    return lax.dot_general(a, b, (((1,), (1,)), ((), ())), preferred_element_type=F32)


def _rms_rows(x, g):
    ms = jnp.mean(x * x, axis=-1, keepdims=True)
    return x * lax.rsqrt(ms + EPS) * g


def _group_rms(acc, gain):
    outs = []
    for c in range(acc.shape[1] // LANE):
        a = acc[:, c * LANE:(c + 1) * LANE]
        outs.append(_rms_rows(a, gain[:, c * LANE:(c + 1) * LANE]))
    return outs[0] if len(outs) == 1 else jnp.concatenate(outs, axis=1)


def _norm_matmul_kernel(x_ref, g_ref, *refs, n_w, n_aux, epilogue, out_dtype):
    w_refs = refs[:n_w]
    aux_refs = refs[n_w:n_w + n_aux]
    o_ref = refs[n_w + n_aux]
    h_scr = refs[n_w + n_aux + 1]

    def column_tile(h):
        accs = [_dot(h, w_ref[...].astype(BF16)) for w_ref in w_refs]
        o_ref[...] = epilogue(accs, aux_refs + (x_ref,)).astype(out_dtype)

    @pl.when(pl.program_id(1) == 0)
    def _():
        h = _rms_rows(x_ref[...], g_ref[...]).astype(BF16)
        h_scr[...] = h
        column_tile(h)

    @pl.when(pl.program_id(1) > 0)
    def _():
        column_tile(h_scr[...])


def _w_spec(w, k, tn, col_of):
    if isinstance(w, tuple):
        layer = w[1]
        return pl.BlockSpec((None, k, tn), lambda i, j: (layer, 0, col_of(j)))
    return pl.BlockSpec((k, tn), lambda i, j: (0, col_of(j)))


def _w_array(w):
    return w[0] if isinstance(w, tuple) else w


def _norm_matmul(x, gain, w, col_offsets, *, tm, tn, nj, epilogue, out_dtype,
                 aux=(), aux_specs=()):
    s, k = x.shape
    in_specs = [pl.BlockSpec((tm, k), lambda i, j: (i, 0)),
                pl.BlockSpec((1, k), lambda i, j: (0, 0))]
    for c0 in col_offsets:
        in_specs.append(_w_spec(w, k, tn, lambda j, c0=c0: c0 + j))
    in_specs += list(aux_specs)
    kern = functools.partial(_norm_matmul_kernel, n_w=len(col_offsets), n_aux=len(aux),
                             epilogue=epilogue, out_dtype=out_dtype)
    return pl.pallas_call(
        kern,
        grid=(s // tm, nj),
        in_specs=in_specs,
        out_specs=pl.BlockSpec((tm, tn), lambda i, j: (i, j)),
        out_shape=jax.ShapeDtypeStruct((s, nj * tn), out_dtype),
        scratch_shapes=[pltpu.VMEM((tm, k), BF16)],
        compiler_params=_cparams(("arbitrary", "arbitrary")),
    )(x, gain.reshape(1, k), *([_w_array(w)] * len(col_offsets)), *aux)


def _norm_matmul_pair_kernel(x_ref, g_ref, wa_ref, wb_ref, aux_ref, oa_ref, ob_ref, h_scr, *, nja, ep_a, ep_b):
    j = pl.program_id(1)

    def tile_a(h):
        oa_ref[...] = ep_a(_dot(h, wa_ref[...].astype(BF16)), aux_ref, j).astype(oa_ref.dtype)

    @pl.when(j == 0)
    def _():
        h = _rms_rows(x_ref[...], g_ref[...]).astype(BF16)
        h_scr[...] = h
        tile_a(h)

    if nja > 1:
        @pl.when((j > 0) & (j < nja))
        def _():
            tile_a(h_scr[...])

    @pl.when(j >= nja)
    def _():
        ob_ref[...] = ep_b(_dot(h_scr[...], wb_ref[...].astype(BF16)), aux_ref, j - nja).astype(ob_ref.dtype)


def _norm_matmul_pair(x, gain, wa, wb, aux, aux_spec, *, tm, tna, nja, col_a, tnb, njb, col_b, ep_a, ep_b,
                      dtype_a, dtype_b, transpose_b=False):
    s, k = x.shape
    if transpose_b:
        b_spec = pl.BlockSpec((tnb, tm), lambda i, j: (jnp.maximum(j - nja, 0), i))
        b_shape = (njb * tnb, s)
    else:
        b_spec = pl.BlockSpec((tm, tnb), lambda i, j: (i, jnp.maximum(j - nja, 0)))
        b_shape = (s, njb * tnb)
    kern = functools.partial(_norm_matmul_pair_kernel, nja=nja, ep_a=ep_a, ep_b=ep_b)
    return pl.pallas_call(
        kern,
        grid=(s // tm, nja + njb),
        in_specs=[pl.BlockSpec((tm, k), lambda i, j: (i, 0)),
                  pl.BlockSpec((1, k), lambda i, j: (0, 0)),
                  _w_spec(wa, k, tna, lambda j: col_a + jnp.minimum(j, nja - 1)),
                  _w_spec(wb, k, tnb, lambda j: col_b + jnp.maximum(j - nja, 0)),
                  aux_spec],
        out_specs=[pl.BlockSpec((tm, tna), lambda i, j: (i, jnp.minimum(j, nja - 1))), b_spec],
        out_shape=[jax.ShapeDtypeStruct((s, nja * tna), dtype_a), jax.ShapeDtypeStruct(b_shape, dtype_b)],
        scratch_shapes=[pltpu.VMEM((tm, k), BF16)],
        compiler_params=_cparams(("arbitrary", "arbitrary")),
    )(x, gain.reshape(1, k), _w_array(wa), _w_array(wb), aux)


def _ep_swiglu(accs, aux):
    g, u = accs
    return g * jax.nn.sigmoid(g) * u


def _ep_ple(accs, aux):
    p_ref, wp_ref, x_ref = aux
    tn = accs[0].shape[1]
    col0 = pl.multiple_of(pl.program_id(1) * tn, tn)
    pp = _dot(p_ref[...].astype(BF16), wp_ref[...].astype(BF16))
    return x_ref[:, pl.ds(col0, tn)] + pp * jax.nn.sigmoid(accs[0])


def _ep_identity(acc, aux_ref, j):
    return acc


def _ep_gates_t(acc, aux_ref, j):
    return jax.nn.sigmoid(acc).T


def _ep_q(acc, gain_ref, j):
    return _group_rms(acc, gain_ref[...])


def _ep_kv(acc, gain_ref, j):
    normed = _group_rms(acc, gain_ref[0:1, :])
    return jnp.where((j % 2) == 0, normed, acc)


def _matmul_res_kernel(a_ref, w_ref, x_ref, o_ref):
    o_ref[...] = x_ref[...] + _dot(a_ref[...], w_ref[...].astype(BF16))


def _matmul_res(a, w, resid, *, tm, tn):
    s, k = a.shape
    n = _w_array(w).shape[-1]
    return pl.pallas_call(
        _matmul_res_kernel,
        grid=(s // tm, n // tn),
        in_specs=[pl.BlockSpec((tm, k), lambda i, j: (i, 0)),
                  _w_spec(w, k, tn, lambda j: j),
                  pl.BlockSpec((tm, tn), lambda i, j: (i, j))],
        out_specs=pl.BlockSpec((tm, tn), lambda i, j: (i, j)),
        out_shape=jax.ShapeDtypeStruct((s, n), F32),
        compiler_params=_cparams(("arbitrary", "arbitrary")),
    )(a, _w_array(w), resid)


GMLP_TN = 512
PROJ_TN = 1024


def _gmlp_kernel(x_ref, g_ref, win_ref, nv_ref, ws_ref, bs_ref, wo_ref, o_ref, h_scr, z_scr, *, tm):
    y_scr = h_scr
    width = GMLP_GROUPS * LANE
    n_in = 2 * width // GMLP_TN
    per_tile = GMLP_TN // LANE
    n_chunk = tm // GMLP_CHUNK
    j = pl.program_id(1)

    def z_tile(h):
        z_scr[j] = jax.nn.gelu(_dot(h, win_ref[...].astype(BF16)))

    @pl.when(j == 0)
    def _():
        h = _rms_rows(x_ref[...], g_ref[...]).astype(BF16)
        h_scr[...] = h
        z_tile(h)

    @pl.when((j > 0) & (j < n_in))
    def _():
        z_tile(h_scr[...])

    @pl.when(j == n_in)
    def _():
        v_tiles = range(n_in // 2, n_in)
        ms = sum(jnp.sum(z_scr[t] * z_scr[t], axis=-1, keepdims=True) for t in v_tiles) / width
        inv = lax.rsqrt(ms + EPS)
        row = lax.broadcasted_iota(jnp.int32, (GMLP_CHUNK, GMLP_CHUNK), 0)
        col = lax.broadcasted_iota(jnp.int32, (GMLP_CHUNK, GMLP_CHUNK), 1)
        causal = col <= row
        for g in range(GMLP_GROUPS):
            t, ls = g // per_tile, slice((g % per_tile) * LANE, (g % per_tile + 1) * LANE)
            cs = slice(g * LANE, (g + 1) * LANE)
            vn = (z_scr[n_in // 2 + t, :, ls] * inv * nv_ref[:, cs]).astype(BF16)
            ws = jnp.where(causal, ws_ref[g], 0.0).astype(BF16)
            vg = jnp.concatenate(
                [vn[c * GMLP_CHUNK:(c + 1) * GMLP_CHUNK, :] for c in range(n_chunk)], axis=1)
            sv = _dot(ws, vg) + bs_ref[:, g:g + 1]
            for c in range(n_chunk):
                rs = slice(c * GMLP_CHUNK, (c + 1) * GMLP_CHUNK)
                y_scr[rs, cs] = (z_scr[t, rs, ls] * sv[:, c * LANE:(c + 1) * LANE]).astype(BF16)

    @pl.when(j >= n_in)
    def _():
        col0 = pl.multiple_of((j - n_in) * GMLP_TN, GMLP_TN)
        o_ref[...] = x_ref[:, pl.ds(col0, GMLP_TN)] + _dot(y_scr[...], wo_ref[...].astype(BF16))


def _gmlp(x, gain, w_in, norm_v, w_s, b_s_t, w_out, layer, *, tm):
    s, d = x.shape
    width = GMLP_GROUPS * LANE
    assert width == d
    n_in = 2 * width // GMLP_TN
    n_out = d // GMLP_TN
    return pl.pallas_call(
        functools.partial(_gmlp_kernel, tm=tm),
        grid=(s // tm, n_in + n_out),
        in_specs=[pl.BlockSpec((tm, d), lambda i, j: (i, 0)),
                  pl.BlockSpec((1, d), lambda i, j: (0, 0)),
                  pl.BlockSpec((None, d, GMLP_TN), lambda i, j: (layer, 0, jnp.minimum(j, n_in - 1))),
                  pl.BlockSpec((1, width), lambda i, j: (0, 0)),
                  pl.BlockSpec((None, GMLP_GROUPS, GMLP_CHUNK, GMLP_CHUNK), lambda i, j: (layer, 0, 0, 0)),
                  pl.BlockSpec((GMLP_CHUNK, GMLP_GROUPS), lambda i, j: (0, 0)),
                  pl.BlockSpec((None, width, GMLP_TN), lambda i, j: (layer, 0, jnp.maximum(j - n_in, 0)))],
        out_specs=pl.BlockSpec((tm, GMLP_TN), lambda i, j: (i, jnp.maximum(j - n_in, 0))),
        out_shape=jax.ShapeDtypeStruct((s, d), F32),
        scratch_shapes=[pltpu.VMEM((tm, d), BF16),
                        pltpu.VMEM((n_in, tm, GMLP_TN), F32)],
        compiler_params=_cparams(("arbitrary", "arbitrary")),
    )(x, gain.reshape(1, d), w_in, norm_v.reshape(1, width), w_s, b_s_t, w_out)


def _compress_kernel(kv_ref, pe_ref, w1_ref, w2_ref, kn_ref, o_ref, *, n_half):
    half_w = CMP_STRIDE * HEAD_DIM
    hh = jnp.concatenate([kv_ref[pl.ds(r, n_half, stride=CMP_STRIDE), :] for r in range(CMP_STRIDE)],
                         axis=1)
    pe = pe_ref[...]
    a = _dot((hh + pe[0:1, :]).astype(BF16), w1_ref[0:half_w, :].astype(BF16))
    b = _dot((hh + pe[1:2, :]).astype(BF16), w1_ref[half_w:2 * half_w, :].astype(BF16))
    pre = a + pltpu.roll(b, n_half - 1, 0)
    out = _dot(jax.nn.gelu(pre).astype(BF16), w2_ref[...].astype(BF16))
    is_key = pl.program_id(0) < N_KV_GROUPS
    out = jnp.where(is_key, _rms_rows(out, kn_ref[...]), out)
    row = lax.broadcasted_iota(jnp.int32, out.shape, 0)
    o_ref[...] = jnp.where(row < n_half - 1, out, 0.0)


def _compress(kvc, pe, w1, w2, k_norm0):
    s = kvc.shape[0]
    n_half = s // CMP_STRIDE
    half_w = CMP_STRIDE * HEAD_DIM
    hid = w1.shape[2]
    return pl.pallas_call(
        functools.partial(_compress_kernel, n_half=n_half),
        grid=(2 * N_KV_GROUPS,),
        in_specs=[pl.BlockSpec((s, HEAD_DIM), lambda n: (0, n)),
                  pl.BlockSpec((None, 2, half_w), lambda n: (n // N_KV_GROUPS, 0, 0)),
                  pl.BlockSpec((None, 2 * half_w, hid), lambda n: (n // N_KV_GROUPS, 0, 0)),
                  pl.BlockSpec((None, hid, HEAD_DIM), lambda n: (n // N_KV_GROUPS, 0, 0)),
                  pl.BlockSpec((1, HEAD_DIM), lambda n: (0, 0))],
        out_specs=pl.BlockSpec((None, n_half, HEAD_DIM), lambda n: (n, 0, 0)),
        out_shape=jax.ShapeDtypeStruct((2 * N_KV_GROUPS, n_half, HEAD_DIM), F32),
        compiler_params=_cparams(("arbitrary",)),
    )(kvc, pe, w1, w2, k_norm0.reshape(1, HEAD_DIM))


def _t5_bucket_np(dist):
    n = np.maximum(dist, 0)
    max_exact = N_BUCKETS // 2
    nf = np.maximum(n, 1).astype(np.float32)
    large = max_exact + (np.log(nf / np.float32(max_exact)) / np.float32(math.log(MAX_DISTANCE / max_exact))
                         * np.float32(N_BUCKETS - max_exact)).astype(np.int32)
    large = np.minimum(large, N_BUCKETS - 1)
    return np.where(n < max_exact, n, large).astype(np.int32)


N_PATTERNS = 4


def _bucket_patterns():
    i = np.arange(Q_BLOCK)[:, None]
    c = np.arange(LANE)[None, :]
    d0 = i - c
    d1 = i - c + Q_BLOCK
    dc = i - CMP_STRIDE * (c - (LANE - 8)) - (CMP_BLOCK - 1)
    pats = [np.where(d >= 0, _t5_bucket_np(d), -1) for d in (d0, d1, dc)]
    pats.append(np.where(i < c, N_BUCKETS - 1, -1))
    return np.stack([p.T for p in pats]).astype(np.int32)


def _bias_tiles_kernel(tab_ref, pat_ref, o_ref):
    h = pl.program_id(0)
    pat = pat_ref[...]
    far = tab_ref[N_BUCKETS - 1, h]
    acc = jnp.full(pat.shape, NEG, F32)
    for b in range(N_BUCKETS):
        acc = jnp.where(pat == b, (tab_ref[b, h] - far) * LOG2E, acc)
    o_ref[...] = acc


def _bias_tiles(rel_bias):
    pats = jnp.asarray(_bucket_patterns())
    return pl.pallas_call(
        _bias_tiles_kernel,
        grid=(N_HEADS,),
        in_specs=[pl.BlockSpec(memory_space=pltpu.SMEM),
                  pl.BlockSpec((N_PATTERNS, Q_BLOCK, LANE), lambda h: (0, 0, 0))],
        out_specs=pl.BlockSpec((None, N_PATTERNS, Q_BLOCK, LANE), lambda h: (h, 0, 0, 0)),
        out_shape=jax.ShapeDtypeStruct((N_HEADS, N_PATTERNS, Q_BLOCK, LANE), F32),
        compiler_params=_cparams(("arbitrary",)),
    )(rel_bias, pats)


def _with_features(qs, feat):
    reps = qs.shape[0] // feat.shape[0]
    return jnp.concatenate([qs, jnp.concatenate([feat] * reps, axis=0)], axis=1)


def _softmax_cols(blocks):
    m = blocks[0].max(axis=0, keepdims=True)
    for b in blocks[1:]:
        m = jnp.maximum(m, b.max(axis=0, keepdims=True))
    es = [jnp.exp2(b - m) for b in blocks]
    den = es[0].sum(axis=0, keepdims=True)
    for e in es[1:]:
        den = den + e.sum(axis=0, keepdims=True)
    return es, m, den


def _nsa_kernel(q_ref, gt_ref, kc_ref, vc_ref, vct_ref, ovl_ref, ovlt_ref, ks_ref, vsf_ref, vsn_ref,
                kw_ref, vwn_ref, tb_ref, o_ref, m_scr, acc_scr, sa_scr, sb_scr, cmp_o_scr, cmp_pf_scr, cmp_pn_scr,
                *, n_slc):
    qb = pl.program_id(1)
    s = qb * Q_BLOCK
    hpg = HEADS_PER_GROUP

    q_all = q_ref[...]
    qs = jnp.concatenate([q_all[:, h * LANE:(h + 1) * LANE] for h in range(hpg)], axis=0)
    cols = [slice(h * Q_BLOCK, (h + 1) * Q_BLOCK) for h in range(hpg)]
    lane_f = lax.broadcasted_iota(jnp.int32, (Q_BLOCK, LANE), 1)
    pad_feat = jnp.where(lane_f == LANE - 1, 1.0, 0.0).astype(BF16)
    q_pad = _with_features(qs, pad_feat)

    n_sub = NEAR // LANE
    row0 = pl.multiple_of(s, Q_BLOCK)

    n_pair = hpg // 2
    pair_rows = [slice(pr * 2 * Q_BLOCK, (pr + 1) * 2 * Q_BLOCK) for pr in range(n_pair)]
    half = [slice(0, Q_BLOCK), slice(Q_BLOCK, 2 * Q_BLOCK)]

    def run_stages(stages):
        pending = stages[0][0]()
        for i, (_, consume) in enumerate(stages):
            cur = pending
            if i + 1 < len(stages):
                pending = stages[i + 1][0]()
            consume(cur)

    def near_stages(k_ref, q_aug, vn_ref, oldest_pat, res):
        k_aug = k_ref[pl.ds(row0, NEAR), :]
        v_blocks = vn_ref[pl.ds(qb, n_sub)]
        v_t = jnp.concatenate([v_blocks[u] for u in range(n_sub)], axis=1)

        def stage(pr):
            def issue():
                return _dot_nt(k_aug, q_aug[pair_rows[pr], :])

            def consume(logits):
                ps = []
                for hh in range(2):
                    h = 2 * pr + hh
                    blocks = [logits[u * LANE:(u + 1) * LANE, half[hh]] for u in range(n_sub)]
                    blocks[n_sub - 1] = blocks[n_sub - 1] + tb_ref[h, 0]
                    blocks[n_sub - 2] = blocks[n_sub - 2] + tb_ref[h, 1]
                    if oldest_pat is not None:
                        blocks[0] = blocks[0] + tb_ref[h, oldest_pat]
                    m = blocks[0].max(axis=0, keepdims=True)
                    for b in blocks[1:]:
                        m = jnp.maximum(m, b.max(axis=0, keepdims=True))
                    ps.append(jnp.concatenate([jnp.exp2((b - m).astype(BF16)) for b in blocks], axis=0))
                    res["m"].append(m)
                res["o"].append(_dot(v_t, jnp.concatenate(ps, axis=1)))
            return issue, consume
        return [stage(pr) for pr in range(n_pair)]

    near0 = pl.multiple_of(qb * 8 + 8, 8)
    kn = kc_ref[pl.ds(near0, LANE), :].astype(BF16)
    vn_t = vc_ref[pl.ds(near0, LANE), :].T.astype(BF16)
    ovl_n_t = ovl_ref[pl.ds(near0, LANE), :].T.astype(BF16)
    far_feat = jnp.where(lane_f > qb - 16, 1.0, 0.0).astype(BF16)
    q_cmp = _with_features(qs, far_feat)
    t_row = s + lax.broadcasted_iota(jnp.int32, (1, Q_BLOCK), 1)
    row_ok = t_row >= CMP_BLOCK - 1
    n_half = kc_ref.shape[0] - CMP_PAD

    def cmp_branch(n_keys):
        kc = kc_ref[CMP_PAD:CMP_PAD + n_keys, :].astype(BF16)
        vc_t = vct_ref[:, CMP_PAD:CMP_PAD + n_keys]
        res = {"o": [], "pf": jnp.zeros((n_keys, Q_BLOCK), F32), "pn": jnp.zeros((LANE, Q_BLOCK), F32)}
        heads = hpg * LANE // min(n_keys, hpg * LANE // 2)

        def stage(first):
            q_rows = slice(first * Q_BLOCK, (first + heads) * Q_BLOCK)

            def issue():
                return _dot_nt(kc, q_cmp[q_rows, :]), _dot_nt(kn, q_pad[q_rows, :])

            def consume(logits):
                sf, sn = logits
                pfs, pns = [], []
                for hh in range(heads):
                    (ef, en), _, den = _softmax_cols([sf[:, cols[hh]], sn[:, cols[hh]] + tb_ref[first + hh, 2]])
                    inv = jnp.where(row_ok, 1.0 / den, 0.0)
                    pf = ef * inv
                    pn = en * inv
                    res["pf"] = res["pf"] + pf
                    res["pn"] = res["pn"] + pn
                    pfs.append(pf.astype(BF16))
                    pns.append(pn.astype(BF16))
                res["o"].append(_dot(vc_t, jnp.concatenate(pfs, axis=1)) + _dot(vn_t, jnp.concatenate(pns, axis=1)))
            return issue, consume

        run_stages([stage(first) for first in range(0, hpg, heads)])
        cmp_o_scr[...] = jnp.concatenate(res["o"], axis=1)
        cmp_pf_scr[0:n_keys, :] = res["pf"]
        if n_keys < n_half:
            cmp_pf_scr[n_keys:, :] = jnp.zeros((n_half - n_keys, Q_BLOCK), F32)
        cmp_pn_scr[...] = res["pn"]

    sizes = sorted({min(n_half, c) for c in (LANE, 2 * LANE)} | {n_half})
    lo = 0
    for idx, n_keys in enumerate(sizes):
        last = idx + 1 == len(sizes)
        hi = (n_keys + LANE - 8) // 8 + 1
        pl.when((qb >= lo) if last else ((qb >= lo) & (qb < hi)))(functools.partial(cmp_branch, n_keys))
        lo = hi

    blk = lax.broadcasted_iota(jnp.int32, (n_slc, Q_BLOCK), 0)
    blk_f = blk.astype(F32)
    cur = (s + lax.broadcasted_iota(jnp.int32, (n_slc, Q_BLOCK), 1)) >> 6
    forced = (blk == 0) | (blk == cur) | (blk == cur - 1)
    valid = blk <= cur
    topk = {}

    def importance_stage():
        def consume(_):
            psum_f, psum_n = cmp_pf_scr[...], cmp_pn_scr[...]
            ovl_t = ovlt_ref[:, CMP_PAD:]
            pf_hi = psum_f.astype(BF16)
            pf_lo = (psum_f - pf_hi.astype(F32)).astype(BF16)
            pn_hi = psum_n.astype(BF16)
            pn_lo = (psum_n - pn_hi.astype(F32)).astype(BF16)
            imp_t = _dot(ovl_t, pf_hi) + _dot(ovl_t, pf_lo) + _dot(ovl_n_t, pn_hi) + _dot(ovl_n_t, pn_lo)
            topk["score"] = jnp.where(valid & ~forced, imp_t, -BIG)
            topk["sel"] = jnp.where(forced, 1.0, 0.0)
        return (lambda: None), consume

    def topk_stage(rounds):
        def consume(_):
            score, sel_t = topk["score"], topk["sel"]
            for _r in range(rounds):
                top = jnp.max(score, axis=0, keepdims=True)
                first = jnp.min(jnp.where(score == top, blk_f, float(n_slc)), axis=0, keepdims=True)
                pick = blk_f == first
                sel_t = jnp.where(pick, 1.0, sel_t)
                score = jnp.where(pick, -2.0 * BIG, score)
            topk["score"], topk["sel"] = score, sel_t
        return (lambda: None), consume

    win_res = {"o": [], "m": []}
    win_stages = near_stages(kw_ref, q_pad, vwn_ref, 3, win_res)
    free_picks = max(min(N_SELECT, n_slc) - 3, 0)
    rounds = [free_picks // n_pair + (1 if pr < free_picks % n_pair else 0) for pr in range(n_pair)]
    stages = [importance_stage()]
    for pr in range(n_pair):
        stages += [win_stages[pr], topk_stage(rounds[pr])]
    run_stages(stages)
    o_win = jnp.concatenate(win_res["o"], axis=1)
    o_cmp = cmp_o_scr[...]
    sel_t = topk["sel"]
    drop_t = jnp.where(valid, 1.0 - sel_t, 1.0)
    near_blk = (s - WINDOW) >> 6
    drop_far_t = jnp.where(blk >= near_blk, 1.0, drop_t)
    q_near = _with_features(qs, drop_t.T.astype(BF16))
    q_far = _with_features(qs, drop_far_t.T.astype(BF16))

    def far_logits(t):
        r0 = pl.multiple_of(WINDOW + KV_TILE * t, KV_TILE)
        return _dot_nt(ks_ref[pl.ds(r0, KV_TILE), :], q_far)

    sa_scr[...] = far_logits(0)
    slc_res = {"o": [], "m": []}
    run_stages(near_stages(ks_ref, q_near, vsn_ref, None, slc_res))
    m_scr[...] = jnp.concatenate(slc_res["m"], axis=1)
    acc_scr[...] = jnp.concatenate(slc_res["o"], axis=1)

    gates = gt_ref[...]
    g_c, g_s, g_w = [jnp.concatenate([gates[3 * h + br:3 * h + br + 1, :] for h in range(hpg)], axis=1)
                     for br in range(N_BRANCH)]
    out_cw = g_c * o_cmp + (g_w / o_win[HEAD_DIM:HEAD_DIM + 1, :]) * o_win[:HEAD_DIM, :]

    n_far = (jnp.maximum(s - WINDOW, 0) + KV_TILE - 1) // KV_TILE

    n_tiles = vsf_ref.shape[0]

    def far_probs(st_ref):
        m_old = m_scr[...]
        sts = [st_ref[:, cols[h]] for h in range(hpg)]
        m_new = jnp.concatenate(
            [jnp.maximum(m_old[:, cols[h]], sts[h].max(axis=0, keepdims=True)) for h in range(hpg)], axis=1)
        p_t = jnp.concatenate(
            [jnp.exp2((sts[h] - m_new[:, cols[h]]).astype(BF16)) for h in range(hpg)], axis=1)
        m_scr[...] = m_new
        return p_t, jnp.exp2(m_old - m_new)

    def far_accumulate(t, p_t, alpha):
        acc_scr[...] = alpha * acc_scr[...] + _dot(vsf_ref[t], p_t)

    def far_body(i, carry):
        t0 = 2 * i
        sb_scr[...] = far_logits(t0 + 1)
        p_a, alpha_a = far_probs(sa_scr)
        sa_scr[...] = far_logits(jnp.minimum(t0 + 2, n_tiles - 1))
        far_accumulate(t0, p_a, alpha_a)
        p_b, alpha_b = far_probs(sb_scr)
        far_accumulate(t0 + 1, p_b, alpha_b)
        return carry

    lax.fori_loop(0, n_far // 2, far_body, 0)

    @pl.when(n_far % 2 == 1)
    def _():
        p_last, alpha_last = far_probs(sa_scr)
        far_accumulate(n_far - 1, p_last, alpha_last)

    out_t = out_cw + (g_s / acc_scr[HEAD_DIM:HEAD_DIM + 1, :]) * acc_scr[:HEAD_DIM, :]
    for h in range(hpg):
        o_ref[:, cols[h]] = out_t[:, cols[h]].T.astype(o_ref.dtype)


def _nsa_attention(q, gates_t, k_cmp, v_cmp, v_cmp_t, ovl, ovl_t, ks, vs_far, vs_near, kw, vw_near, bias_tiles):
    s = q.shape[0]
    n_qb = s // Q_BLOCK
    n_slc = s // SLC_BLOCK
    gw = HEADS_PER_GROUP * HEAD_DIM
    lanes = HEADS_PER_GROUP * Q_BLOCK

    def group_spec(a):
        zeros = (0,) * (a.ndim - 1)
        return pl.BlockSpec((None,) + a.shape[1:], lambda g, i: (g,) + zeros)

    def whole_spec(a):
        zeros = (0,) * a.ndim
        return pl.BlockSpec(a.shape, lambda g, i: zeros)

    return pl.pallas_call(
        functools.partial(_nsa_kernel, n_slc=n_slc),
        grid=(N_KV_GROUPS, n_qb),
        in_specs=[pl.BlockSpec((Q_BLOCK, gw), lambda g, i: (i, g)),
                  pl.BlockSpec((LANE, Q_BLOCK), lambda g, i: (g, i)),
                  group_spec(k_cmp), group_spec(v_cmp), group_spec(v_cmp_t),
                  whole_spec(ovl), whole_spec(ovl_t),
                  group_spec(ks), group_spec(vs_far), group_spec(vs_near),
                  group_spec(kw), group_spec(vw_near),
                  pl.BlockSpec((HEADS_PER_GROUP, N_PATTERNS, LANE, Q_BLOCK), lambda g, i: (g, 0, 0, 0))],
        out_specs=pl.BlockSpec((Q_BLOCK, gw), lambda g, i: (i, g)),
        out_shape=jax.ShapeDtypeStruct((s, N_KV_GROUPS * gw), BF16),
        scratch_shapes=[pltpu.VMEM((1, lanes), F32),
                        pltpu.VMEM((HEAD_DIM + SUM_ROWS, lanes), F32),
                        pltpu.VMEM((KV_TILE, lanes), F32),
                        pltpu.VMEM((KV_TILE, lanes), F32),
                        pltpu.VMEM((HEAD_DIM, lanes), F32),
                        pltpu.VMEM((k_cmp.shape[1] - CMP_PAD, Q_BLOCK), F32),
                        pltpu.VMEM((LANE, Q_BLOCK), F32)],
        compiler_params=_cparams(("arbitrary", "arbitrary")),
    )(q, gates_t, k_cmp, v_cmp, v_cmp_t, ovl, ovl_t, ks, vs_far, vs_near, kw, vw_near, bias_tiles)


def _cmp_mask_columns(n_half):
    assert n_half // 8 < LANE - 1
    out = np.zeros((CMP_PAD + n_half, LANE), np.float32)
    out[np.arange(CMP_PAD), LANE - 1] = NEG
    k = np.arange(n_half)
    out[CMP_PAD + k, k // 8] = NEG
    return out


def _slc_mask_columns(s):
    n_slc = s // SLC_BLOCK
    assert n_slc - 1 > (WINDOW + Q_BLOCK) // SLC_BLOCK
    out = np.zeros((WINDOW + s, n_slc), np.float32)
    out[np.arange(WINDOW), n_slc - 1] = NEG
    pos = np.arange(s)
    out[WINDOW + pos, pos // SLC_BLOCK] = NEG
    return out


def _pad_mask_columns(s):
    out = np.zeros((WINDOW + s, LANE), np.float32)
    out[np.arange(WINDOW), LANE - 1] = NEG
    return out


def _overlap_padded(n_half, n_slc):
    n_cmp = n_half - 1
    c0 = np.arange(n_cmp) * CMP_STRIDE
    s0 = np.arange(n_slc) * SLC_BLOCK
    lo = np.maximum(c0[:, None], s0[None, :])
    hi = np.minimum(c0[:, None] + CMP_BLOCK, s0[None, :] + SLC_BLOCK)
    ovl = np.maximum(hi - lo, 0).astype(np.float32) / CMP_BLOCK
    out = np.zeros((CMP_PAD + n_half, n_slc), np.float32)
    out[CMP_PAD:CMP_PAD + n_cmp] = ovl
    return out


def _row_tile(s):
    return min(1024, s)


def _ffn_ple(x, p, layer, b, norm_ffn, w_in, w_out, norm_ple, ple_w, ple_gate):
    s = x.shape[0]
    tm = _row_tile(s)
    tn = 512
    nj = FFN_DIM // tn
    act = _norm_matmul(x, norm_ffn, (w_in, layer), (0, nj), tm=tm, tn=tn, nj=nj,
                       epilogue=_ep_swiglu, out_dtype=BF16)
    x = _matmul_res(act, (w_out, layer), x, tm=tm, tn=512)
    aux = (p, ple_w)
    tn_ple = PROJ_TN
    aux_specs = (pl.BlockSpec((None, None, tm, PLE_DIM), lambda i, j: (layer, b, i, 0)),
                 pl.BlockSpec((None, PLE_DIM, tn_ple), lambda i, j: (layer, 0, j)))
    return _norm_matmul(x, norm_ple, (ple_gate, layer), (0,), tm=tm, tn=tn_ple, nj=D_MODEL // tn_ple,
                        epilogue=_ep_ple, out_dtype=F32, aux=aux, aux_specs=aux_specs)


def _gmlp_layer(x, layer, norm_mix, w_in, norm_v, w_s, b_s, w_out):
    return _gmlp(x, norm_mix, w_in, norm_v, w_s, jnp.transpose(b_s), w_out, layer, tm=_row_tile(x.shape[0]))


def _shared_kv(x, kv_norm, kv_w, k_norm, cmp_pe_k, cmp_pe_v, cmp_wk1, cmp_wk2, cmp_wv1, cmp_wv2):
    s = x.shape[0]
    tm = _row_tile(s)
    gw = N_KV_GROUPS * HEAD_DIM
    gains = jnp.stack([jnp.tile(k_norm[1], N_KV_GROUPS), jnp.ones((gw,), F32),
                       jnp.tile(k_norm[2], N_KV_GROUPS), jnp.ones((gw,), F32)])
    gains = jnp.broadcast_to(gains[:, None, :], (4, 8, gw))
    kvc, kvr = _norm_matmul_pair(
        x, kv_norm, kv_w, kv_w, gains, pl.BlockSpec((None, 8, gw), lambda i, j: (jnp.maximum(j - 1, 0), 0, 0)),
        tm=tm, tna=2 * gw, nja=1, col_a=0, tnb=gw, njb=4, col_b=2,
        ep_a=_ep_identity, ep_b=_ep_kv, dtype_a=F32, dtype_b=BF16)
    n_half = s // CMP_STRIDE
    pe = jnp.stack([cmp_pe_k, cmp_pe_v]).reshape(2, 2, CMP_STRIDE * HEAD_DIM)
    kv_cmp = _compress(kvc, pe, jnp.stack([cmp_wk1, cmp_wv1]), jnp.stack([cmp_wk2, cmp_wv2]), k_norm[0])
    kv_cmp = jnp.pad(kv_cmp, ((0, 0), (CMP_PAD, 0), (0, 0)))
    cmp_cols = jnp.broadcast_to(jnp.asarray(_cmp_mask_columns(n_half)), (N_KV_GROUPS, CMP_PAD + n_half, LANE))
    k_cmp = jnp.concatenate([kv_cmp[:N_KV_GROUPS], cmp_cols], axis=2)
    v_cmp = kv_cmp[N_KV_GROUPS:]
    v_cmp_t = jnp.transpose(v_cmp, (0, 2, 1)).astype(BF16)
    s_pad = WINDOW + s
    kvr = jnp.pad(kvr, ((WINDOW, 0), (0, 0))).reshape(s_pad, 4, N_KV_GROUPS, HEAD_DIM)
    kvr = jnp.transpose(kvr, (1, 2, 0, 3))
    slc_cols = jnp.asarray(_slc_mask_columns(s)).astype(BF16)
    pad_cols = jnp.asarray(_pad_mask_columns(s)).astype(BF16)
    ks = jnp.concatenate([kvr[0], jnp.broadcast_to(slc_cols, (N_KV_GROUPS,) + slc_cols.shape)], axis=2)
    kw = jnp.concatenate([kvr[2], jnp.broadcast_to(pad_cols, (N_KV_GROUPS,) + pad_cols.shape)], axis=2)

    def key_tiles_t(v, tile):
        v_t = jnp.transpose(v.reshape(N_KV_GROUPS, -1, tile, HEAD_DIM), (0, 1, 3, 2))
        return jnp.concatenate([v_t, jnp.ones(v_t.shape[:2] + (SUM_ROWS, tile), v_t.dtype)], axis=2)

    vs_far = key_tiles_t(kvr[1][:, WINDOW:], KV_TILE)
    return k_cmp, v_cmp, v_cmp_t, ks, vs_far, key_tiles_t(kvr[1], LANE), kw, key_tiles_t(kvr[3], LANE)


def _nsa_layer(x, layer, norm_mix, w_in, q_norm, w_out, kvs):
    s = x.shape[0]
    tm = _row_tile(s)
    nq = N_HEADS * HEAD_DIM
    scale = HEAD_DIM ** -0.5 * LOG2E
    q_gain = jnp.tile(q_norm * scale, PROJ_TN // HEAD_DIM).reshape(1, PROJ_TN)
    w_gate = w_in[layer, :, nq:].reshape(D_MODEL, N_KV_GROUPS, HEADS_PER_GROUP * N_BRANCH)
    w_gate = jnp.pad(w_gate, ((0, 0), (0, 0), (0, LANE - HEADS_PER_GROUP * N_BRANCH)))
    w_gate = w_gate.reshape(D_MODEL, N_KV_GROUPS * LANE)
    q, gates_t = _norm_matmul_pair(
        x, norm_mix, (w_in, layer), w_gate, q_gain, pl.BlockSpec((1, PROJ_TN), lambda i, j: (0, 0)),
        tm=tm, tna=PROJ_TN, nja=nq // PROJ_TN, col_a=0, tnb=N_KV_GROUPS * LANE, njb=1, col_b=0,
        ep_a=_ep_q, ep_b=_ep_gates_t, dtype_a=BF16, dtype_b=F32, transpose_b=True)
    o = _nsa_attention(q, gates_t, *kvs)
    return _matmul_res(o, (w_out, layer), x, tm=tm, tn=PROJ_TN)


def kernel(x, p, norm_mix, norm_ffn, norm_ple, a_w_in, a_norm_v, a_w_s, a_b_s, a_w_out, kv_norm, kv_w, k_norm, cmp_pe_k, cmp_pe_v, cmp_wk1, cmp_wk2, cmp_wv1, cmp_wv2, b_w_in, b_q_norm, b_w_out, rel_bias, ffn_w_in, ffn_w_out, ple_w, ple_gate):
    batch, s, d = x.shape
    depth = norm_mix.shape[0]
    n_a = a_w_in.shape[0]
    a_w_out, b_w_in, b_w_out, kv_w, ple_gate, ffn_w_out = (
        w.astype(BF16) for w in (a_w_out, b_w_in, b_w_out, kv_w, ple_gate, ffn_w_out))
    outs = []
    for b in range(batch):
        xb = x.reshape(s, d) if batch == 1 else x[b]
        kvs = None
        for i in range(depth):
            if i < n_a:
                xb = _gmlp_layer(xb, i, norm_mix[i], a_w_in, a_norm_v[i], a_w_s, a_b_s[i], a_w_out)
            else:
                j = i - n_a
                xb = _nsa_layer(xb, j, norm_mix[i], b_w_in, b_q_norm[j], b_w_out, kvs)
            xb = _ffn_ple(xb, p, i, b, norm_ffn[i], ffn_w_in, ffn_w_out,
                          norm_ple[i], ple_w, ple_gate)
            if i == n_a - 1:
                k_cmp, v_cmp, v_cmp_t, ks, vs_far, vs_near, kw, vw_near = _shared_kv(
                    xb, kv_norm, kv_w, k_norm, cmp_pe_k, cmp_pe_v, cmp_wk1, cmp_wk2, cmp_wv1, cmp_wv2)
                ovl = _overlap_padded(s // CMP_STRIDE, s // SLC_BLOCK)
                kvs = (k_cmp, v_cmp, v_cmp_t, jnp.asarray(ovl), jnp.asarray(ovl.T).astype(BF16),
                       ks, vs_far, vs_near, kw, vw_near, _bias_tiles(rel_bias))
        outs.append(xb)
    return outs[0].reshape(1, s, d) if batch == 1 else jnp.stack(outs)
```

```python
import functools
import math

import numpy as np
import jax
import jax.numpy as jnp
from jax import lax
from jax.experimental import pallas as pl
from jax.experimental.pallas import tpu as pltpu

F32 = jnp.float32
BF16 = jnp.bfloat16

D_MODEL = 2048
PLE_DIM = 256
FFN_DIM = 5632
GMLP_CHUNK = 128
GMLP_GROUPS = 16
HEAD_DIM = 128
N_HEADS = 16
N_KV_GROUPS = 2
HEADS_PER_GROUP = 8
N_BRANCH = 3
CMP_BLOCK = 32
CMP_STRIDE = 16
SLC_BLOCK = 64
N_SELECT = 16
WINDOW = 512
Q_BLOCK = 128
N_BUCKETS = 32
MAX_DISTANCE = 128
EPS = 1e-6
NEG = -1e30
BIG = 1e30
LOG2E = math.log2(math.e)

LANE = 128
KV_TILE = 512
NEAR = WINDOW + Q_BLOCK
CMP_PAD = 128
SUM_ROWS = 16
VMEM_LIMIT = 56 * 1024 * 1024


def _cparams(sem):
    return pltpu.CompilerParams(dimension_semantics=sem, vmem_limit_bytes=VMEM_LIMIT)


def _dot(a, b):
    return jnp.dot(a, b, preferred_element_type=F32)


def _dot_nt(a, b):
    return lax.dot_general(a, b, (((1,), (1,)), ((), ())), preferred_element_type=F32)


def _rms_rows(x, g):
    ms = jnp.mean(x * x, axis=-1, keepdims=True)
    return x * lax.rsqrt(ms + EPS) * g


def _group_rms(acc, gain):
    outs = []
    for c in range(acc.shape[1] // LANE):
        a = acc[:, c * LANE:(c + 1) * LANE]
        outs.append(_rms_rows(a, gain[:, c * LANE:(c + 1) * LANE]))
    return outs[0] if len(outs) == 1 else jnp.concatenate(outs, axis=1)


EPILOGUE_COLS = 256


def _column_chunks(tn):
    cw = min(tn, EPILOGUE_COLS)
    return [slice(c, c + cw) for c in range(0, tn, cw)]


def _norm_matmul_kernel(x_ref, g_ref, *refs, n_w, n_aux, epilogue, out_dtype):
    w_refs = refs[:n_w]
    aux_refs = refs[n_w:n_w + n_aux]
    o_ref = refs[n_w + n_aux]
    h_scr = refs[n_w + n_aux + 1]

    def column_tile(h):
        chunks = _column_chunks(o_ref.shape[1])
        accs = [[_dot(h, w_ref[:, cs].astype(BF16)) for w_ref in w_refs] for cs in chunks]
        for cs, acc in zip(chunks, accs):
            o_ref[:, cs] = epilogue(acc, aux_refs + (x_ref,), cs).astype(out_dtype)

    @pl.when(pl.program_id(1) == 0)
    def _():
        h = _rms_rows(x_ref[...], g_ref[...]).astype(BF16)
        h_scr[...] = h
        column_tile(h)

    @pl.when(pl.program_id(1) > 0)
    def _():
        column_tile(h_scr[...])


def _w_spec(w, k, tn, col_of):
    if isinstance(w, tuple):
        layer = w[1]
        return pl.BlockSpec((None, k, tn), lambda i, j: (layer, 0, col_of(j)))
    return pl.BlockSpec((k, tn), lambda i, j: (0, col_of(j)))


def _w_array(w):
    return w[0] if isinstance(w, tuple) else w


def _norm_matmul(x, gain, w, col_offsets, *, tm, tn, nj, epilogue, out_dtype,
                 aux=(), aux_specs=()):
    s, k = x.shape
    in_specs = [pl.BlockSpec((tm, k), lambda i, j: (i, 0)),
                pl.BlockSpec((1, k), lambda i, j: (0, 0))]
    for c0 in col_offsets:
        in_specs.append(_w_spec(w, k, tn, lambda j, c0=c0: c0 + j))
    in_specs += list(aux_specs)
    kern = functools.partial(_norm_matmul_kernel, n_w=len(col_offsets), n_aux=len(aux),
                             epilogue=epilogue, out_dtype=out_dtype)
    return pl.pallas_call(
        kern,
        grid=(s // tm, nj),
        in_specs=in_specs,
        out_specs=pl.BlockSpec((tm, tn), lambda i, j: (i, j)),
        out_shape=jax.ShapeDtypeStruct((s, nj * tn), out_dtype),
        scratch_shapes=[pltpu.VMEM((tm, k), BF16)],
        compiler_params=_cparams(("arbitrary", "arbitrary")),
    )(x, gain.reshape(1, k), *([_w_array(w)] * len(col_offsets)), *aux)


def _norm_matmul_pair_kernel(x_ref, g_ref, wa_ref, wb_ref, aux_ref, oa_ref, ob_ref, h_scr, *, nja, ep_a, ep_b):
    j = pl.program_id(1)

    def tile_a(h):
        chunks = _column_chunks(oa_ref.shape[1])
        accs = [_dot(h, wa_ref[:, cs].astype(BF16)) for cs in chunks]
        for cs, acc in zip(chunks, accs):
            oa_ref[:, cs] = ep_a(acc, aux_ref, j, cs).astype(oa_ref.dtype)

    @pl.when(j == 0)
    def _():
        h = _rms_rows(x_ref[...], g_ref[...]).astype(BF16)
        h_scr[...] = h
        tile_a(h)

    if nja > 1:
        @pl.when((j > 0) & (j < nja))
        def _():
            tile_a(h_scr[...])

    @pl.when(j >= nja)
    def _():
        acc = _dot(h_scr[...], wb_ref[...].astype(BF16))
        ob_ref[...] = ep_b(acc, aux_ref, j - nja, slice(0, acc.shape[1])).astype(ob_ref.dtype)


def _norm_matmul_pair(x, gain, wa, wb, aux, aux_spec, *, tm, tna, nja, col_a, tnb, njb, col_b, ep_a, ep_b,
                      dtype_a, dtype_b, transpose_b=False):
    s, k = x.shape
    if transpose_b:
        b_spec = pl.BlockSpec((tnb, tm), lambda i, j: (jnp.maximum(j - nja, 0), i))
        b_shape = (njb * tnb, s)
    else:
        b_spec = pl.BlockSpec((tm, tnb), lambda i, j: (i, jnp.maximum(j - nja, 0)))
        b_shape = (s, njb * tnb)
    kern = functools.partial(_norm_matmul_pair_kernel, nja=nja, ep_a=ep_a, ep_b=ep_b)
    return pl.pallas_call(
        kern,
        grid=(s // tm, nja + njb),
        in_specs=[pl.BlockSpec((tm, k), lambda i, j: (i, 0)),
                  pl.BlockSpec((1, k), lambda i, j: (0, 0)),
                  _w_spec(wa, k, tna, lambda j: col_a + jnp.minimum(j, nja - 1)),
                  _w_spec(wb, k, tnb, lambda j: col_b + jnp.maximum(j - nja, 0)),
                  aux_spec],
        out_specs=[pl.BlockSpec((tm, tna), lambda i, j: (i, jnp.minimum(j, nja - 1))), b_spec],
        out_shape=[jax.ShapeDtypeStruct((s, nja * tna), dtype_a), jax.ShapeDtypeStruct(b_shape, dtype_b)],
        scratch_shapes=[pltpu.VMEM((tm, k), BF16)],
        compiler_params=_cparams(("arbitrary", "arbitrary")),
    )(x, gain.reshape(1, k), _w_array(wa), _w_array(wb), aux)


def _ep_swiglu(accs, aux, cols):
    g, u = accs
    return g * jax.nn.sigmoid(g) * u


def _ep_ple(accs, aux, cols):
    p_ref, wp_ref, x_ref = aux
    width = cols.stop - cols.start
    col0 = pl.multiple_of(pl.program_id(1) * wp_ref.shape[1] + cols.start, width)
    pp = _dot(p_ref[...].astype(BF16), wp_ref[:, cols].astype(BF16))
    return x_ref[:, pl.ds(col0, width)] + pp * jax.nn.sigmoid(accs[0])


def _ep_identity(acc, aux_ref, j, cols):
    return acc


def _ep_gates_t(acc, aux_ref, j, cols):
    return jax.nn.sigmoid(acc).T


def _ep_q(acc, gain_ref, j, cols):
    return _group_rms(acc, gain_ref[:, cols])


def _ep_kv(acc, gain_ref, j, cols):
    normed = _group_rms(acc, gain_ref[0:1, cols])
    return jnp.where((j % 2) == 0, normed, acc)


def _matmul_res_kernel(a_ref, w_ref, x_ref, o_ref):
    o_ref[...] = x_ref[...] + _dot(a_ref[...], w_ref[...].astype(BF16))


def _matmul_res(a, w, resid, *, tm, tn):
    s, k = a.shape
    n = _w_array(w).shape[-1]
    return pl.pallas_call(
        _matmul_res_kernel,
        grid=(s // tm, n // tn),
        in_specs=[pl.BlockSpec((tm, k), lambda i, j: (i, 0)),
                  _w_spec(w, k, tn, lambda j: j),
                  pl.BlockSpec((tm, tn), lambda i, j: (i, j))],
        out_specs=pl.BlockSpec((tm, tn), lambda i, j: (i, j)),
        out_shape=jax.ShapeDtypeStruct((s, n), F32),
        compiler_params=_cparams(("arbitrary", "arbitrary")),
    )(a, _w_array(w), resid)


GMLP_TN = 512
PROJ_TN = 1024


def _gmlp_kernel(x_ref, g_ref, win_ref, nv_ref, ws_ref, bs_ref, wo_ref, o_ref, h_scr, z_scr, *, tm):
    y_scr = h_scr
    width = GMLP_GROUPS * LANE
    n_in = 2 * width // GMLP_TN
    per_tile = GMLP_TN // LANE
    n_chunk = tm // GMLP_CHUNK
    j = pl.program_id(1)

    def z_tile(h):
        z_scr[j] = jax.nn.gelu(_dot(h, win_ref[...].astype(BF16)))

    @pl.when(j == 0)
    def _():
        h = _rms_rows(x_ref[...], g_ref[...]).astype(BF16)
        h_scr[...] = h
        z_tile(h)

    @pl.when((j > 0) & (j < n_in))
    def _():
        z_tile(h_scr[...])

    @pl.when(j == n_in)
    def _():
        v_tiles = range(n_in // 2, n_in)
        ms = sum(jnp.sum(z_scr[t] * z_scr[t], axis=-1, keepdims=True) for t in v_tiles) / width
        inv = lax.rsqrt(ms + EPS)
        row = lax.broadcasted_iota(jnp.int32, (GMLP_CHUNK, GMLP_CHUNK), 0)
        col = lax.broadcasted_iota(jnp.int32, (GMLP_CHUNK, GMLP_CHUNK), 1)
        causal = col <= row
        for g in range(GMLP_GROUPS):
            t, ls = g // per_tile, slice((g % per_tile) * LANE, (g % per_tile + 1) * LANE)
            cs = slice(g * LANE, (g + 1) * LANE)
            vn = (z_scr[n_in // 2 + t, :, ls] * inv * nv_ref[:, cs]).astype(BF16)
            ws = jnp.where(causal, ws_ref[g], 0.0).astype(BF16)
            vg = jnp.concatenate(
                [vn[c * GMLP_CHUNK:(c + 1) * GMLP_CHUNK, :] for c in range(n_chunk)], axis=1)
            sv = _dot(ws, vg) + bs_ref[:, g:g + 1]
            for c in range(n_chunk):
                rs = slice(c * GMLP_CHUNK, (c + 1) * GMLP_CHUNK)
                y_scr[rs, cs] = (z_scr[t, rs, ls] * sv[:, c * LANE:(c + 1) * LANE]).astype(BF16)

    @pl.when(j >= n_in)
    def _():
        col0 = pl.multiple_of((j - n_in) * GMLP_TN, GMLP_TN)
        o_ref[...] = x_ref[:, pl.ds(col0, GMLP_TN)] + _dot(y_scr[...], wo_ref[...].astype(BF16))


def _gmlp(x, gain, w_in, norm_v, w_s, b_s_t, w_out, layer, *, tm):
    s, d = x.shape
    width = GMLP_GROUPS * LANE
    assert width == d
    n_in = 2 * width // GMLP_TN
    n_out = d // GMLP_TN
    return pl.pallas_call(
        functools.partial(_gmlp_kernel, tm=tm),
        grid=(s // tm, n_in + n_out),
        in_specs=[pl.BlockSpec((tm, d), lambda i, j: (i, 0)),
                  pl.BlockSpec((1, d), lambda i, j: (0, 0)),
                  pl.BlockSpec((None, d, GMLP_TN), lambda i, j: (layer, 0, jnp.minimum(j, n_in - 1))),
                  pl.BlockSpec((1, width), lambda i, j: (0, 0)),
                  pl.BlockSpec((None, GMLP_GROUPS, GMLP_CHUNK, GMLP_CHUNK), lambda i, j: (layer, 0, 0, 0)),
                  pl.BlockSpec((GMLP_CHUNK, GMLP_GROUPS), lambda i, j: (0, 0)),
                  pl.BlockSpec((None, width, GMLP_TN), lambda i, j: (layer, 0, jnp.maximum(j - n_in, 0)))],
        out_specs=pl.BlockSpec((tm, GMLP_TN), lambda i, j: (i, jnp.maximum(j - n_in, 0))),
        out_shape=jax.ShapeDtypeStruct((s, d), F32),
        scratch_shapes=[pltpu.VMEM((tm, d), BF16),
                        pltpu.VMEM((n_in, tm, GMLP_TN), F32)],
        compiler_params=_cparams(("arbitrary", "arbitrary")),
    )(x, gain.reshape(1, d), w_in, norm_v.reshape(1, width), w_s, b_s_t, w_out)


def _compress_kernel(kv_ref, pe_ref, w1_ref, w2_ref, kn_ref, o_ref, *, n_half):
    half_w = CMP_STRIDE * HEAD_DIM
    hh = jnp.concatenate([kv_ref[pl.ds(r, n_half, stride=CMP_STRIDE), :] for r in range(CMP_STRIDE)],
                         axis=1)
    pe = pe_ref[...]
    a = _dot((hh + pe[0:1, :]).astype(BF16), w1_ref[0:half_w, :].astype(BF16))
    b = _dot((hh + pe[1:2, :]).astype(BF16), w1_ref[half_w:2 * half_w, :].astype(BF16))
    pre = a + pltpu.roll(b, n_half - 1, 0)
    out = _dot(jax.nn.gelu(pre).astype(BF16), w2_ref[...].astype(BF16))
    is_key = pl.program_id(0) < N_KV_GROUPS
    out = jnp.where(is_key, _rms_rows(out, kn_ref[...]), out)
    row = lax.broadcasted_iota(jnp.int32, out.shape, 0)
    o_ref[...] = jnp.where(row < n_half - 1, out, 0.0)


def _compress(kvc, pe, w1, w2, k_norm0):
    s = kvc.shape[0]
    n_half = s // CMP_STRIDE
    half_w = CMP_STRIDE * HEAD_DIM
    hid = w1.shape[2]
    return pl.pallas_call(
        functools.partial(_compress_kernel, n_half=n_half),
        grid=(2 * N_KV_GROUPS,),
        in_specs=[pl.BlockSpec((s, HEAD_DIM), lambda n: (0, n)),
                  pl.BlockSpec((None, 2, half_w), lambda n: (n // N_KV_GROUPS, 0, 0)),
                  pl.BlockSpec((None, 2 * half_w, hid), lambda n: (n // N_KV_GROUPS, 0, 0)),
                  pl.BlockSpec((None, hid, HEAD_DIM), lambda n: (n // N_KV_GROUPS, 0, 0)),
                  pl.BlockSpec((1, HEAD_DIM), lambda n: (0, 0))],
        out_specs=pl.BlockSpec((None, n_half, HEAD_DIM), lambda n: (n, 0, 0)),
        out_shape=jax.ShapeDtypeStruct((2 * N_KV_GROUPS, n_half, HEAD_DIM), F32),
        compiler_params=_cparams(("arbitrary",)),
    )(kvc, pe, w1, w2, k_norm0.reshape(1, HEAD_DIM))


def _t5_bucket_np(dist):
    n = np.maximum(dist, 0)
    max_exact = N_BUCKETS // 2
    nf = np.maximum(n, 1).astype(np.float32)
    large = max_exact + (np.log(nf / np.float32(max_exact)) / np.float32(math.log(MAX_DISTANCE / max_exact))
                         * np.float32(N_BUCKETS - max_exact)).astype(np.int32)
    large = np.minimum(large, N_BUCKETS - 1)
    return np.where(n < max_exact, n, large).astype(np.int32)


N_PATTERNS = 4


def _bucket_patterns():
    i = np.arange(Q_BLOCK)[:, None]
    c = np.arange(LANE)[None, :]
    d0 = i - c
    d1 = i - c + Q_BLOCK
    dc = i - CMP_STRIDE * (c - (LANE - 8)) - (CMP_BLOCK - 1)
    pats = [np.where(d >= 0, _t5_bucket_np(d), -1) for d in (d0, d1, dc)]
    pats.append(np.where(i < c, N_BUCKETS - 1, -1))
    return np.stack([p.T for p in pats]).astype(np.int32)


def _bias_tiles_kernel(tab_ref, pat_ref, o_ref):
    h = pl.program_id(0)
    pat = pat_ref[...]
    far = tab_ref[N_BUCKETS - 1, h]
    acc = jnp.full(pat.shape, NEG, F32)
    for b in range(N_BUCKETS):
        acc = jnp.where(pat == b, (tab_ref[b, h] - far) * LOG2E, acc)
    o_ref[...] = acc


def _bias_tiles(rel_bias):
    pats = jnp.asarray(_bucket_patterns())
    return pl.pallas_call(
        _bias_tiles_kernel,
        grid=(N_HEADS,),
        in_specs=[pl.BlockSpec(memory_space=pltpu.SMEM),
                  pl.BlockSpec((N_PATTERNS, Q_BLOCK, LANE), lambda h: (0, 0, 0))],
        out_specs=pl.BlockSpec((None, N_PATTERNS, Q_BLOCK, LANE), lambda h: (h, 0, 0, 0)),
        out_shape=jax.ShapeDtypeStruct((N_HEADS, N_PATTERNS, Q_BLOCK, LANE), F32),
        compiler_params=_cparams(("arbitrary",)),
    )(rel_bias, pats)


def _with_features(qs, feat):
    reps = qs.shape[0] // feat.shape[0]
    return jnp.concatenate([qs, jnp.concatenate([feat] * reps, axis=0)], axis=1)


def _softmax_cols(blocks):
    m = blocks[0].max(axis=0, keepdims=True)
    for b in blocks[1:]:
        m = jnp.maximum(m, b.max(axis=0, keepdims=True))
    es = [jnp.exp2(b - m) for b in blocks]
    den = es[0].sum(axis=0, keepdims=True)
    for e in es[1:]:
        den = den + e.sum(axis=0, keepdims=True)
    return es, m, den


def _nsa_kernel(q_ref, gt_ref, kc_ref, vc_ref, vct_ref, ovl_ref, ovlt_ref, ks_ref, vsf_ref, vsn_ref,
                kw_ref, vwn_ref, tb_ref, o_ref, m_scr, acc_scr, sa_scr, sb_scr, cmp_o_scr, cmp_pf_scr, cmp_pn_scr,
                *, n_slc):
    qb = pl.program_id(1)
    s = qb * Q_BLOCK
    hpg = HEADS_PER_GROUP

    q_all = q_ref[...]
    qs = jnp.concatenate([q_all[:, h * LANE:(h + 1) * LANE] for h in range(hpg)], axis=0)
    cols = [slice(h * Q_BLOCK, (h + 1) * Q_BLOCK) for h in range(hpg)]
    lane_f = lax.broadcasted_iota(jnp.int32, (Q_BLOCK, LANE), 1)
    pad_feat = jnp.where(lane_f == LANE - 1, 1.0, 0.0).astype(BF16)
    q_pad = _with_features(qs, pad_feat)

    n_sub = NEAR // LANE
    row0 = pl.multiple_of(s, Q_BLOCK)

    n_pair = hpg // 2
    pair_rows = [slice(pr * 2 * Q_BLOCK, (pr + 1) * 2 * Q_BLOCK) for pr in range(n_pair)]
    half = [slice(0, Q_BLOCK), slice(Q_BLOCK, 2 * Q_BLOCK)]

    def run_stages(stages):
        pending = stages[0][0]()
        for i, (_, consume) in enumerate(stages):
            cur = pending
            if i + 1 < len(stages):
                pending = stages[i + 1][0]()
            consume(cur)

    def near_stages(k_ref, q_aug, vn_ref, oldest_pat, res):
        k_aug = k_ref[pl.ds(row0, NEAR), :]
        v_blocks = vn_ref[pl.ds(qb, n_sub)]
        v_t = jnp.concatenate([v_blocks[u] for u in range(n_sub)], axis=1)

        def stage(pr):
            def issue():
                return _dot_nt(k_aug, q_aug[pair_rows[pr], :])

            def consume(logits):
                ps = []
                for hh in range(2):
                    h = 2 * pr + hh
                    blocks = [logits[u * LANE:(u + 1) * LANE, half[hh]] for u in range(n_sub)]
                    blocks[n_sub - 1] = blocks[n_sub - 1] + tb_ref[h, 0]
                    blocks[n_sub - 2] = blocks[n_sub - 2] + tb_ref[h, 1]
                    if oldest_pat is not None:
                        blocks[0] = blocks[0] + tb_ref[h, oldest_pat]
                    m = blocks[0].max(axis=0, keepdims=True)
                    for b in blocks[1:]:
                        m = jnp.maximum(m, b.max(axis=0, keepdims=True))
                    ps.append(jnp.concatenate([jnp.exp2((b - m).astype(BF16)) for b in blocks], axis=0))
                    res["m"].append(m)
                res["o"].append(_dot(v_t, jnp.concatenate(ps, axis=1)))
            return issue, consume
        return [stage(pr) for pr in range(n_pair)]

    near0 = pl.multiple_of(qb * 8 + 8, 8)
    kn = kc_ref[pl.ds(near0, LANE), :].astype(BF16)
    vn_t = vc_ref[pl.ds(near0, LANE), :].T.astype(BF16)
    ovl_n_t = ovl_ref[pl.ds(near0, LANE), :].T.astype(BF16)
    far_feat = jnp.where(lane_f > qb - 16, 1.0, 0.0).astype(BF16)
    q_cmp = _with_features(qs, far_feat)
    t_row = s + lax.broadcasted_iota(jnp.int32, (1, Q_BLOCK), 1)
    row_ok = t_row >= CMP_BLOCK - 1
    n_half = kc_ref.shape[0] - CMP_PAD

    def cmp_branch(n_keys):
        kc = kc_ref[CMP_PAD:CMP_PAD + n_keys, :].astype(BF16)
        vc_t = vct_ref[:, CMP_PAD:CMP_PAD + n_keys]
        res = {"o": [], "pf": jnp.zeros((n_keys, Q_BLOCK), F32), "pn": jnp.zeros((LANE, Q_BLOCK), F32)}
        heads = hpg * LANE // min(n_keys, hpg * LANE // 2)

        def stage(first):
            q_rows = slice(first * Q_BLOCK, (first + heads) * Q_BLOCK)

            def issue():
                return _dot_nt(kc, q_cmp[q_rows, :]), _dot_nt(kn, q_pad[q_rows, :])

            def consume(logits):
                sf, sn = logits
                pfs, pns = [], []
                for hh in range(heads):
                    (ef, en), _, den = _softmax_cols([sf[:, cols[hh]], sn[:, cols[hh]] + tb_ref[first + hh, 2]])
                    inv = jnp.where(row_ok, 1.0 / den, 0.0)
                    pf = ef * inv
                    pn = en * inv
                    res["pf"] = res["pf"] + pf
                    res["pn"] = res["pn"] + pn
                    pfs.append(pf.astype(BF16))
                    pns.append(pn.astype(BF16))
                res["o"].append(_dot(vc_t, jnp.concatenate(pfs, axis=1)) + _dot(vn_t, jnp.concatenate(pns, axis=1)))
            return issue, consume

        run_stages([stage(first) for first in range(0, hpg, heads)])
        cmp_o_scr[...] = jnp.concatenate(res["o"], axis=1)
        cmp_pf_scr[0:n_keys, :] = res["pf"]
        if n_keys < n_half:
            cmp_pf_scr[n_keys:, :] = jnp.zeros((n_half - n_keys, Q_BLOCK), F32)
        cmp_pn_scr[...] = res["pn"]

    sizes = sorted({min(n_half, c) for c in (LANE, 2 * LANE)} | {n_half})
    lo = 0
    for idx, n_keys in enumerate(sizes):
        last = idx + 1 == len(sizes)
        hi = (n_keys + LANE - 8) // 8 + 1
        pl.when((qb >= lo) if last else ((qb >= lo) & (qb < hi)))(functools.partial(cmp_branch, n_keys))
        lo = hi

    blk = lax.broadcasted_iota(jnp.int32, (n_slc, Q_BLOCK), 0)
    blk_f = blk.astype(F32)
    cur = (s + lax.broadcasted_iota(jnp.int32, (n_slc, Q_BLOCK), 1)) >> 6
    forced = (blk == 0) | (blk == cur) | (blk == cur - 1)
    valid = blk <= cur
    topk = {}

    def importance_stage():
        def consume(_):
            psum_f, psum_n = cmp_pf_scr[...], cmp_pn_scr[...]
            ovl_t = ovlt_ref[:, CMP_PAD:]
            pf_hi = psum_f.astype(BF16)
            pf_lo = (psum_f - pf_hi.astype(F32)).astype(BF16)
            pn_hi = psum_n.astype(BF16)
            pn_lo = (psum_n - pn_hi.astype(F32)).astype(BF16)
            imp_t = _dot(ovl_t, pf_hi) + _dot(ovl_t, pf_lo) + _dot(ovl_n_t, pn_hi) + _dot(ovl_n_t, pn_lo)
            topk["score"] = jnp.where(valid & ~forced, imp_t, -BIG)
            topk["sel"] = jnp.where(forced, 1.0, 0.0)
        return (lambda: None), consume

    def topk_stage(rounds):
        def consume(_):
            score, sel_t = topk["score"], topk["sel"]
            for _r in range(rounds):
                top = jnp.max(score, axis=0, keepdims=True)
                first = jnp.min(jnp.where(score == top, blk_f, float(n_slc)), axis=0, keepdims=True)
                pick = blk_f == first
                sel_t = jnp.where(pick, 1.0, sel_t)
                score = jnp.where(pick, -2.0 * BIG, score)
            topk["score"], topk["sel"] = score, sel_t
        return (lambda: None), consume

    win_res = {"o": [], "m": []}
    win_stages = near_stages(kw_ref, q_pad, vwn_ref, 3, win_res)
    free_picks = max(min(N_SELECT, n_slc) - 3, 0)
    rounds = [free_picks // n_pair + (1 if pr < free_picks % n_pair else 0) for pr in range(n_pair)]
    stages = [importance_stage()]
    for pr in range(n_pair):
        stages += [win_stages[pr], topk_stage(rounds[pr])]
    run_stages(stages)
    o_win = jnp.concatenate(win_res["o"], axis=1)
    o_cmp = cmp_o_scr[...]
    sel_t = topk["sel"]
    drop_t = jnp.where(valid, 1.0 - sel_t, 1.0)
    near_blk = (s - WINDOW) >> 6
    drop_far_t = jnp.where(blk >= near_blk, 1.0, drop_t)
    q_near = _with_features(qs, drop_t.T.astype(BF16))
    q_far = _with_features(qs, drop_far_t.T.astype(BF16))

    def far_logits(t):
        r0 = pl.multiple_of(WINDOW + KV_TILE * t, KV_TILE)
        return _dot_nt(ks_ref[pl.ds(r0, KV_TILE), :], q_far)

    sa_scr[...] = far_logits(0)
    slc_res = {"o": [], "m": []}
    run_stages(near_stages(ks_ref, q_near, vsn_ref, None, slc_res))
    m_scr[...] = jnp.concatenate(slc_res["m"], axis=1)
    acc_scr[...] = jnp.concatenate(slc_res["o"], axis=1)

    gates = gt_ref[...]
    g_c, g_s, g_w = [jnp.concatenate([gates[3 * h + br:3 * h + br + 1, :] for h in range(hpg)], axis=1)
                     for br in range(N_BRANCH)]
    out_cw = g_c * o_cmp + (g_w / o_win[HEAD_DIM:HEAD_DIM + 1, :]) * o_win[:HEAD_DIM, :]

    n_far = (jnp.maximum(s - WINDOW, 0) + KV_TILE - 1) // KV_TILE

    n_tiles = vsf_ref.shape[0]

    def far_probs(st_ref):
        m_old = m_scr[...]
        sts = [st_ref[:, cols[h]] for h in range(hpg)]
        m_new = jnp.concatenate(
            [jnp.maximum(m_old[:, cols[h]], sts[h].max(axis=0, keepdims=True)) for h in range(hpg)], axis=1)
        p_t = jnp.concatenate(
            [jnp.exp2((sts[h] - m_new[:, cols[h]]).astype(BF16)) for h in range(hpg)], axis=1)
        m_scr[...] = m_new
        return p_t, jnp.exp2(m_old - m_new)

    def far_accumulate(t, p_t, alpha):
        acc_scr[...] = alpha * acc_scr[...] + _dot(vsf_ref[t], p_t)

    def far_body(i, carry):
        t0 = 2 * i
        sb_scr[...] = far_logits(t0 + 1)
        p_a, alpha_a = far_probs(sa_scr)
        sa_scr[...] = far_logits(jnp.minimum(t0 + 2, n_tiles - 1))
        far_accumulate(t0, p_a, alpha_a)
        p_b, alpha_b = far_probs(sb_scr)
        far_accumulate(t0 + 1, p_b, alpha_b)
        return carry

    lax.fori_loop(0, n_far // 2, far_body, 0)

    @pl.when(n_far % 2 == 1)
    def _():
        p_last, alpha_last = far_probs(sa_scr)
        far_accumulate(n_far - 1, p_last, alpha_last)

    out_t = out_cw + (g_s / acc_scr[HEAD_DIM:HEAD_DIM + 1, :]) * acc_scr[:HEAD_DIM, :]
    for h in range(hpg):
        o_ref[:, cols[h]] = out_t[:, cols[h]].T.astype(o_ref.dtype)


def _nsa_attention(q, gates_t, k_cmp, v_cmp, v_cmp_t, ovl, ovl_t, ks, vs_far, vs_near, kw, vw_near, bias_tiles):
    s = q.shape[0]
    n_qb = s // Q_BLOCK
    n_slc = s // SLC_BLOCK
    gw = HEADS_PER_GROUP * HEAD_DIM
    lanes = HEADS_PER_GROUP * Q_BLOCK

    def group_spec(a):
        zeros = (0,) * (a.ndim - 1)
        return pl.BlockSpec((None,) + a.shape[1:], lambda g, i: (g,) + zeros)

    def whole_spec(a):
        zeros = (0,) * a.ndim
        return pl.BlockSpec(a.shape, lambda g, i: zeros)

    return pl.pallas_call(
        functools.partial(_nsa_kernel, n_slc=n_slc),
        grid=(N_KV_GROUPS, n_qb),
        in_specs=[pl.BlockSpec((Q_BLOCK, gw), lambda g, i: (i, g)),
                  pl.BlockSpec((LANE, Q_BLOCK), lambda g, i: (g, i)),
                  group_spec(k_cmp), group_spec(v_cmp), group_spec(v_cmp_t),
                  whole_spec(ovl), whole_spec(ovl_t),
                  group_spec(ks), group_spec(vs_far), group_spec(vs_near),
                  group_spec(kw), group_spec(vw_near),
                  pl.BlockSpec((HEADS_PER_GROUP, N_PATTERNS, LANE, Q_BLOCK), lambda g, i: (g, 0, 0, 0))],
        out_specs=pl.BlockSpec((Q_BLOCK, gw), lambda g, i: (i, g)),
        out_shape=jax.ShapeDtypeStruct((s, N_KV_GROUPS * gw), BF16),
        scratch_shapes=[pltpu.VMEM((1, lanes), F32),
                        pltpu.VMEM((HEAD_DIM + SUM_ROWS, lanes), F32),
                        pltpu.VMEM((KV_TILE, lanes), F32),
                        pltpu.VMEM((KV_TILE, lanes), F32),
                        pltpu.VMEM((HEAD_DIM, lanes), F32),
                        pltpu.VMEM((k_cmp.shape[1] - CMP_PAD, Q_BLOCK), F32),
                        pltpu.VMEM((LANE, Q_BLOCK), F32)],
        compiler_params=_cparams(("arbitrary", "arbitrary")),
    )(q, gates_t, k_cmp, v_cmp, v_cmp_t, ovl, ovl_t, ks, vs_far, vs_near, kw, vw_near, bias_tiles)


def _cmp_mask_columns(n_half):
    assert n_half // 8 < LANE - 1
    out = np.zeros((CMP_PAD + n_half, LANE), np.float32)
    out[np.arange(CMP_PAD), LANE - 1] = NEG
    k = np.arange(n_half)
    out[CMP_PAD + k, k // 8] = NEG
    return out


def _slc_mask_columns(s):
    n_slc = s // SLC_BLOCK
    assert n_slc - 1 > (WINDOW + Q_BLOCK) // SLC_BLOCK
    out = np.zeros((WINDOW + s, n_slc), np.float32)
    out[np.arange(WINDOW), n_slc - 1] = NEG
    pos = np.arange(s)
    out[WINDOW + pos, pos // SLC_BLOCK] = NEG
    return out


def _pad_mask_columns(s):
    out = np.zeros((WINDOW + s, LANE), np.float32)
    out[np.arange(WINDOW), LANE - 1] = NEG
    return out


def _overlap_padded(n_half, n_slc):
    n_cmp = n_half - 1
    c0 = np.arange(n_cmp) * CMP_STRIDE
    s0 = np.arange(n_slc) * SLC_BLOCK
    lo = np.maximum(c0[:, None], s0[None, :])
    hi = np.minimum(c0[:, None] + CMP_BLOCK, s0[None, :] + SLC_BLOCK)
    ovl = np.maximum(hi - lo, 0).astype(np.float32) / CMP_BLOCK
    out = np.zeros((CMP_PAD + n_half, n_slc), np.float32)
    out[CMP_PAD:CMP_PAD + n_cmp] = ovl
    return out


def _row_tile(s):
    return min(1024, s)


def _ffn_ple(x, p, layer, b, norm_ffn, w_in, w_out, norm_ple, ple_w, ple_gate):
    s = x.shape[0]
    tm = _row_tile(s)
    tn = 512
    nj = FFN_DIM // tn
    act = _norm_matmul(x, norm_ffn, (w_in, layer), (0, nj), tm=tm, tn=tn, nj=nj,
                       epilogue=_ep_swiglu, out_dtype=BF16)
    x = _matmul_res(act, (w_out, layer), x, tm=tm, tn=512)
    aux = (p, ple_w)
    tn_ple = PROJ_TN
    aux_specs = (pl.BlockSpec((None, None, tm, PLE_DIM), lambda i, j: (layer, b, i, 0)),
                 pl.BlockSpec((None, PLE_DIM, tn_ple), lambda i, j: (layer, 0, j)))
    return _norm_matmul(x, norm_ple, (ple_gate, layer), (0,), tm=tm, tn=tn_ple, nj=D_MODEL // tn_ple,
                        epilogue=_ep_ple, out_dtype=F32, aux=aux, aux_specs=aux_specs)


def _gmlp_layer(x, layer, norm_mix, w_in, norm_v, w_s, b_s, w_out):
    return _gmlp(x, norm_mix, w_in, norm_v, w_s, jnp.transpose(b_s), w_out, layer, tm=_row_tile(x.shape[0]))


def _shared_kv(x, kv_norm, kv_w, k_norm, cmp_pe_k, cmp_pe_v, cmp_wk1, cmp_wk2, cmp_wv1, cmp_wv2):
    s = x.shape[0]
    tm = _row_tile(s)
    gw = N_KV_GROUPS * HEAD_DIM
    gains = jnp.stack([jnp.tile(k_norm[1], N_KV_GROUPS), jnp.ones((gw,), F32),
                       jnp.tile(k_norm[2], N_KV_GROUPS), jnp.ones((gw,), F32)])
    gains = jnp.broadcast_to(gains[:, None, :], (4, 8, gw))
    kvc, kvr = _norm_matmul_pair(
        x, kv_norm, kv_w, kv_w, gains, pl.BlockSpec((None, 8, gw), lambda i, j: (jnp.maximum(j - 1, 0), 0, 0)),
        tm=tm, tna=2 * gw, nja=1, col_a=0, tnb=gw, njb=4, col_b=2,
        ep_a=_ep_identity, ep_b=_ep_kv, dtype_a=F32, dtype_b=BF16)
    n_half = s // CMP_STRIDE
    pe = jnp.stack([cmp_pe_k, cmp_pe_v]).reshape(2, 2, CMP_STRIDE * HEAD_DIM)
    kv_cmp = _compress(kvc, pe, jnp.stack([cmp_wk1, cmp_wv1]), jnp.stack([cmp_wk2, cmp_wv2]), k_norm[0])
    kv_cmp = jnp.pad(kv_cmp, ((0, 0), (CMP_PAD, 0), (0, 0)))
    cmp_cols = jnp.broadcast_to(jnp.asarray(_cmp_mask_columns(n_half)), (N_KV_GROUPS, CMP_PAD + n_half, LANE))
    k_cmp = jnp.concatenate([kv_cmp[:N_KV_GROUPS], cmp_cols], axis=2)
    v_cmp = kv_cmp[N_KV_GROUPS:]
    v_cmp_t = jnp.transpose(v_cmp, (0, 2, 1)).astype(BF16)
    s_pad = WINDOW + s
    kvr = jnp.pad(kvr, ((WINDOW, 0), (0, 0))).reshape(s_pad, 4, N_KV_GROUPS, HEAD_DIM)
    kvr = jnp.transpose(kvr, (1, 2, 0, 3))
    slc_cols = jnp.asarray(_slc_mask_columns(s)).astype(BF16)
    pad_cols = jnp.asarray(_pad_mask_columns(s)).astype(BF16)
    ks = jnp.concatenate([kvr[0], jnp.broadcast_to(slc_cols, (N_KV_GROUPS,) + slc_cols.shape)], axis=2)
    kw = jnp.concatenate([kvr[2], jnp.broadcast_to(pad_cols, (N_KV_GROUPS,) + pad_cols.shape)], axis=2)

    def key_tiles_t(v, tile):
        v_t = jnp.transpose(v.reshape(N_KV_GROUPS, -1, tile, HEAD_DIM), (0, 1, 3, 2))
        return jnp.concatenate([v_t, jnp.ones(v_t.shape[:2] + (SUM_ROWS, tile), v_t.dtype)], axis=2)

    vs_far = key_tiles_t(kvr[1][:, WINDOW:], KV_TILE)
    return k_cmp, v_cmp, v_cmp_t, ks, vs_far, key_tiles_t(kvr[1], LANE), kw, key_tiles_t(kvr[3], LANE)


def _nsa_layer(x, layer, norm_mix, w_in, q_norm, w_out, kvs):
    s = x.shape[0]
    tm = _row_tile(s)
    nq = N_HEADS * HEAD_DIM
    scale = HEAD_DIM ** -0.5 * LOG2E
    q_gain = jnp.tile(q_norm * scale, PROJ_TN // HEAD_DIM).reshape(1, PROJ_TN)
    w_gate = w_in[layer, :, nq:].reshape(D_MODEL, N_KV_GROUPS, HEADS_PER_GROUP * N_BRANCH)
    w_gate = jnp.pad(w_gate, ((0, 0), (0, 0), (0, LANE - HEADS_PER_GROUP * N_BRANCH)))
    w_gate = w_gate.reshape(D_MODEL, N_KV_GROUPS * LANE)
    q, gates_t = _norm_matmul_pair(
        x, norm_mix, (w_in, layer), w_gate, q_gain, pl.BlockSpec((1, PROJ_TN), lambda i, j: (0, 0)),
        tm=tm, tna=PROJ_TN, nja=nq // PROJ_TN, col_a=0, tnb=N_KV_GROUPS * LANE, njb=1, col_b=0,
        ep_a=_ep_q, ep_b=_ep_gates_t, dtype_a=BF16, dtype_b=F32, transpose_b=True)
    o = _nsa_attention(q, gates_t, *kvs)
    return _matmul_res(o, (w_out, layer), x, tm=tm, tn=PROJ_TN)


def kernel(x, p, norm_mix, norm_ffn, norm_ple, a_w_in, a_norm_v, a_w_s, a_b_s, a_w_out, kv_norm, kv_w, k_norm, cmp_pe_k, cmp_pe_v, cmp_wk1, cmp_wk2, cmp_wv1, cmp_wv2, b_w_in, b_q_norm, b_w_out, rel_bias, ffn_w_in, ffn_w_out, ple_w, ple_gate):
    batch, s, d = x.shape
    depth = norm_mix.shape[0]
    n_a = a_w_in.shape[0]
    a_w_out, b_w_out, kv_w, ple_gate, ffn_w_out = (
        w.astype(BF16) for w in (a_w_out, b_w_out, kv_w, ple_gate, ffn_w_out))
    outs = []
    for b in range(batch):
        xb = x.reshape(s, d) if batch == 1 else x[b]
        kvs = None
        for i in range(depth):
            if i < n_a:
                xb = _gmlp_layer(xb, i, norm_mix[i], a_w_in, a_norm_v[i], a_w_s, a_b_s[i], a_w_out)
            else:
                j = i - n_a
                xb = _nsa_layer(xb, j, norm_mix[i], b_w_in, b_q_norm[j], b_w_out, kvs)
            xb = _ffn_ple(xb, p, i, b, norm_ffn[i], ffn_w_in, ffn_w_out,
                          norm_ple[i], ple_w, ple_gate)
            if i == n_a - 1:
                k_cmp, v_cmp, v_cmp_t, ks, vs_far, vs_near, kw, vw_near = _shared_kv(
                    xb, kv_norm, kv_w, k_norm, cmp_pe_k, cmp_pe_v, cmp_wk1, cmp_wk2, cmp_wv1, cmp_wv2)
                ovl = _overlap_padded(s // CMP_STRIDE, s // SLC_BLOCK)
                kvs = (k_cmp, v_cmp, v_cmp_t, jnp.asarray(ovl), jnp.asarray(ovl.T).astype(BF16),
                       ks, vs_far, vs_near, kw, vw_near, _bias_tiles(rel_bias))
        outs.append(xb)
    return outs[0].reshape(1, s, d) if batch == 1 else jnp.stack(outs)
```

```python
import functools
import math

import numpy as np
import jax
import jax.numpy as jnp
from jax import lax
from jax.experimental import pallas as pl
from jax.experimental.pallas import tpu as pltpu

F32 = jnp.float32
BF16 = jnp.bfloat16

D_MODEL = 2048
PLE_DIM = 256
FFN_DIM = 5632
GMLP_CHUNK = 128
GMLP_GROUPS = 16
HEAD_DIM = 128
N_HEADS = 16
N_KV_GROUPS = 2
HEADS_PER_GROUP = 8
N_BRANCH = 3
CMP_BLOCK = 32
CMP_STRIDE = 16
SLC_BLOCK = 64
N_SELECT = 16
WINDOW = 512
Q_BLOCK = 128
N_BUCKETS = 32
MAX_DISTANCE = 128
EPS = 1e-6
NEG = -1e30
BIG = 1e30
LOG2E = math.log2(math.e)

LANE = 128
KV_TILE = 512
NEAR = WINDOW + Q_BLOCK
CMP_PAD = 128
SUM_ROWS = 16
VMEM_LIMIT = 56 * 1024 * 1024


def _cparams(sem):
    return pltpu.CompilerParams(dimension_semantics=sem, vmem_limit_bytes=VMEM_LIMIT)


def _dot(a, b):
    return jnp.dot(a, b, preferred_element_type=F32)


def _dot_nt(a, b):
    return lax.dot_general(a, b, (((1,), (1,)), ((), ())), preferred_element_type=F32)


def _rms_rows(x, g):
    ms = jnp.mean(x * x, axis=-1, keepdims=True)
    return x * lax.rsqrt(ms + EPS) * g


def _group_rms(acc, gain):
    outs = []
    for c in range(acc.shape[1] // LANE):
        a = acc[:, c * LANE:(c + 1) * LANE]
        outs.append(_rms_rows(a, gain[:, c * LANE:(c + 1) * LANE]))
    return outs[0] if len(outs) == 1 else jnp.concatenate(outs, axis=1)


EPILOGUE_COLS = 256


def _column_chunks(tn):
    cw = min(tn, EPILOGUE_COLS)
    return [slice(c, c + cw) for c in range(0, tn, cw)]


def _norm_matmul_kernel(x_ref, g_ref, *refs, n_w, n_aux, epilogue, out_dtype):
    w_refs = refs[:n_w]
    aux_refs = refs[n_w:n_w + n_aux]
    o_ref = refs[n_w + n_aux]
    h_scr = refs[n_w + n_aux + 1]

    def column_tile(h):
        chunks = _column_chunks(o_ref.shape[1])
        accs = [[_dot(h, w_ref[:, cs].astype(BF16)) for w_ref in w_refs] for cs in chunks]
        for cs, acc in zip(chunks, accs):
            o_ref[:, cs] = epilogue(acc, aux_refs + (x_ref,), cs).astype(out_dtype)

    @pl.when(pl.program_id(1) == 0)
    def _():
        h = _rms_rows(x_ref[...], g_ref[...]).astype(BF16)
        h_scr[...] = h
        column_tile(h)

    @pl.when(pl.program_id(1) > 0)
    def _():
        column_tile(h_scr[...])


def _w_spec(w, k, tn, col_of):
    if isinstance(w, tuple):
        layer = w[1]
        return pl.BlockSpec((None, k, tn), lambda i, j: (layer, 0, col_of(j)))
    return pl.BlockSpec((k, tn), lambda i, j: (0, col_of(j)))


def _w_array(w):
    return w[0] if isinstance(w, tuple) else w


def _norm_matmul(x, gain, w, col_offsets, *, tm, tn, nj, epilogue, out_dtype,
                 aux=(), aux_specs=()):
    s, k = x.shape
    in_specs = [pl.BlockSpec((tm, k), lambda i, j: (i, 0)),
                pl.BlockSpec((1, k), lambda i, j: (0, 0))]
    for c0 in col_offsets:
        in_specs.append(_w_spec(w, k, tn, lambda j, c0=c0: c0 + j))
    in_specs += list(aux_specs)
    kern = functools.partial(_norm_matmul_kernel, n_w=len(col_offsets), n_aux=len(aux),
                             epilogue=epilogue, out_dtype=out_dtype)
    return pl.pallas_call(
        kern,
        grid=(s // tm, nj),
        in_specs=in_specs,
        out_specs=pl.BlockSpec((tm, tn), lambda i, j: (i, j)),
        out_shape=jax.ShapeDtypeStruct((s, nj * tn), out_dtype),
        scratch_shapes=[pltpu.VMEM((tm, k), BF16)],
        compiler_params=_cparams(("arbitrary", "arbitrary")),
    )(x, gain.reshape(1, k), *([_w_array(w)] * len(col_offsets)), *aux)


def _norm_matmul_pair_kernel(x_ref, g_ref, wa_ref, wb_ref, aux_ref, oa_ref, ob_ref, h_scr, *, nja, ep_a, ep_b):
    j = pl.program_id(1)

    def tile_a(h):
        chunks = _column_chunks(oa_ref.shape[1])
        accs = [_dot(h, wa_ref[:, cs].astype(BF16)) for cs in chunks]
        for cs, acc in zip(chunks, accs):
            oa_ref[:, cs] = ep_a(acc, aux_ref, j, cs).astype(oa_ref.dtype)

    @pl.when(j == 0)
    def _():
        h = _rms_rows(x_ref[...], g_ref[...]).astype(BF16)
        h_scr[...] = h
        tile_a(h)

    if nja > 1:
        @pl.when((j > 0) & (j < nja))
        def _():
            tile_a(h_scr[...])

    @pl.when(j >= nja)
    def _():
        acc = _dot(h_scr[...], wb_ref[...].astype(BF16))
        ob_ref[...] = ep_b(acc, aux_ref, j - nja, slice(0, acc.shape[1])).astype(ob_ref.dtype)


def _norm_matmul_pair(x, gain, wa, wb, aux, aux_spec, *, tm, tna, nja, col_a, tnb, njb, col_b, ep_a, ep_b,
                      dtype_a, dtype_b, transpose_b=False):
    s, k = x.shape
    if transpose_b:
        b_spec = pl.BlockSpec((tnb, tm), lambda i, j: (jnp.maximum(j - nja, 0), i))
        b_shape = (njb * tnb, s)
    else:
        b_spec = pl.BlockSpec((tm, tnb), lambda i, j: (i, jnp.maximum(j - nja, 0)))
        b_shape = (s, njb * tnb)
    kern = functools.partial(_norm_matmul_pair_kernel, nja=nja, ep_a=ep_a, ep_b=ep_b)
    return pl.pallas_call(
        kern,
        grid=(s // tm, nja + njb),
        in_specs=[pl.BlockSpec((tm, k), lambda i, j: (i, 0)),
                  pl.BlockSpec((1, k), lambda i, j: (0, 0)),
                  _w_spec(wa, k, tna, lambda j: col_a + jnp.minimum(j, nja - 1)),
                  _w_spec(wb, k, tnb, lambda j: col_b + jnp.maximum(j - nja, 0)),
                  aux_spec],
        out_specs=[pl.BlockSpec((tm, tna), lambda i, j: (i, jnp.minimum(j, nja - 1))), b_spec],
        out_shape=[jax.ShapeDtypeStruct((s, nja * tna), dtype_a), jax.ShapeDtypeStruct(b_shape, dtype_b)],
        scratch_shapes=[pltpu.VMEM((tm, k), BF16)],
        compiler_params=_cparams(("arbitrary", "arbitrary")),
    )(x, gain.reshape(1, k), _w_array(wa), _w_array(wb), aux)


def _ep_swiglu(accs, aux, cols):
    g, u = accs
    return g * jax.nn.sigmoid(g) * u


def _ep_ple(accs, aux, cols):
    p_ref, wp_ref, x_ref = aux
    width = cols.stop - cols.start
    col0 = pl.multiple_of(pl.program_id(1) * wp_ref.shape[1] + cols.start, width)
    pp = _dot(p_ref[...].astype(BF16), wp_ref[:, cols].astype(BF16))
    return x_ref[:, pl.ds(col0, width)] + pp * jax.nn.sigmoid(accs[0])


def _ep_identity(acc, aux_ref, j, cols):
    return acc


def _ep_gates_t(acc, aux_ref, j, cols):
    return jax.nn.sigmoid(acc).T


def _ep_q(acc, gain_ref, j, cols):
    return _group_rms(acc, gain_ref[:, cols])


def _ep_kv(acc, gain_ref, j, cols):
    normed = _group_rms(acc, gain_ref[0:1, cols])
    return jnp.where((j % 2) == 0, normed, acc)


def _matmul_res_kernel(a_ref, w_ref, x_ref, o_ref):
    o_ref[...] = x_ref[...] + _dot(a_ref[...], w_ref[...].astype(BF16))


def _matmul_res(a, w, resid, *, tm, tn):
    s, k = a.shape
    n = _w_array(w).shape[-1]
    return pl.pallas_call(
        _matmul_res_kernel,
        grid=(s // tm, n // tn),
        in_specs=[pl.BlockSpec((tm, k), lambda i, j: (i, 0)),
                  _w_spec(w, k, tn, lambda j: j),
                  pl.BlockSpec((tm, tn), lambda i, j: (i, j))],
        out_specs=pl.BlockSpec((tm, tn), lambda i, j: (i, j)),
        out_shape=jax.ShapeDtypeStruct((s, n), F32),
        compiler_params=_cparams(("arbitrary", "arbitrary")),
    )(a, _w_array(w), resid)


GMLP_TN = 512
PROJ_TN = 1024


def _gmlp_kernel(x_ref, g_ref, win_ref, nv_ref, ws_ref, bs_ref, wo_ref, o_ref, h_scr, z_scr, *, tm):
    y_scr = h_scr
    width = GMLP_GROUPS * LANE
    n_in = 2 * width // GMLP_TN
    per_tile = GMLP_TN // LANE
    n_chunk = tm // GMLP_CHUNK
    j = pl.program_id(1)

    def z_tile(h):
        z_scr[j] = jax.nn.gelu(_dot(h, win_ref[...].astype(BF16)))

    @pl.when(j == 0)
    def _():
        h = _rms_rows(x_ref[...], g_ref[...]).astype(BF16)
        h_scr[...] = h
        z_tile(h)

    @pl.when((j > 0) & (j < n_in))
    def _():
        z_tile(h_scr[...])

    @pl.when(j == n_in)
    def _():
        v_tiles = range(n_in // 2, n_in)
        ms = sum(jnp.sum(z_scr[t] * z_scr[t], axis=-1, keepdims=True) for t in v_tiles) / width
        inv = lax.rsqrt(ms + EPS)
        row = lax.broadcasted_iota(jnp.int32, (GMLP_CHUNK, GMLP_CHUNK), 0)
        col = lax.broadcasted_iota(jnp.int32, (GMLP_CHUNK, GMLP_CHUNK), 1)
        causal = col <= row
        for g in range(GMLP_GROUPS):
            t, ls = g // per_tile, slice((g % per_tile) * LANE, (g % per_tile + 1) * LANE)
            cs = slice(g * LANE, (g + 1) * LANE)
            vn = (z_scr[n_in // 2 + t, :, ls] * inv * nv_ref[:, cs]).astype(BF16)
            ws = jnp.where(causal, ws_ref[g], 0.0).astype(BF16)
            vg = jnp.concatenate(
                [vn[c * GMLP_CHUNK:(c + 1) * GMLP_CHUNK, :] for c in range(n_chunk)], axis=1)
            sv = _dot(ws, vg) + bs_ref[:, g:g + 1]
            for c in range(n_chunk):
                rs = slice(c * GMLP_CHUNK, (c + 1) * GMLP_CHUNK)
                y_scr[rs, cs] = (z_scr[t, rs, ls] * sv[:, c * LANE:(c + 1) * LANE]).astype(BF16)

    @pl.when(j >= n_in)
    def _():
        col0 = pl.multiple_of((j - n_in) * GMLP_TN, GMLP_TN)
        o_ref[...] = x_ref[:, pl.ds(col0, GMLP_TN)] + _dot(y_scr[...], wo_ref[...].astype(BF16))


def _gmlp(x, gain, w_in, norm_v, w_s, b_s_t, w_out, layer, *, tm):
    s, d = x.shape
    width = GMLP_GROUPS * LANE
    assert width == d
    n_in = 2 * width // GMLP_TN
    n_out = d // GMLP_TN
    return pl.pallas_call(
        functools.partial(_gmlp_kernel, tm=tm),
        grid=(s // tm, n_in + n_out),
        in_specs=[pl.BlockSpec((tm, d), lambda i, j: (i, 0)),
                  pl.BlockSpec((1, d), lambda i, j: (0, 0)),
                  pl.BlockSpec((None, d, GMLP_TN), lambda i, j: (layer, 0, jnp.minimum(j, n_in - 1))),
                  pl.BlockSpec((1, width), lambda i, j: (0, 0)),
                  pl.BlockSpec((None, GMLP_GROUPS, GMLP_CHUNK, GMLP_CHUNK), lambda i, j: (layer, 0, 0, 0)),
                  pl.BlockSpec((GMLP_CHUNK, GMLP_GROUPS), lambda i, j: (0, 0)),
                  pl.BlockSpec((None, width, GMLP_TN), lambda i, j: (layer, 0, jnp.maximum(j - n_in, 0)))],
        out_specs=pl.BlockSpec((tm, GMLP_TN), lambda i, j: (i, jnp.maximum(j - n_in, 0))),
        out_shape=jax.ShapeDtypeStruct((s, d), F32),
        scratch_shapes=[pltpu.VMEM((tm, d), BF16),
                        pltpu.VMEM((n_in, tm, GMLP_TN), F32)],
        compiler_params=_cparams(("arbitrary", "arbitrary")),
    )(x, gain.reshape(1, d), w_in, norm_v.reshape(1, width), w_s, b_s_t, w_out)


def _compress_kernel(kv_ref, pe_ref, w1_ref, w2_ref, kn_ref, o_ref, *, n_half):
    half_w = CMP_STRIDE * HEAD_DIM
    hh = jnp.concatenate([kv_ref[pl.ds(r, n_half, stride=CMP_STRIDE), :] for r in range(CMP_STRIDE)],
                         axis=1)
    pe = pe_ref[...]
    a = _dot((hh + pe[0:1, :]).astype(BF16), w1_ref[0:half_w, :].astype(BF16))
    b = _dot((hh + pe[1:2, :]).astype(BF16), w1_ref[half_w:2 * half_w, :].astype(BF16))
    pre = a + pltpu.roll(b, n_half - 1, 0)
    out = _dot(jax.nn.gelu(pre).astype(BF16), w2_ref[...].astype(BF16))
    is_key = pl.program_id(0) < N_KV_GROUPS
    out = jnp.where(is_key, _rms_rows(out, kn_ref[...]), out)
    row = lax.broadcasted_iota(jnp.int32, out.shape, 0)
    o_ref[...] = jnp.where(row < n_half - 1, out, 0.0)


def _compress(kvc, pe, w1, w2, k_norm0):
    s = kvc.shape[0]
    n_half = s // CMP_STRIDE
    half_w = CMP_STRIDE * HEAD_DIM
    hid = w1.shape[2]
    return pl.pallas_call(
        functools.partial(_compress_kernel, n_half=n_half),
        grid=(2 * N_KV_GROUPS,),
        in_specs=[pl.BlockSpec((s, HEAD_DIM), lambda n: (0, n)),
                  pl.BlockSpec((None, 2, half_w), lambda n: (n // N_KV_GROUPS, 0, 0)),
                  pl.BlockSpec((None, 2 * half_w, hid), lambda n: (n // N_KV_GROUPS, 0, 0)),
                  pl.BlockSpec((None, hid, HEAD_DIM), lambda n: (n // N_KV_GROUPS, 0, 0)),
                  pl.BlockSpec((1, HEAD_DIM), lambda n: (0, 0))],
        out_specs=pl.BlockSpec((None, n_half, HEAD_DIM), lambda n: (n, 0, 0)),
        out_shape=jax.ShapeDtypeStruct((2 * N_KV_GROUPS, n_half, HEAD_DIM), F32),
        compiler_params=_cparams(("arbitrary",)),
    )(kvc, pe, w1, w2, k_norm0.reshape(1, HEAD_DIM))


def _t5_bucket_np(dist):
    n = np.maximum(dist, 0)
    max_exact = N_BUCKETS // 2
    nf = np.maximum(n, 1).astype(np.float32)
    large = max_exact + (np.log(nf / np.float32(max_exact)) / np.float32(math.log(MAX_DISTANCE / max_exact))
                         * np.float32(N_BUCKETS - max_exact)).astype(np.int32)
    large = np.minimum(large, N_BUCKETS - 1)
    return np.where(n < max_exact, n, large).astype(np.int32)


N_PATTERNS = 4


def _bucket_patterns():
    i = np.arange(Q_BLOCK)[:, None]
    c = np.arange(LANE)[None, :]
    d0 = i - c
    d1 = i - c + Q_BLOCK
    dc = i - CMP_STRIDE * (c - (LANE - 8)) - (CMP_BLOCK - 1)
    pats = [np.where(d >= 0, _t5_bucket_np(d), -1) for d in (d0, d1, dc)]
    pats.append(np.where(i < c, N_BUCKETS - 1, -1))
    return np.stack([p.T for p in pats]).astype(np.int32)


def _bias_tiles_kernel(tab_ref, pat_ref, o_ref):
    h = pl.program_id(0)
    pat = pat_ref[...]
    far = tab_ref[N_BUCKETS - 1, h]
    acc = jnp.full(pat.shape, NEG, F32)
    for b in range(N_BUCKETS):
        acc = jnp.where(pat == b, (tab_ref[b, h] - far) * LOG2E, acc)
    o_ref[...] = acc


def _bias_tiles(rel_bias):
    pats = jnp.asarray(_bucket_patterns())
    return pl.pallas_call(
        _bias_tiles_kernel,
        grid=(N_HEADS,),
        in_specs=[pl.BlockSpec(memory_space=pltpu.SMEM),
                  pl.BlockSpec((N_PATTERNS, Q_BLOCK, LANE), lambda h: (0, 0, 0))],
        out_specs=pl.BlockSpec((None, N_PATTERNS, Q_BLOCK, LANE), lambda h: (h, 0, 0, 0)),
        out_shape=jax.ShapeDtypeStruct((N_HEADS, N_PATTERNS, Q_BLOCK, LANE), F32),
        compiler_params=_cparams(("arbitrary",)),
    )(rel_bias, pats)


def _with_features(qs, feat):
    reps = qs.shape[0] // feat.shape[0]
    return jnp.concatenate([qs, jnp.concatenate([feat] * reps, axis=0)], axis=1)


def _softmax_cols(blocks):
    m = blocks[0].max(axis=0, keepdims=True)
    for b in blocks[1:]:
        m = jnp.maximum(m, b.max(axis=0, keepdims=True))
    es = [jnp.exp2(b - m) for b in blocks]
    den = es[0].sum(axis=0, keepdims=True)
    for e in es[1:]:
        den = den + e.sum(axis=0, keepdims=True)
    return es, m, den


def _nsa_kernel(q_ref, gt_ref, kc_ref, vc_ref, vct_ref, ovl_ref, ovlt_ref, ks_ref, vsf_ref, vsn_ref,
                kw_ref, vwn_ref, tb_ref, o_ref, m_scr, acc_scr, sa_scr, sb_scr, cmp_o_scr, cmp_pf_scr, cmp_pn_scr,
                *, n_slc):
    qb = pl.program_id(1)
    s = qb * Q_BLOCK
    hpg = HEADS_PER_GROUP

    q_all = q_ref[...]
    qs = jnp.concatenate([q_all[:, h * LANE:(h + 1) * LANE] for h in range(hpg)], axis=0)
    cols = [slice(h * Q_BLOCK, (h + 1) * Q_BLOCK) for h in range(hpg)]
    lane_f = lax.broadcasted_iota(jnp.int32, (Q_BLOCK, LANE), 1)
    pad_feat = jnp.where(lane_f == LANE - 1, 1.0, 0.0).astype(BF16)
    q_pad = _with_features(qs, pad_feat)

    n_sub = NEAR // LANE
    row0 = pl.multiple_of(s, Q_BLOCK)

    n_pair = hpg // 2
    pair_rows = [slice(pr * 2 * Q_BLOCK, (pr + 1) * 2 * Q_BLOCK) for pr in range(n_pair)]
    half = [slice(0, Q_BLOCK), slice(Q_BLOCK, 2 * Q_BLOCK)]

    def run_stages(stages):
        pending = stages[0][0]()
        for i, (_, consume) in enumerate(stages):
            cur = pending
            if i + 1 < len(stages):
                pending = stages[i + 1][0]()
            consume(cur)

    def near_stages(k_ref, q_aug, vn_ref, oldest_pat, res):
        k_aug = k_ref[pl.ds(row0, NEAR), :]
        v_blocks = vn_ref[pl.ds(qb, n_sub)]
        v_t = jnp.concatenate([v_blocks[u] for u in range(n_sub)], axis=1)

        def stage(pr):
            def issue():
                return _dot_nt(k_aug, q_aug[pair_rows[pr], :])

            def consume(logits):
                ps = []
                for hh in range(2):
                    h = 2 * pr + hh
                    blocks = [logits[u * LANE:(u + 1) * LANE, half[hh]] for u in range(n_sub)]
                    blocks[n_sub - 1] = blocks[n_sub - 1] + tb_ref[h, 0]
                    blocks[n_sub - 2] = blocks[n_sub - 2] + tb_ref[h, 1]
                    if oldest_pat is not None:
                        blocks[0] = blocks[0] + tb_ref[h, oldest_pat]
                    m = blocks[0].max(axis=0, keepdims=True)
                    for b in blocks[1:]:
                        m = jnp.maximum(m, b.max(axis=0, keepdims=True))
                    ps.append(jnp.concatenate([jnp.exp2((b - m).astype(BF16)) for b in blocks], axis=0))
                    res["m"].append(m)
                res["o"].append(_dot(v_t, jnp.concatenate(ps, axis=1)))
            return issue, consume
        return [stage(pr) for pr in range(n_pair)]

    near0 = pl.multiple_of(qb * 8 + 8, 8)
    kn = kc_ref[pl.ds(near0, LANE), :].astype(BF16)
    vn_t = vc_ref[pl.ds(near0, LANE), :].T.astype(BF16)
    ovl_n_t = ovl_ref[pl.ds(near0, LANE), :].T.astype(BF16)
    far_feat = jnp.where(lane_f > qb - 16, 1.0, 0.0).astype(BF16)
    q_cmp = _with_features(qs, far_feat)
    t_row = s + lax.broadcasted_iota(jnp.int32, (1, Q_BLOCK), 1)
    row_ok = t_row >= CMP_BLOCK - 1
    n_half = kc_ref.shape[0] - CMP_PAD

    def cmp_branch(n_keys):
        kc = kc_ref[CMP_PAD:CMP_PAD + n_keys, :].astype(BF16)
        vc_t = vct_ref[:, CMP_PAD:CMP_PAD + n_keys]
        res = {"o": [], "pf": jnp.zeros((n_keys, Q_BLOCK), F32), "pn": jnp.zeros((LANE, Q_BLOCK), F32)}
        heads = hpg * LANE // min(n_keys, hpg * LANE // 2)

        def stage(first):
            q_rows = slice(first * Q_BLOCK, (first + heads) * Q_BLOCK)

            def issue():
                return _dot_nt(kc, q_cmp[q_rows, :]), _dot_nt(kn, q_pad[q_rows, :])

            def consume(logits):
                sf, sn = logits
                pfs, pns = [], []
                for hh in range(heads):
                    (ef, en), _, den = _softmax_cols([sf[:, cols[hh]], sn[:, cols[hh]] + tb_ref[first + hh, 2]])
                    inv = jnp.where(row_ok, 1.0 / den, 0.0)
                    pf = ef * inv
                    pn = en * inv
                    res["pf"] = res["pf"] + pf
                    res["pn"] = res["pn"] + pn
                    pfs.append(pf.astype(BF16))
                    pns.append(pn.astype(BF16))
                res["o"].append(_dot(vc_t, jnp.concatenate(pfs, axis=1)) + _dot(vn_t, jnp.concatenate(pns, axis=1)))
            return issue, consume

        run_stages([stage(first) for first in range(0, hpg, heads)])
        cmp_o_scr[...] = jnp.concatenate(res["o"], axis=1)
        cmp_pf_scr[0:n_keys, :] = res["pf"]
        if n_keys < n_half:
            cmp_pf_scr[n_keys:, :] = jnp.zeros((n_half - n_keys, Q_BLOCK), F32)
        cmp_pn_scr[...] = res["pn"]

    sizes = sorted({min(n_half, c) for c in (LANE, 2 * LANE)} | {n_half})
    lo = 0
    for idx, n_keys in enumerate(sizes):
        last = idx + 1 == len(sizes)
        hi = (n_keys + LANE - 8) // 8 + 1
        pl.when((qb >= lo) if last else ((qb >= lo) & (qb < hi)))(functools.partial(cmp_branch, n_keys))
        lo = hi

    blk = lax.broadcasted_iota(jnp.int32, (n_slc, Q_BLOCK), 0)
    blk_f = blk.astype(F32)
    cur = (s + lax.broadcasted_iota(jnp.int32, (n_slc, Q_BLOCK), 1)) >> 6
    forced = (blk == 0) | (blk == cur) | (blk == cur - 1)
    valid = blk <= cur
    topk = {}

    def importance_stage():
        def consume(_):
            psum_f, psum_n = cmp_pf_scr[...], cmp_pn_scr[...]
            ovl_t = ovlt_ref[:, CMP_PAD:]
            pf_hi = psum_f.astype(BF16)
            pf_lo = (psum_f - pf_hi.astype(F32)).astype(BF16)
            pn_hi = psum_n.astype(BF16)
            pn_lo = (psum_n - pn_hi.astype(F32)).astype(BF16)
            imp_t = _dot(ovl_t, pf_hi) + _dot(ovl_t, pf_lo) + _dot(ovl_n_t, pn_hi) + _dot(ovl_n_t, pn_lo)
            topk["score"] = jnp.where(valid & ~forced, imp_t, -BIG)
            topk["sel"] = jnp.where(forced, 1.0, 0.0)
        return (lambda: None), consume

    def topk_stage(rounds):
        def consume(_):
            score, sel_t = topk["score"], topk["sel"]
            for _r in range(rounds):
                top = jnp.max(score, axis=0, keepdims=True)
                first = jnp.min(jnp.where(score == top, blk_f, float(n_slc)), axis=0, keepdims=True)
                pick = blk_f == first
                sel_t = jnp.where(pick, 1.0, sel_t)
                score = jnp.where(pick, -2.0 * BIG, score)
            topk["score"], topk["sel"] = score, sel_t
        return (lambda: None), consume

    win_res = {"o": [], "m": []}
    win_stages = near_stages(kw_ref, q_pad, vwn_ref, 3, win_res)
    free_picks = max(min(N_SELECT, n_slc) - 3, 0)
    rounds = [free_picks // n_pair + (1 if pr < free_picks % n_pair else 0) for pr in range(n_pair)]
    stages = [importance_stage()]
    for pr in range(n_pair):
        stages += [win_stages[pr], topk_stage(rounds[pr])]
    run_stages(stages)
    o_win = jnp.concatenate(win_res["o"], axis=1)
    o_cmp = cmp_o_scr[...]
    sel_t = topk["sel"]
    drop_t = jnp.where(valid, 1.0 - sel_t, 1.0)
    near_blk = (s - WINDOW) >> 6
    drop_far_t = jnp.where(blk >= near_blk, 1.0, drop_t)
    q_near = _with_features(qs, drop_t.T.astype(BF16))
    q_far = _with_features(qs, drop_far_t.T.astype(BF16))

    def far_logits(t):
        r0 = pl.multiple_of(WINDOW + KV_TILE * t, KV_TILE)
        return _dot_nt(ks_ref[pl.ds(r0, KV_TILE), :], q_far)

    sa_scr[...] = far_logits(0)
    slc_res = {"o": [], "m": []}
    run_stages(near_stages(ks_ref, q_near, vsn_ref, None, slc_res))
    m_scr[...] = jnp.concatenate(slc_res["m"], axis=1)
    acc_scr[...] = jnp.concatenate(slc_res["o"], axis=1)

    gates = gt_ref[...]
    g_c, g_s, g_w = [jnp.concatenate([gates[3 * h + br:3 * h + br + 1, :] for h in range(hpg)], axis=1)
                     for br in range(N_BRANCH)]
    out_cw = g_c * o_cmp + (g_w / o_win[HEAD_DIM:HEAD_DIM + 1, :]) * o_win[:HEAD_DIM, :]

    n_far = (jnp.maximum(s - WINDOW, 0) + KV_TILE - 1) // KV_TILE

    n_tiles = vsf_ref.shape[0]

    def far_probs(st_ref):
        m_old = m_scr[...]
        sts = [st_ref[:, cols[h]] for h in range(hpg)]
        m_new = jnp.concatenate(
            [jnp.maximum(m_old[:, cols[h]], sts[h].max(axis=0, keepdims=True)) for h in range(hpg)], axis=1)
        p_t = jnp.concatenate(
            [jnp.exp2((sts[h] - m_new[:, cols[h]]).astype(BF16)) for h in range(hpg)], axis=1)
        m_scr[...] = m_new
        return p_t, jnp.exp2(m_old - m_new)

    def far_accumulate(t, p_t, alpha):
        acc_scr[...] = alpha * acc_scr[...] + _dot(vsf_ref[t], p_t)

    def far_body(i, carry):
        t0 = 2 * i
        sb_scr[...] = far_logits(t0 + 1)
        p_a, alpha_a = far_probs(sa_scr)
        sa_scr[...] = far_logits(jnp.minimum(t0 + 2, n_tiles - 1))
        far_accumulate(t0, p_a, alpha_a)
        p_b, alpha_b = far_probs(sb_scr)
        far_accumulate(t0 + 1, p_b, alpha_b)
        return carry

    lax.fori_loop(0, n_far // 2, far_body, 0)

    @pl.when(n_far % 2 == 1)
    def _():
        p_last, alpha_last = far_probs(sa_scr)
        far_accumulate(n_far - 1, p_last, alpha_last)

    out_t = out_cw + (g_s / acc_scr[HEAD_DIM:HEAD_DIM + 1, :]) * acc_scr[:HEAD_DIM, :]
    for h in range(hpg):
        o_ref[:, cols[h]] = out_t[:, cols[h]].T.astype(o_ref.dtype)


def _nsa_attention(q, gates_t, k_cmp, v_cmp, v_cmp_t, ovl, ovl_t, ks, vs_far, vs_near, kw, vw_near, bias_tiles):
    s = q.shape[0]
    n_qb = s // Q_BLOCK
    n_slc = s // SLC_BLOCK
    gw = HEADS_PER_GROUP * HEAD_DIM
    lanes = HEADS_PER_GROUP * Q_BLOCK

    def group_spec(a):
        zeros = (0,) * (a.ndim - 1)
        return pl.BlockSpec((None,) + a.shape[1:], lambda g, i: (g,) + zeros)

    def whole_spec(a):
        zeros = (0,) * a.ndim
        return pl.BlockSpec(a.shape, lambda g, i: zeros)

    return pl.pallas_call(
        functools.partial(_nsa_kernel, n_slc=n_slc),
        grid=(N_KV_GROUPS, n_qb),
        in_specs=[pl.BlockSpec((Q_BLOCK, gw), lambda g, i: (i, g)),
                  pl.BlockSpec((LANE, Q_BLOCK), lambda g, i: (g, i)),
                  group_spec(k_cmp), group_spec(v_cmp), group_spec(v_cmp_t),
                  whole_spec(ovl), whole_spec(ovl_t),
                  group_spec(ks), group_spec(vs_far), group_spec(vs_near),
                  group_spec(kw), group_spec(vw_near),
                  pl.BlockSpec((HEADS_PER_GROUP, N_PATTERNS, LANE, Q_BLOCK), lambda g, i: (g, 0, 0, 0))],
        out_specs=pl.BlockSpec((Q_BLOCK, gw), lambda g, i: (i, g)),
        out_shape=jax.ShapeDtypeStruct((s, N_KV_GROUPS * gw), BF16),
        scratch_shapes=[pltpu.VMEM((1, lanes), F32),
                        pltpu.VMEM((HEAD_DIM + SUM_ROWS, lanes), F32),
                        pltpu.VMEM((KV_TILE, lanes), F32),
                        pltpu.VMEM((KV_TILE, lanes), F32),
                        pltpu.VMEM((HEAD_DIM, lanes), F32),
                        pltpu.VMEM((k_cmp.shape[1] - CMP_PAD, Q_BLOCK), F32),
                        pltpu.VMEM((LANE, Q_BLOCK), F32)],
        compiler_params=_cparams(("arbitrary", "arbitrary")),
    )(q, gates_t, k_cmp, v_cmp, v_cmp_t, ovl, ovl_t, ks, vs_far, vs_near, kw, vw_near, bias_tiles)


def _cmp_mask_columns(n_half):
    assert n_half // 8 < LANE - 1
    out = np.zeros((CMP_PAD + n_half, LANE), np.float32)
    out[np.arange(CMP_PAD), LANE - 1] = NEG
    k = np.arange(n_half)
    out[CMP_PAD + k, k // 8] = NEG
    return out


def _slc_mask_columns(s):
    n_slc = s // SLC_BLOCK
    assert n_slc - 1 > (WINDOW + Q_BLOCK) // SLC_BLOCK
    out = np.zeros((WINDOW + s, n_slc), np.float32)
    out[np.arange(WINDOW), n_slc - 1] = NEG
    pos = np.arange(s)
    out[WINDOW + pos, pos // SLC_BLOCK] = NEG
    return out


def _pad_mask_columns(s):
    out = np.zeros((WINDOW + s, LANE), np.float32)
    out[np.arange(WINDOW), LANE - 1] = NEG
    return out


def _overlap_padded(n_half, n_slc):
    n_cmp = n_half - 1
    c0 = np.arange(n_cmp) * CMP_STRIDE
    s0 = np.arange(n_slc) * SLC_BLOCK
    lo = np.maximum(c0[:, None], s0[None, :])
    hi = np.minimum(c0[:, None] + CMP_BLOCK, s0[None, :] + SLC_BLOCK)
    ovl = np.maximum(hi - lo, 0).astype(np.float32) / CMP_BLOCK
    out = np.zeros((CMP_PAD + n_half, n_slc), np.float32)
    out[CMP_PAD:CMP_PAD + n_cmp] = ovl
    return out


def _row_tile(s):
    return min(1024, s)


def _ffn_ple(x, p, layer, b, norm_ffn, w_in, w_out, norm_ple, ple_w, ple_gate):
    s = x.shape[0]
    tm = _row_tile(s)
    tn = 512
    nj = FFN_DIM // tn
    act = _norm_matmul(x, norm_ffn, (w_in, layer), (0, nj), tm=tm, tn=tn, nj=nj,
                       epilogue=_ep_swiglu, out_dtype=BF16)
    x = _matmul_res(act, (w_out, layer), x, tm=tm, tn=512)
    aux = (p, ple_w)
    tn_ple = PROJ_TN
    aux_specs = (pl.BlockSpec((None, None, tm, PLE_DIM), lambda i, j: (layer, b, i, 0)),
                 pl.BlockSpec((None, PLE_DIM, tn_ple), lambda i, j: (layer, 0, j)))
    return _norm_matmul(x, norm_ple, (ple_gate, layer), (0,), tm=tm, tn=tn_ple, nj=D_MODEL // tn_ple,
                        epilogue=_ep_ple, out_dtype=F32, aux=aux, aux_specs=aux_specs)


def _gmlp_layer(x, layer, norm_mix, w_in, norm_v, w_s, b_s, w_out):
    return _gmlp(x, norm_mix, w_in, norm_v, w_s, jnp.transpose(b_s), w_out, layer, tm=_row_tile(x.shape[0]))


def _shared_kv(x, kv_norm, kv_w, k_norm, cmp_pe_k, cmp_pe_v, cmp_wk1, cmp_wk2, cmp_wv1, cmp_wv2):
    s = x.shape[0]
    tm = _row_tile(s)
    gw = N_KV_GROUPS * HEAD_DIM
    gains = jnp.stack([jnp.tile(k_norm[1], N_KV_GROUPS), jnp.ones((gw,), F32),
                       jnp.tile(k_norm[2], N_KV_GROUPS), jnp.ones((gw,), F32)])
    gains = jnp.broadcast_to(gains[:, None, :], (4, 8, gw))
    kvc, kvr = _norm_matmul_pair(
        x, kv_norm, kv_w, kv_w, gains, pl.BlockSpec((None, 8, gw), lambda i, j: (jnp.maximum(j - 1, 0), 0, 0)),
        tm=tm, tna=2 * gw, nja=1, col_a=0, tnb=gw, njb=4, col_b=2,
        ep_a=_ep_identity, ep_b=_ep_kv, dtype_a=F32, dtype_b=BF16)
    n_half = s // CMP_STRIDE
    pe = jnp.stack([cmp_pe_k, cmp_pe_v]).reshape(2, 2, CMP_STRIDE * HEAD_DIM)
    kv_cmp = _compress(kvc, pe, jnp.stack([cmp_wk1, cmp_wv1]), jnp.stack([cmp_wk2, cmp_wv2]), k_norm[0])
    kv_cmp = jnp.pad(kv_cmp, ((0, 0), (CMP_PAD, 0), (0, 0)))
    cmp_cols = jnp.broadcast_to(jnp.asarray(_cmp_mask_columns(n_half)), (N_KV_GROUPS, CMP_PAD + n_half, LANE))
    k_cmp = jnp.concatenate([kv_cmp[:N_KV_GROUPS], cmp_cols], axis=2)
    v_cmp = kv_cmp[N_KV_GROUPS:]
    v_cmp_t = jnp.transpose(v_cmp, (0, 2, 1)).astype(BF16)
    s_pad = WINDOW + s
    kvr = jnp.pad(kvr, ((WINDOW, 0), (0, 0))).reshape(s_pad, 4, N_KV_GROUPS, HEAD_DIM)
    kvr = jnp.transpose(kvr, (1, 2, 0, 3))
    slc_cols = jnp.asarray(_slc_mask_columns(s)).astype(BF16)
    pad_cols = jnp.asarray(_pad_mask_columns(s)).astype(BF16)
    ks = jnp.concatenate([kvr[0], jnp.broadcast_to(slc_cols, (N_KV_GROUPS,) + slc_cols.shape)], axis=2)
    kw = jnp.concatenate([kvr[2], jnp.broadcast_to(pad_cols, (N_KV_GROUPS,) + pad_cols.shape)], axis=2)

    def key_tiles_t(v, tile):
        v_t = jnp.transpose(v.reshape(N_KV_GROUPS, -1, tile, HEAD_DIM), (0, 1, 3, 2))
        return jnp.concatenate([v_t, jnp.ones(v_t.shape[:2] + (SUM_ROWS, tile), v_t.dtype)], axis=2)

    vs_far = key_tiles_t(kvr[1][:, WINDOW:], KV_TILE)
    return k_cmp, v_cmp, v_cmp_t, ks, vs_far, key_tiles_t(kvr[1], LANE), kw, key_tiles_t(kvr[3], LANE)


def _nsa_layer(x, layer, norm_mix, w_in, q_norm, w_out, kvs):
    s = x.shape[0]
    tm = _row_tile(s)
    nq = N_HEADS * HEAD_DIM
    scale = HEAD_DIM ** -0.5 * LOG2E
    q_gain = jnp.tile(q_norm * scale, PROJ_TN // HEAD_DIM).reshape(1, PROJ_TN)
    w_gate = w_in[layer, :, nq:].reshape(D_MODEL, N_KV_GROUPS, HEADS_PER_GROUP * N_BRANCH)
    w_gate = jnp.pad(w_gate, ((0, 0), (0, 0), (0, LANE - HEADS_PER_GROUP * N_BRANCH)))
    w_gate = w_gate.reshape(D_MODEL, N_KV_GROUPS * LANE)
    q, gates_t = _norm_matmul_pair(
        x, norm_mix, (w_in, layer), w_gate, q_gain, pl.BlockSpec((1, PROJ_TN), lambda i, j: (0, 0)),
        tm=tm, tna=PROJ_TN, nja=nq // PROJ_TN, col_a=0, tnb=N_KV_GROUPS * LANE, njb=1, col_b=0,
        ep_a=_ep_q, ep_b=_ep_gates_t, dtype_a=BF16, dtype_b=F32, transpose_b=True)
    o = _nsa_attention(q, gates_t, *kvs)
    return _matmul_res(o, (w_out, layer), x, tm=tm, tn=PROJ_TN)


def kernel(x, p, norm_mix, norm_ffn, norm_ple, a_w_in, a_norm_v, a_w_s, a_b_s, a_w_out, kv_norm, kv_w, k_norm, cmp_pe_k, cmp_pe_v, cmp_wk1, cmp_wk2, cmp_wv1, cmp_wv2, b_w_in, b_q_norm, b_w_out, rel_bias, ffn_w_in, ffn_w_out, ple_w, ple_gate):
    batch, s, d = x.shape
    depth = norm_mix.shape[0]
    n_a = a_w_in.shape[0]
    a_w_out, b_w_in, b_w_out, kv_w, ple_gate, ffn_w_out = (
        w.astype(BF16) for w in (a_w_out, b_w_in, b_w_out, kv_w, ple_gate, ffn_w_out))
    outs = []
    for b in range(batch):
        xb = x.reshape(s, d) if batch == 1 else x[b]
        kvs = None
        for i in range(depth):
            if i < n_a:
                xb = _gmlp_layer(xb, i, norm_mix[i], a_w_in, a_norm_v[i], a_w_s, a_b_s[i], a_w_out)
            else:
                j = i - n_a
                xb = _nsa_layer(xb, j, norm_mix[i], b_w_in, b_q_norm[j], b_w_out, kvs)
            xb = _ffn_ple(xb, p, i, b, norm_ffn[i], ffn_w_in, ffn_w_out,
                          norm_ple[i], ple_w, ple_gate)
            if i == n_a - 1:
                k_cmp, v_cmp, v_cmp_t, ks, vs_far, vs_near, kw, vw_near = _shared_kv(
                    xb, kv_norm, kv_w, k_norm, cmp_pe_k, cmp_pe_v, cmp_wk1, cmp_wk2, cmp_wv1, cmp_wv2)
                ovl = _overlap_padded(s // CMP_STRIDE, s // SLC_BLOCK)
                kvs = (k_cmp, v_cmp, v_cmp_t, jnp.asarray(ovl), jnp.asarray(ovl.T).astype(BF16),
                       ks, vs_far, vs_near, kw, vw_near, _bias_tiles(rel_bias))
        outs.append(xb)
    return outs[0].reshape(1, s, d) if batch == 1 else jnp.stack(outs)
```

```python
import functools
import math

import numpy as np
import jax
import jax.numpy as jnp
from jax import lax
from jax.experimental import pallas as pl
from jax.experimental.pallas import tpu as pltpu

F32 = jnp.float32
BF16 = jnp.bfloat16

D_MODEL = 2048
PLE_DIM = 256
FFN_DIM = 5632
GMLP_CHUNK = 128
GMLP_GROUPS = 16
HEAD_DIM = 128
N_HEADS = 16
N_KV_GROUPS = 2
HEADS_PER_GROUP = 8
N_BRANCH = 3
CMP_BLOCK = 32
CMP_STRIDE = 16
SLC_BLOCK = 64
N_SELECT = 16
WINDOW = 512
Q_BLOCK = 128
N_BUCKETS = 32
MAX_DISTANCE = 128
EPS = 1e-6
NEG = -1e30
BIG = 1e30
LOG2E = math.log2(math.e)

LANE = 128
KV_TILE = 512
NEAR = WINDOW + Q_BLOCK
CMP_PAD = 128
SUM_ROWS = 16
VMEM_LIMIT = 56 * 1024 * 1024


def _cparams(sem):
    return pltpu.CompilerParams(dimension_semantics=sem, vmem_limit_bytes=VMEM_LIMIT)


def _dot(a, b):
    return jnp.dot(a, b, preferred_element_type=F32)


def _dot_nt(a, b):
    return lax.dot_general(a, b, (((1,), (1,)), ((), ())), preferred_element_type=F32)


def _rms_rows(x, g):
    ms = jnp.mean(x * x, axis=-1, keepdims=True)
    return x * lax.rsqrt(ms + EPS) * g


def _group_rms(acc, gain):
    outs = []
    for c in range(acc.shape[1] // LANE):
        a = acc[:, c * LANE:(c + 1) * LANE]
        outs.append(_rms_rows(a, gain[:, c * LANE:(c + 1) * LANE]))
    return outs[0] if len(outs) == 1 else jnp.concatenate(outs, axis=1)


EPILOGUE_COLS = 256


def _column_chunks(tn):
    cw = min(tn, EPILOGUE_COLS)
    return [slice(c, c + cw) for c in range(0, tn, cw)]


def _norm_matmul_kernel(x_ref, g_ref, *refs, n_w, n_aux, n_cast, epilogue, out_dtype):
    w_refs = refs[:n_w]
    aux_refs = refs[n_w:n_w + n_aux]
    o_ref = refs[n_w + n_aux + n_cast]
    h_scr = refs[-1]
    if n_cast:
        cast_in, cast_out = refs[n_w + n_aux], refs[n_w + n_aux + n_cast + 1]
        cast_out[...] = cast_in[...].astype(BF16)

    def column_tile(h):
        chunks = _column_chunks(o_ref.shape[1])
        accs = [[_dot(h, w_ref[:, cs].astype(BF16)) for w_ref in w_refs] for cs in chunks]
        for cs, acc in zip(chunks, accs):
            o_ref[:, cs] = epilogue(acc, aux_refs + (x_ref,), cs).astype(out_dtype)

    @pl.when(pl.program_id(1) == 0)
    def _():
        h = _rms_rows(x_ref[...], g_ref[...]).astype(BF16)
        h_scr[...] = h
        column_tile(h)

    @pl.when(pl.program_id(1) > 0)
    def _():
        column_tile(h_scr[...])


def _w_spec(w, k, tn, col_of):
    if isinstance(w, tuple):
        layer = w[1]
        return pl.BlockSpec((None, k, tn), lambda i, j: (layer, 0, col_of(j)))
    return pl.BlockSpec((k, tn), lambda i, j: (0, col_of(j)))


def _w_array(w):
    return w[0] if isinstance(w, tuple) else w


def _norm_matmul(x, gain, w, col_offsets, *, tm, tn, nj, epilogue, out_dtype,
                 aux=(), aux_specs=(), cast=None):
    s, k = x.shape
    in_specs = [pl.BlockSpec((tm, k), lambda i, j: (i, 0)),
                pl.BlockSpec((1, k), lambda i, j: (0, 0))]
    for c0 in col_offsets:
        in_specs.append(_w_spec(w, k, tn, lambda j, c0=c0: c0 + j))
    in_specs += list(aux_specs)
    out_specs = [pl.BlockSpec((tm, tn), lambda i, j: (i, j))]
    out_shape = [jax.ShapeDtypeStruct((s, nj * tn), out_dtype)]
    operands = [x, gain.reshape(1, k)] + [_w_array(w)] * len(col_offsets) + list(aux)
    if cast is not None:
        cast_w, cast_layer = cast
        rows, cols = cast_w.shape[1:]
        slab = rows // ((s // tm) * nj)
        assert slab * (s // tm) * nj == rows and slab % 16 == 0
        in_specs.append(pl.BlockSpec((None, slab, cols), lambda i, j: (cast_layer, i * nj + j, 0)))
        out_specs.append(pl.BlockSpec((slab, cols), lambda i, j: (i * nj + j, 0)))
        out_shape.append(jax.ShapeDtypeStruct((rows, cols), BF16))
        operands.append(cast_w)
    kern = functools.partial(_norm_matmul_kernel, n_w=len(col_offsets), n_aux=len(aux), n_cast=len(out_specs) - 1,
                             epilogue=epilogue, out_dtype=out_dtype)
    outs = pl.pallas_call(
        kern,
        grid=(s // tm, nj),
        in_specs=in_specs,
        out_specs=out_specs,
        out_shape=out_shape,
        scratch_shapes=[pltpu.VMEM((tm, k), BF16)],
        compiler_params=_cparams(("arbitrary", "arbitrary")),
    )(*operands)
    return outs[0] if cast is None else outs


def _norm_matmul_pair_kernel(x_ref, g_ref, wa_ref, wb_ref, aux_ref, oa_ref, ob_ref, h_scr, *, nja, ep_a, ep_b):
    j = pl.program_id(1)

    def tile_a(h):
        chunks = _column_chunks(oa_ref.shape[1])
        accs = [_dot(h, wa_ref[:, cs].astype(BF16)) for cs in chunks]
        for cs, acc in zip(chunks, accs):
            oa_ref[:, cs] = ep_a(acc, aux_ref, j, cs).astype(oa_ref.dtype)

    @pl.when(j == 0)
    def _():
        h = _rms_rows(x_ref[...], g_ref[...]).astype(BF16)
        h_scr[...] = h
        tile_a(h)

    if nja > 1:
        @pl.when((j > 0) & (j < nja))
        def _():
            tile_a(h_scr[...])

    @pl.when(j >= nja)
    def _():
        acc = _dot(h_scr[...], wb_ref[...].astype(BF16))
        ob_ref[...] = ep_b(acc, aux_ref, j - nja, slice(0, acc.shape[1])).astype(ob_ref.dtype)


def _norm_matmul_pair(x, gain, wa, wb, aux, aux_spec, *, tm, tna, nja, col_a, tnb, njb, col_b, ep_a, ep_b,
                      dtype_a, dtype_b, transpose_b=False):
    s, k = x.shape
    if transpose_b:
        b_spec = pl.BlockSpec((tnb, tm), lambda i, j: (jnp.maximum(j - nja, 0), i))
        b_shape = (njb * tnb, s)
    else:
        b_spec = pl.BlockSpec((tm, tnb), lambda i, j: (i, jnp.maximum(j - nja, 0)))
        b_shape = (s, njb * tnb)
    kern = functools.partial(_norm_matmul_pair_kernel, nja=nja, ep_a=ep_a, ep_b=ep_b)
    return pl.pallas_call(
        kern,
        grid=(s // tm, nja + njb),
        in_specs=[pl.BlockSpec((tm, k), lambda i, j: (i, 0)),
                  pl.BlockSpec((1, k), lambda i, j: (0, 0)),
                  _w_spec(wa, k, tna, lambda j: col_a + jnp.minimum(j, nja - 1)),
                  _w_spec(wb, k, tnb, lambda j: col_b + jnp.maximum(j - nja, 0)),
                  aux_spec],
        out_specs=[pl.BlockSpec((tm, tna), lambda i, j: (i, jnp.minimum(j, nja - 1))), b_spec],
        out_shape=[jax.ShapeDtypeStruct((s, nja * tna), dtype_a), jax.ShapeDtypeStruct(b_shape, dtype_b)],
        scratch_shapes=[pltpu.VMEM((tm, k), BF16)],
        compiler_params=_cparams(("arbitrary", "arbitrary")),
    )(x, gain.reshape(1, k), _w_array(wa), _w_array(wb), aux)


def _ep_swiglu(accs, aux, cols):
    g, u = accs
    return g * jax.nn.sigmoid(g) * u


def _ep_ple(accs, aux, cols):
    p_ref, wp_ref, x_ref = aux
    width = cols.stop - cols.start
    col0 = pl.multiple_of(pl.program_id(1) * wp_ref.shape[1] + cols.start, width)
    pp = _dot(p_ref[...].astype(BF16), wp_ref[:, cols].astype(BF16))
    return x_ref[:, pl.ds(col0, width)] + pp * jax.nn.sigmoid(accs[0])


def _ep_identity(acc, aux_ref, j, cols):
    return acc


def _ep_gates_t(acc, aux_ref, j, cols):
    return jax.nn.sigmoid(acc).T


def _ep_q(acc, gain_ref, j, cols):
    return _group_rms(acc, gain_ref[:, cols])


def _ep_kv(acc, gain_ref, j, cols):
    normed = _group_rms(acc, gain_ref[0:1, cols])
    return jnp.where((j % 2) == 0, normed, acc)


def _matmul_res_kernel(a_ref, w_ref, x_ref, o_ref):
    o_ref[...] = x_ref[...] + _dot(a_ref[...], w_ref[...].astype(BF16))


def _matmul_res(a, w, resid, *, tm, tn):
    s, k = a.shape
    n = _w_array(w).shape[-1]
    return pl.pallas_call(
        _matmul_res_kernel,
        grid=(s // tm, n // tn),
        in_specs=[pl.BlockSpec((tm, k), lambda i, j: (i, 0)),
                  _w_spec(w, k, tn, lambda j: j),
                  pl.BlockSpec((tm, tn), lambda i, j: (i, j))],
        out_specs=pl.BlockSpec((tm, tn), lambda i, j: (i, j)),
        out_shape=jax.ShapeDtypeStruct((s, n), F32),
        compiler_params=_cparams(("arbitrary", "arbitrary")),
    )(a, _w_array(w), resid)


GMLP_TN = 512
PROJ_TN = 1024


def _gmlp_kernel(x_ref, g_ref, win_ref, nv_ref, ws_ref, bs_ref, wo_ref, o_ref, h_scr, z_scr, *, tm):
    y_scr = h_scr
    width = GMLP_GROUPS * LANE
    n_in = 2 * width // GMLP_TN
    per_tile = GMLP_TN // LANE
    n_chunk = tm // GMLP_CHUNK
    j = pl.program_id(1)

    def z_tile(h):
        z_scr[j] = jax.nn.gelu(_dot(h, win_ref[...].astype(BF16)))

    @pl.when(j == 0)
    def _():
        h = _rms_rows(x_ref[...], g_ref[...]).astype(BF16)
        h_scr[...] = h
        z_tile(h)

    @pl.when((j > 0) & (j < n_in))
    def _():
        z_tile(h_scr[...])

    @pl.when(j == n_in)
    def _():
        v_tiles = range(n_in // 2, n_in)
        ms = sum(jnp.sum(z_scr[t] * z_scr[t], axis=-1, keepdims=True) for t in v_tiles) / width
        inv = lax.rsqrt(ms + EPS)
        row = lax.broadcasted_iota(jnp.int32, (GMLP_CHUNK, GMLP_CHUNK), 0)
        col = lax.broadcasted_iota(jnp.int32, (GMLP_CHUNK, GMLP_CHUNK), 1)
        causal = col <= row
        for g in range(GMLP_GROUPS):
            t, ls = g // per_tile, slice((g % per_tile) * LANE, (g % per_tile + 1) * LANE)
            cs = slice(g * LANE, (g + 1) * LANE)
            vn = (z_scr[n_in // 2 + t, :, ls] * inv * nv_ref[:, cs]).astype(BF16)
            ws = jnp.where(causal, ws_ref[g], 0.0).astype(BF16)
            vg = jnp.concatenate(
                [vn[c * GMLP_CHUNK:(c + 1) * GMLP_CHUNK, :] for c in range(n_chunk)], axis=1)
            sv = _dot(ws, vg) + bs_ref[:, g:g + 1]
            for c in range(n_chunk):
                rs = slice(c * GMLP_CHUNK, (c + 1) * GMLP_CHUNK)
                y_scr[rs, cs] = (z_scr[t, rs, ls] * sv[:, c * LANE:(c + 1) * LANE]).astype(BF16)

    @pl.when(j >= n_in)
    def _():
        col0 = pl.multiple_of((j - n_in) * GMLP_TN, GMLP_TN)
        o_ref[...] = x_ref[:, pl.ds(col0, GMLP_TN)] + _dot(y_scr[...], wo_ref[...].astype(BF16))


def _gmlp(x, gain, w_in, norm_v, w_s, b_s_t, w_out, layer, *, tm):
    s, d = x.shape
    width = GMLP_GROUPS * LANE
    assert width == d
    n_in = 2 * width // GMLP_TN
    n_out = d // GMLP_TN
    return pl.pallas_call(
        functools.partial(_gmlp_kernel, tm=tm),
        grid=(s // tm, n_in + n_out),
        in_specs=[pl.BlockSpec((tm, d), lambda i, j: (i, 0)),
                  pl.BlockSpec((1, d), lambda i, j: (0, 0)),
                  pl.BlockSpec((None, d, GMLP_TN), lambda i, j: (layer, 0, jnp.minimum(j, n_in - 1))),
                  pl.BlockSpec((1, width), lambda i, j: (0, 0)),
                  pl.BlockSpec((None, GMLP_GROUPS, GMLP_CHUNK, GMLP_CHUNK), lambda i, j: (layer, 0, 0, 0)),
                  pl.BlockSpec((GMLP_CHUNK, GMLP_GROUPS), lambda i, j: (0, 0)),
                  pl.BlockSpec((None, width, GMLP_TN), lambda i, j: (layer, 0, jnp.maximum(j - n_in, 0)))],
        out_specs=pl.BlockSpec((tm, GMLP_TN), lambda i, j: (i, jnp.maximum(j - n_in, 0))),
        out_shape=jax.ShapeDtypeStruct((s, d), F32),
        scratch_shapes=[pltpu.VMEM((tm, d), BF16),
                        pltpu.VMEM((n_in, tm, GMLP_TN), F32)],
        compiler_params=_cparams(("arbitrary", "arbitrary")),
    )(x, gain.reshape(1, d), w_in, norm_v.reshape(1, width), w_s, b_s_t, w_out)


def _compress_kernel(kv_ref, pe_ref, w1_ref, w2_ref, kn_ref, o_ref, *, n_half):
    half_w = CMP_STRIDE * HEAD_DIM
    hh = jnp.concatenate([kv_ref[pl.ds(r, n_half, stride=CMP_STRIDE), :] for r in range(CMP_STRIDE)],
                         axis=1)
    pe = pe_ref[...]
    a = _dot((hh + pe[0:1, :]).astype(BF16), w1_ref[0:half_w, :].astype(BF16))
    b = _dot((hh + pe[1:2, :]).astype(BF16), w1_ref[half_w:2 * half_w, :].astype(BF16))
    pre = a + pltpu.roll(b, n_half - 1, 0)
    out = _dot(jax.nn.gelu(pre).astype(BF16), w2_ref[...].astype(BF16))
    is_key = pl.program_id(0) < N_KV_GROUPS
    out = jnp.where(is_key, _rms_rows(out, kn_ref[...]), out)
    row = lax.broadcasted_iota(jnp.int32, out.shape, 0)
    o_ref[...] = jnp.where(row < n_half - 1, out, 0.0)


def _compress(kvc, pe, w1, w2, k_norm0):
    s = kvc.shape[0]
    n_half = s // CMP_STRIDE
    half_w = CMP_STRIDE * HEAD_DIM
    hid = w1.shape[2]
    return pl.pallas_call(
        functools.partial(_compress_kernel, n_half=n_half),
        grid=(2 * N_KV_GROUPS,),
        in_specs=[pl.BlockSpec((s, HEAD_DIM), lambda n: (0, n)),
                  pl.BlockSpec((None, 2, half_w), lambda n: (n // N_KV_GROUPS, 0, 0)),
                  pl.BlockSpec((None, 2 * half_w, hid), lambda n: (n // N_KV_GROUPS, 0, 0)),
                  pl.BlockSpec((None, hid, HEAD_DIM), lambda n: (n // N_KV_GROUPS, 0, 0)),
                  pl.BlockSpec((1, HEAD_DIM), lambda n: (0, 0))],
        out_specs=pl.BlockSpec((None, n_half, HEAD_DIM), lambda n: (n, 0, 0)),
        out_shape=jax.ShapeDtypeStruct((2 * N_KV_GROUPS, n_half, HEAD_DIM), F32),
        compiler_params=_cparams(("arbitrary",)),
    )(kvc, pe, w1, w2, k_norm0.reshape(1, HEAD_DIM))


def _t5_bucket_np(dist):
    n = np.maximum(dist, 0)
    max_exact = N_BUCKETS // 2
    nf = np.maximum(n, 1).astype(np.float32)
    large = max_exact + (np.log(nf / np.float32(max_exact)) / np.float32(math.log(MAX_DISTANCE / max_exact))
                         * np.float32(N_BUCKETS - max_exact)).astype(np.int32)
    large = np.minimum(large, N_BUCKETS - 1)
    return np.where(n < max_exact, n, large).astype(np.int32)


N_PATTERNS = 4


def _bucket_patterns():
    i = np.arange(Q_BLOCK)[:, None]
    c = np.arange(LANE)[None, :]
    d0 = i - c
    d1 = i - c + Q_BLOCK
    dc = i - CMP_STRIDE * (c - (LANE - 8)) - (CMP_BLOCK - 1)
    pats = [np.where(d >= 0, _t5_bucket_np(d), -1) for d in (d0, d1, dc)]
    pats.append(np.where(i < c, N_BUCKETS - 1, -1))
    return np.stack([p.T for p in pats]).astype(np.int32)


def _bias_tiles_kernel(tab_ref, pat_ref, o_ref):
    h = pl.program_id(0)
    pat = pat_ref[...]
    far = tab_ref[N_BUCKETS - 1, h]
    acc = jnp.full(pat.shape, NEG, F32)
    for b in range(N_BUCKETS):
        acc = jnp.where(pat == b, (tab_ref[b, h] - far) * LOG2E, acc)
    o_ref[...] = acc


def _bias_tiles(rel_bias):
    pats = jnp.asarray(_bucket_patterns())
    return pl.pallas_call(
        _bias_tiles_kernel,
        grid=(N_HEADS,),
        in_specs=[pl.BlockSpec(memory_space=pltpu.SMEM),
                  pl.BlockSpec((N_PATTERNS, Q_BLOCK, LANE), lambda h: (0, 0, 0))],
        out_specs=pl.BlockSpec((None, N_PATTERNS, Q_BLOCK, LANE), lambda h: (h, 0, 0, 0)),
        out_shape=jax.ShapeDtypeStruct((N_HEADS, N_PATTERNS, Q_BLOCK, LANE), F32),
        compiler_params=_cparams(("arbitrary",)),
    )(rel_bias, pats)


def _with_features(qs, feat):
    reps = qs.shape[0] // feat.shape[0]
    return jnp.concatenate([qs, jnp.concatenate([feat] * reps, axis=0)], axis=1)


def _softmax_cols(blocks):
    m = blocks[0].max(axis=0, keepdims=True)
    for b in blocks[1:]:
        m = jnp.maximum(m, b.max(axis=0, keepdims=True))
    es = [jnp.exp2(b - m) for b in blocks]
    den = es[0].sum(axis=0, keepdims=True)
    for e in es[1:]:
        den = den + e.sum(axis=0, keepdims=True)
    return es, m, den


def _nsa_kernel(q_ref, gt_ref, kc_ref, vc_ref, vct_ref, ovl_ref, ovlt_ref, ks_ref, vsf_ref, vsn_ref,
                kw_ref, vwn_ref, tb_ref, o_ref, m_scr, acc_scr, sa_scr, sb_scr, cmp_o_scr, cmp_pf_scr, cmp_pn_scr,
                *, n_slc):
    qb = pl.program_id(1)
    s = qb * Q_BLOCK
    hpg = HEADS_PER_GROUP

    q_all = q_ref[...]
    qs = jnp.concatenate([q_all[:, h * LANE:(h + 1) * LANE] for h in range(hpg)], axis=0)
    cols = [slice(h * Q_BLOCK, (h + 1) * Q_BLOCK) for h in range(hpg)]
    lane_f = lax.broadcasted_iota(jnp.int32, (Q_BLOCK, LANE), 1)
    pad_feat = jnp.where(lane_f == LANE - 1, 1.0, 0.0).astype(BF16)
    q_pad = _with_features(qs, pad_feat)

    n_sub = NEAR // LANE
    row0 = pl.multiple_of(s, Q_BLOCK)

    n_pair = hpg // 2
    pair_rows = [slice(pr * 2 * Q_BLOCK, (pr + 1) * 2 * Q_BLOCK) for pr in range(n_pair)]
    half = [slice(0, Q_BLOCK), slice(Q_BLOCK, 2 * Q_BLOCK)]

    def run_stages(stages):
        pending = stages[0][0]()
        for i, (_, consume) in enumerate(stages):
            cur = pending
            if i + 1 < len(stages):
                pending = stages[i + 1][0]()
            consume(cur)

    def near_stages(k_ref, q_aug, vn_ref, oldest_pat, res):
        k_aug = k_ref[pl.ds(row0, NEAR), :]
        v_blocks = vn_ref[pl.ds(qb, n_sub)]
        v_t = jnp.concatenate([v_blocks[u] for u in range(n_sub)], axis=1)

        def stage(pr):
            def issue():
                return _dot_nt(k_aug, q_aug[pair_rows[pr], :])

            def consume(logits):
                ps = []
                for hh in range(2):
                    h = 2 * pr + hh
                    blocks = [logits[u * LANE:(u + 1) * LANE, half[hh]] for u in range(n_sub)]
                    blocks[n_sub - 1] = blocks[n_sub - 1] + tb_ref[h, 0]
                    blocks[n_sub - 2] = blocks[n_sub - 2] + tb_ref[h, 1]
                    if oldest_pat is not None:
                        blocks[0] = blocks[0] + tb_ref[h, oldest_pat]
                    m = blocks[0].max(axis=0, keepdims=True)
                    for b in blocks[1:]:
                        m = jnp.maximum(m, b.max(axis=0, keepdims=True))
                    ps.append(jnp.concatenate([jnp.exp2((b - m).astype(BF16)) for b in blocks], axis=0))
                    res["m"].append(m)
                res["o"].append(_dot(v_t, jnp.concatenate(ps, axis=1)))
            return issue, consume
        return [stage(pr) for pr in range(n_pair)]

    near0 = pl.multiple_of(qb * 8 + 8, 8)
    kn = kc_ref[pl.ds(near0, LANE), :].astype(BF16)
    vn_t = vc_ref[pl.ds(near0, LANE), :].T.astype(BF16)
    ovl_n_t = ovl_ref[pl.ds(near0, LANE), :].T.astype(BF16)
    far_feat = jnp.where(lane_f > qb - 16, 1.0, 0.0).astype(BF16)
    q_cmp = _with_features(qs, far_feat)
    t_row = s + lax.broadcasted_iota(jnp.int32, (1, Q_BLOCK), 1)
    row_ok = t_row >= CMP_BLOCK - 1
    n_half = kc_ref.shape[0] - CMP_PAD

    def cmp_branch(n_keys):
        kc = kc_ref[CMP_PAD:CMP_PAD + n_keys, :].astype(BF16)
        vc_t = vct_ref[:, CMP_PAD:CMP_PAD + n_keys]
        res = {"o": [], "pf": jnp.zeros((n_keys, Q_BLOCK), F32), "pn": jnp.zeros((LANE, Q_BLOCK), F32)}
        heads = hpg * LANE // min(n_keys, hpg * LANE // 2)

        def stage(first):
            q_rows = slice(first * Q_BLOCK, (first + heads) * Q_BLOCK)

            def issue():
                return _dot_nt(kc, q_cmp[q_rows, :]), _dot_nt(kn, q_pad[q_rows, :])

            def consume(logits):
                sf, sn = logits
                pfs, pns = [], []
                for hh in range(heads):
                    (ef, en), _, den = _softmax_cols([sf[:, cols[hh]], sn[:, cols[hh]] + tb_ref[first + hh, 2]])
                    inv = jnp.where(row_ok, 1.0 / den, 0.0)
                    pf = ef * inv
                    pn = en * inv
                    res["pf"] = res["pf"] + pf
                    res["pn"] = res["pn"] + pn
                    pfs.append(pf.astype(BF16))
                    pns.append(pn.astype(BF16))
                res["o"].append(_dot(vc_t, jnp.concatenate(pfs, axis=1)) + _dot(vn_t, jnp.concatenate(pns, axis=1)))
            return issue, consume

        run_stages([stage(first) for first in range(0, hpg, heads)])
        cmp_o_scr[...] = jnp.concatenate(res["o"], axis=1)
        cmp_pf_scr[0:n_keys, :] = res["pf"]
        if n_keys < n_half:
            cmp_pf_scr[n_keys:, :] = jnp.zeros((n_half - n_keys, Q_BLOCK), F32)
        cmp_pn_scr[...] = res["pn"]

    sizes = sorted({min(n_half, c) for c in (LANE, 2 * LANE)} | {n_half})
    lo = 0
    for idx, n_keys in enumerate(sizes):
        last = idx + 1 == len(sizes)
        hi = (n_keys + LANE - 8) // 8 + 1
        pl.when((qb >= lo) if last else ((qb >= lo) & (qb < hi)))(functools.partial(cmp_branch, n_keys))
        lo = hi

    blk = lax.broadcasted_iota(jnp.int32, (n_slc, Q_BLOCK), 0)
    blk_f = blk.astype(F32)
    cur = (s + lax.broadcasted_iota(jnp.int32, (n_slc, Q_BLOCK), 1)) >> 6
    forced = (blk == 0) | (blk == cur) | (blk == cur - 1)
    valid = blk <= cur
    topk = {}

    def importance_stage():
        def consume(_):
            psum_f, psum_n = cmp_pf_scr[...], cmp_pn_scr[...]
            ovl_t = ovlt_ref[:, CMP_PAD:]
            pf_hi = psum_f.astype(BF16)
            pf_lo = (psum_f - pf_hi.astype(F32)).astype(BF16)
            pn_hi = psum_n.astype(BF16)
            pn_lo = (psum_n - pn_hi.astype(F32)).astype(BF16)
            imp_t = _dot(ovl_t, pf_hi) + _dot(ovl_t, pf_lo) + _dot(ovl_n_t, pn_hi) + _dot(ovl_n_t, pn_lo)
            topk["score"] = jnp.where(valid & ~forced, imp_t, -BIG)
            topk["sel"] = jnp.where(forced, 1.0, 0.0)
        return (lambda: None), consume

    def topk_stage(rounds):
        def consume(_):
            score, sel_t = topk["score"], topk["sel"]
            for _r in range(rounds):
                top = jnp.max(score, axis=0, keepdims=True)
                first = jnp.min(jnp.where(score == top, blk_f, float(n_slc)), axis=0, keepdims=True)
                pick = blk_f == first
                sel_t = jnp.where(pick, 1.0, sel_t)
                score = jnp.where(pick, -2.0 * BIG, score)
            topk["score"], topk["sel"] = score, sel_t
        return (lambda: None), consume

    win_res = {"o": [], "m": []}
    win_stages = near_stages(kw_ref, q_pad, vwn_ref, 3, win_res)
    free_picks = max(min(N_SELECT, n_slc) - 3, 0)
    rounds = [free_picks // n_pair + (1 if pr < free_picks % n_pair else 0) for pr in range(n_pair)]
    stages = [importance_stage()]
    for pr in range(n_pair):
        stages += [win_stages[pr], topk_stage(rounds[pr])]
    run_stages(stages)
    o_win = jnp.concatenate(win_res["o"], axis=1)
    o_cmp = cmp_o_scr[...]
    sel_t = topk["sel"]
    drop_t = jnp.where(valid, 1.0 - sel_t, 1.0)
    near_blk = (s - WINDOW) >> 6
    drop_far_t = jnp.where(blk >= near_blk, 1.0, drop_t)
    q_near = _with_features(qs, drop_t.T.astype(BF16))
    q_far = _with_features(qs, drop_far_t.T.astype(BF16))

    def far_logits(t):
        r0 = pl.multiple_of(WINDOW + KV_TILE * t, KV_TILE)
        return _dot_nt(ks_ref[pl.ds(r0, KV_TILE), :], q_far)

    sa_scr[...] = far_logits(0)
    slc_res = {"o": [], "m": []}
    run_stages(near_stages(ks_ref, q_near, vsn_ref, None, slc_res))
    m_scr[...] = jnp.concatenate(slc_res["m"], axis=1)
    acc_scr[...] = jnp.concatenate(slc_res["o"], axis=1)

    gates = gt_ref[...]
    g_c, g_s, g_w = [jnp.concatenate([gates[3 * h + br:3 * h + br + 1, :] for h in range(hpg)], axis=1)
                     for br in range(N_BRANCH)]
    out_cw = g_c * o_cmp + (g_w / o_win[HEAD_DIM:HEAD_DIM + 1, :]) * o_win[:HEAD_DIM, :]

    n_far = (jnp.maximum(s - WINDOW, 0) + KV_TILE - 1) // KV_TILE

    n_tiles = vsf_ref.shape[0]

    def far_probs(st_ref):
        m_old = m_scr[...]
        sts = [st_ref[:, cols[h]] for h in range(hpg)]
        m_new = jnp.concatenate(
            [jnp.maximum(m_old[:, cols[h]], sts[h].max(axis=0, keepdims=True)) for h in range(hpg)], axis=1)
        p_t = jnp.concatenate(
            [jnp.exp2((sts[h] - m_new[:, cols[h]]).astype(BF16)) for h in range(hpg)], axis=1)
        m_scr[...] = m_new
        return p_t, jnp.exp2(m_old - m_new)

    def far_accumulate(t, p_t, alpha):
        acc_scr[...] = alpha * acc_scr[...] + _dot(vsf_ref[t], p_t)

    def far_body(i, carry):
        t0 = 2 * i
        sb_scr[...] = far_logits(t0 + 1)
        p_a, alpha_a = far_probs(sa_scr)
        sa_scr[...] = far_logits(jnp.minimum(t0 + 2, n_tiles - 1))
        far_accumulate(t0, p_a, alpha_a)
        p_b, alpha_b = far_probs(sb_scr)
        far_accumulate(t0 + 1, p_b, alpha_b)
        return carry

    lax.fori_loop(0, n_far // 2, far_body, 0)

    @pl.when(n_far % 2 == 1)
    def _():
        p_last, alpha_last = far_probs(sa_scr)
        far_accumulate(n_far - 1, p_last, alpha_last)

    out_t = out_cw + (g_s / acc_scr[HEAD_DIM:HEAD_DIM + 1, :]) * acc_scr[:HEAD_DIM, :]
    for h in range(hpg):
        o_ref[:, cols[h]] = out_t[:, cols[h]].T.astype(o_ref.dtype)


def _nsa_attention(q, gates_t, k_cmp, v_cmp, v_cmp_t, ovl, ovl_t, ks, vs_far, vs_near, kw, vw_near, bias_tiles):
    s = q.shape[0]
    n_qb = s // Q_BLOCK
    n_slc = s // SLC_BLOCK
    gw = HEADS_PER_GROUP * HEAD_DIM
    lanes = HEADS_PER_GROUP * Q_BLOCK

    def group_spec(a):
        zeros = (0,) * (a.ndim - 1)
        return pl.BlockSpec((None,) + a.shape[1:], lambda g, i: (g,) + zeros)

    def whole_spec(a):
        zeros = (0,) * a.ndim
        return pl.BlockSpec(a.shape, lambda g, i: zeros)

    return pl.pallas_call(
        functools.partial(_nsa_kernel, n_slc=n_slc),
        grid=(N_KV_GROUPS, n_qb),
        in_specs=[pl.BlockSpec((Q_BLOCK, gw), lambda g, i: (i, g)),
                  pl.BlockSpec((LANE, Q_BLOCK), lambda g, i: (g, i)),
                  group_spec(k_cmp), group_spec(v_cmp), group_spec(v_cmp_t),
                  whole_spec(ovl), whole_spec(ovl_t),
                  group_spec(ks), group_spec(vs_far), group_spec(vs_near),
                  group_spec(kw), group_spec(vw_near),
                  pl.BlockSpec((HEADS_PER_GROUP, N_PATTERNS, LANE, Q_BLOCK), lambda g, i: (g, 0, 0, 0))],
        out_specs=pl.BlockSpec((Q_BLOCK, gw), lambda g, i: (i, g)),
        out_shape=jax.ShapeDtypeStruct((s, N_KV_GROUPS * gw), BF16),
        scratch_shapes=[pltpu.VMEM((1, lanes), F32),
                        pltpu.VMEM((HEAD_DIM + SUM_ROWS, lanes), F32),
                        pltpu.VMEM((KV_TILE, lanes), F32),
                        pltpu.VMEM((KV_TILE, lanes), F32),
                        pltpu.VMEM((HEAD_DIM, lanes), F32),
                        pltpu.VMEM((k_cmp.shape[1] - CMP_PAD, Q_BLOCK), F32),
                        pltpu.VMEM((LANE, Q_BLOCK), F32)],
        compiler_params=_cparams(("arbitrary", "arbitrary")),
    )(q, gates_t, k_cmp, v_cmp, v_cmp_t, ovl, ovl_t, ks, vs_far, vs_near, kw, vw_near, bias_tiles)


def _cmp_mask_columns(n_half):
    assert n_half // 8 < LANE - 1
    out = np.zeros((CMP_PAD + n_half, LANE), np.float32)
    out[np.arange(CMP_PAD), LANE - 1] = NEG
    k = np.arange(n_half)
    out[CMP_PAD + k, k // 8] = NEG
    return out


def _slc_mask_columns(s):
    n_slc = s // SLC_BLOCK
    assert n_slc - 1 > (WINDOW + Q_BLOCK) // SLC_BLOCK
    out = np.zeros((WINDOW + s, n_slc), np.float32)
    out[np.arange(WINDOW), n_slc - 1] = NEG
    pos = np.arange(s)
    out[WINDOW + pos, pos // SLC_BLOCK] = NEG
    return out


def _pad_mask_columns(s):
    out = np.zeros((WINDOW + s, LANE), np.float32)
    out[np.arange(WINDOW), LANE - 1] = NEG
    return out


def _overlap_padded(n_half, n_slc):
    n_cmp = n_half - 1
    c0 = np.arange(n_cmp) * CMP_STRIDE
    s0 = np.arange(n_slc) * SLC_BLOCK
    lo = np.maximum(c0[:, None], s0[None, :])
    hi = np.minimum(c0[:, None] + CMP_BLOCK, s0[None, :] + SLC_BLOCK)
    ovl = np.maximum(hi - lo, 0).astype(np.float32) / CMP_BLOCK
    out = np.zeros((CMP_PAD + n_half, n_slc), np.float32)
    out[CMP_PAD:CMP_PAD + n_cmp] = ovl
    return out


def _row_tile(s):
    return min(1024, s)


def _ffn_ple(x, p, layer, b, norm_ffn, w_in, w_out, norm_ple, ple_w, ple_gate):
    s = x.shape[0]
    tm = _row_tile(s)
    tn = 512
    nj = FFN_DIM // tn
    act, w_out_bf = _norm_matmul(x, norm_ffn, (w_in, layer), (0, nj), tm=tm, tn=tn, nj=nj,
                                 epilogue=_ep_swiglu, out_dtype=BF16, cast=(w_out, layer))
    x = _matmul_res(act, w_out_bf, x, tm=tm, tn=512)
    aux = (p, ple_w)
    tn_ple = PROJ_TN
    aux_specs = (pl.BlockSpec((None, None, tm, PLE_DIM), lambda i, j: (layer, b, i, 0)),
                 pl.BlockSpec((None, PLE_DIM, tn_ple), lambda i, j: (layer, 0, j)))
    return _norm_matmul(x, norm_ple, (ple_gate, layer), (0,), tm=tm, tn=tn_ple, nj=D_MODEL // tn_ple,
                        epilogue=_ep_ple, out_dtype=F32, aux=aux, aux_specs=aux_specs)


def _gmlp_layer(x, layer, norm_mix, w_in, norm_v, w_s, b_s, w_out):
    return _gmlp(x, norm_mix, w_in, norm_v, w_s, jnp.transpose(b_s), w_out, layer, tm=_row_tile(x.shape[0]))


def _shared_kv(x, kv_norm, kv_w, k_norm, cmp_pe_k, cmp_pe_v, cmp_wk1, cmp_wk2, cmp_wv1, cmp_wv2):
    s = x.shape[0]
    tm = _row_tile(s)
    gw = N_KV_GROUPS * HEAD_DIM
    gains = jnp.stack([jnp.tile(k_norm[1], N_KV_GROUPS), jnp.ones((gw,), F32),
                       jnp.tile(k_norm[2], N_KV_GROUPS), jnp.ones((gw,), F32)])
    gains = jnp.broadcast_to(gains[:, None, :], (4, 8, gw))
    kvc, kvr = _norm_matmul_pair(
        x, kv_norm, kv_w, kv_w, gains, pl.BlockSpec((None, 8, gw), lambda i, j: (jnp.maximum(j - 1, 0), 0, 0)),
        tm=tm, tna=2 * gw, nja=1, col_a=0, tnb=gw, njb=4, col_b=2,
        ep_a=_ep_identity, ep_b=_ep_kv, dtype_a=F32, dtype_b=BF16)
    n_half = s // CMP_STRIDE
    pe = jnp.stack([cmp_pe_k, cmp_pe_v]).reshape(2, 2, CMP_STRIDE * HEAD_DIM)
    kv_cmp = _compress(kvc, pe, jnp.stack([cmp_wk1, cmp_wv1]), jnp.stack([cmp_wk2, cmp_wv2]), k_norm[0])
    kv_cmp = jnp.pad(kv_cmp, ((0, 0), (CMP_PAD, 0), (0, 0)))
    cmp_cols = jnp.broadcast_to(jnp.asarray(_cmp_mask_columns(n_half)), (N_KV_GROUPS, CMP_PAD + n_half, LANE))
    k_cmp = jnp.concatenate([kv_cmp[:N_KV_GROUPS], cmp_cols], axis=2)
    v_cmp = kv_cmp[N_KV_GROUPS:]
    v_cmp_t = jnp.transpose(v_cmp, (0, 2, 1)).astype(BF16)
    s_pad = WINDOW + s
    kvr = jnp.pad(kvr, ((WINDOW, 0), (0, 0))).reshape(s_pad, 4, N_KV_GROUPS, HEAD_DIM)
    kvr = jnp.transpose(kvr, (1, 2, 0, 3))
    slc_cols = jnp.asarray(_slc_mask_columns(s)).astype(BF16)
    pad_cols = jnp.asarray(_pad_mask_columns(s)).astype(BF16)
    ks = jnp.concatenate([kvr[0], jnp.broadcast_to(slc_cols, (N_KV_GROUPS,) + slc_cols.shape)], axis=2)
    kw = jnp.concatenate([kvr[2], jnp.broadcast_to(pad_cols, (N_KV_GROUPS,) + pad_cols.shape)], axis=2)

    def key_tiles_t(v, tile):
        v_t = jnp.transpose(v.reshape(N_KV_GROUPS, -1, tile, HEAD_DIM), (0, 1, 3, 2))
        return jnp.concatenate([v_t, jnp.ones(v_t.shape[:2] + (SUM_ROWS, tile), v_t.dtype)], axis=2)

    vs_far = key_tiles_t(kvr[1][:, WINDOW:], KV_TILE)
    return k_cmp, v_cmp, v_cmp_t, ks, vs_far, key_tiles_t(kvr[1], LANE), kw, key_tiles_t(kvr[3], LANE)


def _nsa_layer(x, layer, norm_mix, w_in, q_norm, w_out, kvs):
    s = x.shape[0]
    tm = _row_tile(s)
    nq = N_HEADS * HEAD_DIM
    scale = HEAD_DIM ** -0.5 * LOG2E
    q_gain = jnp.tile(q_norm * scale, PROJ_TN // HEAD_DIM).reshape(1, PROJ_TN)
    w_gate = w_in[layer, :, nq:].reshape(D_MODEL, N_KV_GROUPS, HEADS_PER_GROUP * N_BRANCH)
    w_gate = jnp.pad(w_gate, ((0, 0), (0, 0), (0, LANE - HEADS_PER_GROUP * N_BRANCH)))
    w_gate = w_gate.reshape(D_MODEL, N_KV_GROUPS * LANE)
    q, gates_t = _norm_matmul_pair(
        x, norm_mix, (w_in, layer), w_gate, q_gain, pl.BlockSpec((1, PROJ_TN), lambda i, j: (0, 0)),
        tm=tm, tna=PROJ_TN, nja=nq // PROJ_TN, col_a=0, tnb=N_KV_GROUPS * LANE, njb=1, col_b=0,
        ep_a=_ep_q, ep_b=_ep_gates_t, dtype_a=BF16, dtype_b=F32, transpose_b=True)
    o = _nsa_attention(q, gates_t, *kvs)
    return _matmul_res(o, (w_out, layer), x, tm=tm, tn=PROJ_TN)


def kernel(x, p, norm_mix, norm_ffn, norm_ple, a_w_in, a_norm_v, a_w_s, a_b_s, a_w_out, kv_norm, kv_w, k_norm, cmp_pe_k, cmp_pe_v, cmp_wk1, cmp_wk2, cmp_wv1, cmp_wv2, b_w_in, b_q_norm, b_w_out, rel_bias, ffn_w_in, ffn_w_out, ple_w, ple_gate):
    batch, s, d = x.shape
    depth = norm_mix.shape[0]
    n_a = a_w_in.shape[0]
    a_w_out, b_w_in, b_w_out, kv_w, ple_gate = (
        w.astype(BF16) for w in (a_w_out, b_w_in, b_w_out, kv_w, ple_gate))
    outs = []
    for b in range(batch):
        xb = x.reshape(s, d) if batch == 1 else x[b]
        kvs = None
        for i in range(depth):
            if i < n_a:
                xb = _gmlp_layer(xb, i, norm_mix[i], a_w_in, a_norm_v[i], a_w_s, a_b_s[i], a_w_out)
            else:
                j = i - n_a
                xb = _nsa_layer(xb, j, norm_mix[i], b_w_in, b_q_norm[j], b_w_out, kvs)
            xb = _ffn_ple(xb, p, i, b, norm_ffn[i], ffn_w_in, ffn_w_out,
                          norm_ple[i], ple_w, ple_gate)
            if i == n_a - 1:
                k_cmp, v_cmp, v_cmp_t, ks, vs_far, vs_near, kw, vw_near = _shared_kv(
                    xb, kv_norm, kv_w, k_norm, cmp_pe_k, cmp_pe_v, cmp_wk1, cmp_wk2, cmp_wv1, cmp_wv2)
                ovl = _overlap_padded(s // CMP_STRIDE, s // SLC_BLOCK)
                kvs = (k_cmp, v_cmp, v_cmp_t, jnp.asarray(ovl), jnp.asarray(ovl.T).astype(BF16),
                       ks, vs_far, vs_near, kw, vw_near, _bias_tiles(rel_bias))
        outs.append(xb)
    return outs[0].reshape(1, s, d) if batch == 1 else jnp.stack(outs)
```
